```python
import math
import jax, jax.numpy as jnp
from jax import lax
import numpy as np

D_MODEL = 2048
BATCH = 4
SEQ = 2048
DEPTH = 4

GRID_W = 64
CTX_LEN = 256
D_MIX = D_MODEL
D_MLSTM = D_MIX // 2
MLSTM_HEADS = 4
MLSTM_DV = D_MLSTM // MLSTM_HEADS
MLSTM_DQK = MLSTM_DV // 2
CHUNK = 64
D_CONV = D_MIX - D_MLSTM
CONV_GROUPS = 8
CONV_GROUP_W = D_CONV // CONV_GROUPS
D_CONV_H = (CONV_GROUPS // 2) * CONV_GROUP_W
CONV_W = 3
N_EXPERTS = 16
N_GROUPS = 4
EXPERTS_PER_GROUP = N_EXPERTS // N_GROUPS
TOP_K = 2
D_FF = D_MODEL // 2
ALPHA = (2 * DEPTH) ** 0.25
BETA = (8 * DEPTH) ** -0.25
LN_EPS = 1e-6
SPLIT_SIZES = (MLSTM_HEADS * MLSTM_DQK, MLSTM_HEADS * MLSTM_DQK, D_MLSTM, 2 * MLSTM_HEADS, 2 * MLSTM_HEADS,
               D_MLSTM, D_CONV, D_CONV, D_CONV)
D_STATE_COLS = 2 * MLSTM_HEADS * MLSTM_DQK + D_MLSTM + 4 * MLSTM_HEADS
D_IN = D_STATE_COLS + D_MLSTM + 3 * D_CONV

kernel_name = "hybrid_mlstm_shortconv_moe_dit"


def _split(u, sizes):
    return jnp.split(u, np.cumsum(sizes)[:-1].tolist(), axis=-1)


def _ln(x):
    xf = x.astype(jnp.float32)
    xc = xf - xf.mean(-1, keepdims=True)
    return xc * lax.rsqrt((xc * xc).mean(-1, keepdims=True) + LN_EPS)


def _modulate(x, shift, scale):
    return (_ln(x) * (1 + scale) + shift).astype(x.dtype)


def _post_norm(x, g, b):
    return (_ln(x) * g + b).astype(x.dtype)


def _mlstm_prep(q, k, v, ig, fg, b_i, b_f):
    Bn, T, _ = q.shape
    heads = lambda a: a.reshape(Bn, T, MLSTM_HEADS, -1).transpose(0, 2, 1, 3).astype(jnp.float32)
    gates = lambda a: a.astype(jnp.float32).reshape(Bn, T, 2, MLSTM_HEADS).transpose(2, 0, 3, 1)
    return (heads(q), heads(k) * MLSTM_DQK ** -0.5, heads(v),
            gates(ig + b_i), jax.nn.log_sigmoid(gates(fg + b_f)))


def _mlstm_scan(q, k, v, ig, lf, state):
    Bn, H, T, _ = q.shape
    nc = T // CHUNK

    def to_chunks(a):
        return jnp.moveaxis(a.reshape(a.shape[:2] + (nc, CHUNK) + a.shape[3:]), 2, 0)

    lower = jnp.tril(jnp.ones((CHUNK, CHUNK), dtype=bool))

    def step(carry, inp):
        C, n, m = carry
        qc, kc, vc, ic, fc = inp
        b = jnp.cumsum(fc, axis=-1)
        dmat = b[..., :, None] - b[..., None, :] + ic[..., None, :]
        dmat = jnp.where(lower, dmat, -jnp.inf)
        m_t = jnp.maximum(b + m[..., None], dmat.max(-1))
        inter = jnp.exp(b + m[..., None] - m_t)
        w = jnp.exp(dmat - m_t[..., None])
        s = jnp.einsum('bhtk,bhsk->bhts', qc, kc) * w
        num = inter[..., None] * jnp.einsum('bhvk,bhtk->bhtv', C, qc) + jnp.einsum('bhts,bhsv->bhtv', s, vc)
        den = inter * jnp.einsum('bhk,bhtk->bht', n, qc) + s.sum(-1)
        h = num / jnp.maximum(jnp.abs(den), jnp.exp(-m_t))[..., None]
        decay = inter[..., -1]
        wg = w[..., -1, :]
        C = decay[..., None, None] * C + jnp.einsum('bhs,bhsv,bhsk->bhvk', wg, vc, kc)
        n = decay[..., None] * n + jnp.einsum('bhs,bhsk->bhk', wg, kc)
        return (C, n, m_t[..., -1]), h

    state, h = lax.scan(step, state, tuple(to_chunks(a) for a in (q, k, v, ig, lf)))
    return jnp.moveaxis(h, 0, 2).reshape(Bn, H, T, -1), state


def _mlstm_bidir(ctx_in, lat_in):
    qc, kc, vc, ic, fc = ctx_in
    ql, kl, vl, il, fl = lat_in
    Bn = qc.shape[0]
    zero = (jnp.zeros((Bn, MLSTM_HEADS, MLSTM_DV, MLSTM_DQK), jnp.float32),
            jnp.zeros((Bn, MLSTM_HEADS, MLSTM_DQK), jnp.float32),
            jnp.zeros((Bn, MLSTM_HEADS), jnp.float32))
    rev = lambda a: jnp.flip(a, axis=2)
    hcf, st_f = _mlstm_scan(qc, kc, vc, ic[0], fc[0], zero)
    hcb, st_b = _mlstm_scan(rev(qc), rev(kc), rev(vc), rev(ic[1]), rev(fc[1]), zero)
    hlf, _ = _mlstm_scan(ql, kl, vl, il[0], fl[0], st_f)
    hlb, _ = _mlstm_scan(rev(ql), rev(kl), rev(vl), rev(il[1]), rev(fl[1]), st_b)
    return hcf + rev(hcb), hlf + rev(hlb)


def _mlstm_out(h, o, g):
    Bn, H, T, DV = h.shape
    hn = h * lax.rsqrt(jnp.mean(h * h, -1, keepdims=True) + LN_EPS)
    hn = hn.transpose(0, 2, 1, 3).reshape(Bn, T, H * DV)
    return (hn * g * jax.nn.sigmoid(o.astype(jnp.float32))).astype(o.dtype)


def _conv3(z, w, axis):
    n = z.shape[axis]
    pad = [(0, 0)] * z.ndim
    pad[axis] = (1, 1)
    zp = jnp.pad(z, pad)
    sl = lambda s: lax.slice_in_dim(zp, s, s + n, axis=axis)
    return w[0] * sl(0) + w[1] * sl(1) + w[2] * sl(2)


def _conv_latent(z, w, rows):
    Bn, S, Cd = z.shape
    zg = z.reshape(Bn, rows, GRID_W, Cd)
    yh = _conv3(zg[..., :D_CONV_H], w[:, :D_CONV_H], axis=2)
    yv = _conv3(zg[..., D_CONV_H:], w[:, D_CONV_H:], axis=1)
    return jnp.concatenate([yh, yv], -1).reshape(Bn, S, Cd)


def _moe(h, w_router, b_router, w1, w3, w2):
    s = jax.nn.sigmoid((h @ w_router).astype(jnp.float32))
    sb = s + b_router.astype(jnp.float32)
    gscore = lax.top_k(sb.reshape(-1, N_GROUPS, EXPERTS_PER_GROUP), 2)[0].sum(-1)
    gsel = jnp.argmax(gscore, axis=-1)
    in_group = (jnp.arange(N_EXPERTS) // EXPERTS_PER_GROUP)[None, :] == gsel[:, None]
    _, idx = lax.top_k(jnp.where(in_group, sb, -jnp.inf), TOP_K)
    sel = jnp.take_along_axis(s, idx, axis=-1)
    wts = sel / sel.sum(-1, keepdims=True)
    gates = (jax.nn.one_hot(idx, N_EXPERTS, dtype=jnp.float32) * wts[..., None]).sum(1).astype(h.dtype)
    out = jnp.zeros_like(h)
    for e in range(N_EXPERTS):
        a = jax.nn.silu(h @ w1[e]) * (h @ w3[e])
        out = out + gates[:, e:e + 1] * (a @ w2[e])
    return out


def setup_inputs(seed: int = 0) -> dict:
    key = jax.random.key(seed)
    ks = jax.random.split(key, 24)
    f32 = jnp.float32
    nrm = lambda k, shape, scale: jax.random.normal(k, shape, f32) * scale
    lin = jnp.linspace(3.0, 6.0, MLSTM_HEADS, dtype=f32)
    return {
        "x": nrm(ks[0], (BATCH, SEQ, D_MODEL), 1.0),
        "c": nrm(ks[1], (BATCH, D_MODEL), 1.0),
        "ctx": nrm(ks[2], (BATCH, CTX_LEN, D_MODEL), 1.0),
        "c_ctx": nrm(ks[3], (D_MODEL,), 1.0),
        "w_ada": nrm(ks[4], (DEPTH, D_MODEL, 6 * D_MODEL), 0.5 * D_MODEL ** -0.5),
        "b_ada": nrm(ks[5], (DEPTH, 6 * D_MODEL), 0.02),
        "w_in": nrm(ks[6], (DEPTH, D_MODEL, D_IN), D_MODEL ** -0.5),
        "b_igate": nrm(ks[7], (DEPTH, 2 * MLSTM_HEADS), 0.1),
        "b_fgate": jnp.concatenate([lin, lin])[None, :] + nrm(ks[8], (DEPTH, 2 * MLSTM_HEADS), 0.1),
        "mh_norm_g": 1.0 + nrm(ks[9], (DEPTH, D_MLSTM), 0.02),
        "conv_w": nrm(ks[10], (DEPTH, CONV_W, D_CONV), CONV_W ** -0.5),
        "conv_b": nrm(ks[11], (DEPTH, D_CONV), 0.02),
        "w_out": nrm(ks[12], (DEPTH, D_MIX, D_MODEL), BETA * D_MIX ** -0.5),
        "ln1_g": 1.0 + nrm(ks[13], (DEPTH, D_MODEL), 0.02),
        "ln1_b": nrm(ks[14], (DEPTH, D_MODEL), 0.02),
        "w_router": nrm(ks[15], (D_MODEL, N_EXPERTS), D_MODEL ** -0.5),
        "b_router": nrm(ks[16], (N_EXPERTS,), 0.01),
        "w1": nrm(ks[17], (DEPTH, N_EXPERTS, D_MODEL, D_FF), D_MODEL ** -0.5),
        "w3": nrm(ks[18], (DEPTH, N_EXPERTS, D_MODEL, D_FF), D_MODEL ** -0.5),
        "w2": nrm(ks[19], (DEPTH, N_EXPERTS, D_FF, D_MODEL), BETA * D_FF ** -0.5),
        "ln2_g": 1.0 + nrm(ks[20], (DEPTH, D_MODEL), 0.02),
        "ln2_b": nrm(ks[21], (DEPTH, D_MODEL), 0.02),
    }


def reference(x, c, ctx, c_ctx, w_ada, b_ada, w_in, b_igate, b_fgate, mh_norm_g, conv_w, conv_b,
              w_out, ln1_g, ln1_b, w_router, b_router, w1, w3, w2, ln2_g, ln2_b):
    rows = x.shape[1] // GRID_W
    cond = jax.nn.silu(jnp.concatenate([c, c_ctx[None, :]], axis=0))
    xl, xc = x, ctx
    for l in range(DEPTH):
        last = l == DEPTH - 1
        mod = (cond @ w_ada[l] + b_ada[l]).reshape(cond.shape[0], 6, 1, D_MODEL)
        ml, mc = mod[:-1], mod[-1:]

        hl = _modulate(xl, ml[:, 0], ml[:, 1])
        hc = _modulate(xc, mc[:, 0], mc[:, 1])
        ql, kl, vl, il, fl, ol, ul, bgl, cgl = _split(hl @ w_in[l], SPLIT_SIZES)
        if last:
            pc = _split(hc @ w_in[l][:, :D_STATE_COLS], SPLIT_SIZES[:5])
        else:
            pc = _split(hc @ w_in[l], SPLIT_SIZES)
        ctx_m = _mlstm_prep(pc[0], pc[1], pc[2], pc[3], pc[4], b_igate[l], b_fgate[l])
        lat_m = _mlstm_prep(ql, kl, vl, il, fl, b_igate[l], b_fgate[l])
        h_ctx, h_lat = _mlstm_bidir(ctx_m, lat_m)
        m_lat = _mlstm_out(h_lat, ol, mh_norm_g[l])
        y_lat = bgl * (_conv_latent(cgl * ul, conv_w[l], rows) + conv_b[l])
        out_l = jnp.concatenate([m_lat, y_lat], axis=-1) @ w_out[l]
        xl = _post_norm(ALPHA * xl + ml[:, 2] * out_l, ln1_g[l], ln1_b[l])
        if not last:
            oc, uc, bgc, cgc = pc[5], pc[6], pc[7], pc[8]
            m_ctx = _mlstm_out(h_ctx, oc, mh_norm_g[l])
            y_ctx = bgc * (_conv3(cgc * uc, conv_w[l], axis=1) + conv_b[l])
            out_c = jnp.concatenate([m_ctx, y_ctx], axis=-1) @ w_out[l]
            xc = _post_norm(ALPHA * xc + mc[:, 2] * out_c, ln1_g[l], ln1_b[l])

        hl = _modulate(xl, ml[:, 3], ml[:, 4]).reshape(-1, D_MODEL)
        n_lat = hl.shape[0]
        if last:
            tokens = hl
        else:
            hc = _modulate(xc, mc[:, 3], mc[:, 4]).reshape(-1, D_MODEL)
            tokens = jnp.concatenate([hl, hc], axis=0)
        moe = _moe(tokens, w_router, b_router, w1[l], w3[l], w2[l])
        xl = _post_norm(ALPHA * xl + ml[:, 5] * moe[:n_lat].reshape(xl.shape), ln2_g[l], ln2_b[l])
        if not last:
            xc = _post_norm(ALPHA * xc + mc[:, 5] * moe[n_lat:].reshape(xc.shape), ln2_g[l], ln2_b[l])
    return xl
```

```python
import functools

import jax
import jax.numpy as jnp
from jax import lax
from jax.experimental import pallas as pl
from jax.experimental.pallas import tpu as pltpu

f32 = jnp.float32
bf16 = jnp.bfloat16
i32 = jnp.int32
u32 = jnp.uint32

D_MODEL = 2048
BATCH = 4
SEQ = 2048
DEPTH = 4
GRID_W = 64
CTX_LEN = 256
D_MLSTM = 1024
HEADS = 4
DV = 256
DQK = 128
CHUNK = 64
D_CONV = 1024
D_CONV_H = 512
N_EXPERTS = 16
N_GROUPS = 4
EPG = 4
D_FF = 1024
ALPHA = (2 * DEPTH) ** 0.25
LN_EPS = 1e-6
QK_SCALE = DQK ** -0.5

N_LAT = BATCH * SEQ
N_CTX = BATCH * CTX_LEN
N_ALL = N_LAT + N_CTX
D_PROJ = 6144
N_GATE = 16
HALF = D_MODEL // 2

TM_IN = 1024
TN_IN = 512
TM_MIX = 256
TM_EXP = 256
TM_CMB = 256
VMEM_LIMIT = 56 * 1024 * 1024

HIGHEST = lax.Precision.HIGHEST


def _sigmoid(x):
    return 1.0 / (1.0 + jnp.exp(-x))


def _log_sigmoid(x):
    return jnp.minimum(x, 0.0) - jnp.log1p(jnp.exp(-jnp.abs(x)))


def _ln(x):
    mu = jnp.mean(x, axis=-1, keepdims=True)
    xc = x - mu
    var = jnp.mean(xc * xc, axis=-1, keepdims=True)
    return xc * lax.rsqrt(var + LN_EPS)


def _mod_row(i, tm):
    return jnp.minimum((i * tm) // SEQ, BATCH)


def _params(sem, vmem=VMEM_LIMIT):
    return pltpu.CompilerParams(dimension_semantics=sem, vmem_limit_bytes=vmem)


def _ada_kernel(c_ref, w_ref, b_ref, o_ref):
    c = c_ref[...]
    cond = c * _sigmoid(c)
    o_ref[...] = jnp.dot(cond, w_ref[...], preferred_element_type=f32) + b_ref[...]


def _ada(cond_raw, w_ada, b_ada):
    tn = 1024
    n = 6 * D_MODEL
    return pl.pallas_call(
        _ada_kernel,
        grid=(DEPTH, n // tn),
        in_specs=[
            pl.BlockSpec((8, D_MODEL), lambda l, j: (0, 0)),
            pl.BlockSpec((None, D_MODEL, tn), lambda l, j: (l, 0, j)),
            pl.BlockSpec((None, 1, tn), lambda l, j: (l, 0, j)),
        ],
        out_specs=pl.BlockSpec((None, 8, tn), lambda l, j: (l, 0, j)),
        out_shape=jax.ShapeDtypeStruct((DEPTH, 8, n), f32),
        compiler_params=_params(("parallel", "parallel")),
        name="ada",
    )(cond_raw, w_ada, b_ada.reshape(DEPTH, 1, n))


def _inproj_kernel(x_ref, mod_ref, w_ref, wg_ref, o_ref, g_ref, h_scr, *, slab):
    j = pl.program_id(1)

    @pl.when(j == 0)
    def _():
        shift = mod_ref[0:1, :]
        scale = mod_ref[1:2, :]

        def body(s, c):
            r0 = pl.multiple_of(s * slab, slab)
            h = _ln(x_ref[pl.ds(r0, slab), :]) * (1.0 + scale) + shift
            hb = h.astype(bf16)
            h_scr[pl.ds(r0, slab), :] = hb
            g_ref[pl.ds(r0, slab), :] = jnp.dot(hb, wg_ref[...], preferred_element_type=f32)
            return c

        lax.fori_loop(0, TM_IN // slab, body, 0)

    o_ref[...] = jnp.dot(h_scr[...], w_ref[...], preferred_element_type=f32).astype(o_ref.dtype)


def _inproj(x, mod_l, w_main, w_gate):
    m = x.shape[0]
    return pl.pallas_call(
        functools.partial(_inproj_kernel, slab=128),
        grid=(m // TM_IN, D_PROJ // TN_IN),
        in_specs=[
            pl.BlockSpec((TM_IN, D_MODEL), lambda i, j: (i, 0)),
            pl.BlockSpec((None, 6, D_MODEL), lambda i, j: (_mod_row(i, TM_IN), 0, 0)),
            pl.BlockSpec((D_MODEL, TN_IN), lambda i, j: (0, j)),
            pl.BlockSpec((D_MODEL, 128), lambda i, j: (0, 0)),
        ],
        out_specs=[
            pl.BlockSpec((TM_IN, TN_IN), lambda i, j: (i, j)),
            pl.BlockSpec((TM_IN, 128), lambda i, j: (i, 0)),
        ],
        out_shape=[
            jax.ShapeDtypeStruct((m, D_PROJ), bf16),
            jax.ShapeDtypeStruct((m, 128), f32),
        ],
        scratch_shapes=[pltpu.VMEM((TM_IN, D_MODEL), bf16)],
        compiler_params=_params(("parallel", "arbitrary")),
        name="inproj",
    )(x, mod_l, w_main, w_gate)


def _mlstm_chunk(q, k, v, fc_raw, ic_raw, fr_raw, ir_raw, b_i, b_f, ct_ref, n_ref, d, m, fwd):
    L = CHUNK
    lf_c = _log_sigmoid(fc_raw + b_f)
    i_c = ic_raw + b_i
    lf_r = _log_sigmoid(fr_raw + b_f)
    i_r = ir_raw + b_i
    rr = lax.broadcasted_iota(i32, (L, L), 0)
    cc = lax.broadcasted_iota(i32, (L, L), 1)
    lo = rr >= cc
    up = rr <= cc
    mask = lo if fwd else up
    a_mat = mask.astype(f32)
    a_t = (up if fwd else lo).astype(f32)
    bcol = jnp.dot(a_mat, jnp.broadcast_to(lf_c, (L, L)), precision=HIGHEST,
                   preferred_element_type=f32)
    brow = jnp.dot(jnp.broadcast_to(lf_r, (L, L)), a_t, precision=HIGHEST,
                   preferred_element_type=f32)
    dm = jnp.where(mask, bcol - brow + i_r, -jnp.inf)
    mloc = jnp.max(dm, axis=1, keepdims=True)
    wloc = jnp.exp(dm - mloc)
    b1 = bcol[:, 0:1]
    m_t = jnp.maximum(b1 + m, mloc)
    inter = jnp.exp(b1 + m - m_t)
    a = jnp.exp(mloc - m_t)
    qk = lax.dot_general(q, k, (((1,), (1,)), ((), ())), preferred_element_type=f32) * QK_SCALE
    sloc = qk * wloc
    ct = ct_ref[d]
    n = n_ref[d]
    num = inter * jnp.dot(q, ct.astype(bf16), preferred_element_type=f32) \
        + a * jnp.dot(sloc.astype(bf16), v, preferred_element_type=f32)
    den = inter * jnp.sum(q.astype(f32) * n, axis=1, keepdims=True) \
        + a * jnp.sum(sloc, axis=1, keepdims=True)
    hout = num / jnp.maximum(jnp.abs(den), jnp.exp(-m_t))
    last = L - 1 if fwd else 0
    b_last = b1[last:last + 1, :]
    wl = jnp.exp(b_last - b1 + i_c - mloc[last:last + 1, :])
    kw = k.astype(f32) * (wl * QK_SCALE)
    u = lax.dot_general(kw.astype(bf16), v, (((0,), (0,)), ((), ())), preferred_element_type=f32)
    decay = inter[last:last + 1, :]
    a_l = a[last:last + 1, :]
    ct_ref[d] = decay * ct + a_l * u
    n_ref[d] = decay * n + a_l * jnp.sum(kw, axis=0, keepdims=True)
    return hout, m_t[last:last + 1, :]


def _mlstm_kernel(bias_ref, ql_ref, kl_ref, vl_ref, ol_ref, qc_ref, kc_ref, vc_ref, oc_ref,
                  gcl_ref, grl_ref, gcc_ref, grc_ref, gain_ref, *rest, has_ctx_out):
    if has_ctx_out:
        ml_ref, mc_ref, ct_ref, n_ref, hfl, hbl, hfc, hbc = rest
    else:
        ml_ref, ct_ref, n_ref, hfl, hbl = rest
        mc_ref = hfc = hbc = None
    b = pl.program_id(0)
    h = pl.program_id(1)
    L = CHUNK
    bi = (bias_ref[h], bias_ref[HEADS + h])
    bf = (bias_ref[2 * HEADS + h], bias_ref[3 * HEADS + h])
    ct_ref[...] = jnp.zeros_like(ct_ref)
    n_ref[...] = jnp.zeros_like(n_ref)

    def run(q_ref, k_ref, v_ref, gc_ref, gr_ref, c_base, nchunks, hf, hb, carry):
        def body(j, carry):
            m_f, m_b = carry
            outs = []
            for d, fwd in ((0, True), (1, False)):
                c = j if fwd else nchunks - 1 - j
                r0 = pl.multiple_of(c * L, L)
                q = q_ref[pl.ds(r0, L), :]
                k = k_ref[pl.ds(r0, L), :]
                v = v_ref[pl.ds(r0, L), :]
                gcol = gc_ref[c_base + c]
                ic = gcol[:, d:d + 1]
                fc = gcol[:, 2 + d:3 + d]
                ir = gr_ref[d, pl.ds(c_base + c, 1), :]
                fr = gr_ref[2 + d, pl.ds(c_base + c, 1), :]
                hout, m_new = _mlstm_chunk(q, k, v, fc, ic, fr, ir, bi[d], bf[d], ct_ref, n_ref, d,
                                           m_f if fwd else m_b, fwd)
                if hf is not None:
                    (hf if fwd else hb)[pl.ds(r0, L), :] = hout
                outs.append(m_new)
            return tuple(outs)

        return lax.fori_loop(0, nchunks, body, carry)

    zero = jnp.zeros((1, 1), f32)
    carry = run(qc_ref, kc_ref, vc_ref, gcc_ref, grc_ref, b * (CTX_LEN // L), CTX_LEN // L, hfc, hbc,
                (zero, zero))
    run(ql_ref, kl_ref, vl_ref, gcl_ref, grl_ref, 0, SEQ // L, hfl, hbl, carry)

    gain = gain_ref[...]

    def finish(hf, hb, o_ref, out_ref, rows):
        slab = 256

        def ep(s, c):
            r0 = pl.multiple_of(s * slab, slab)
            hs = hf[pl.ds(r0, slab), :] + hb[pl.ds(r0, slab), :]
            r = lax.rsqrt(jnp.mean(hs * hs, axis=-1, keepdims=True) + LN_EPS)
            o = o_ref[pl.ds(r0, slab), :].astype(f32)
            out_ref[pl.ds(r0, slab), :] = (hs * r * gain * _sigmoid(o)).astype(out_ref.dtype)
            return c

        lax.fori_loop(0, rows // slab, ep, 0)

    finish(hfl, hbl, ol_ref, ml_ref, SEQ)
    if has_ctx_out:
        finish(hfc, hbc, oc_ref, mc_ref, CTX_LEN)


def _mlstm(proj, gates, bias, gain, has_ctx_out):
    L = CHUNK
    g = gates[:, :N_GATE]
    g4 = jnp.stack([g[:, 0:4], g[:, 4:8], g[:, 8:12], g[:, 12:16]], axis=-1)
    gh = jnp.transpose(g4, (1, 0, 2))
    gcol_l = gh[:, :N_LAT].reshape(HEADS, N_LAT // L, L, 4)
    gcol_c = gh[:, N_LAT:].reshape(HEADS, N_CTX // L, L, 4)
    gr = jnp.transpose(g4, (1, 2, 0))
    grow_l = gr[:, :, :N_LAT].reshape(HEADS, 4, N_LAT // L, L)
    grow_c = gr[:, :, N_LAT:].reshape(HEADS, 4, N_CTX // L, L)

    lat_rb = lambda b, h: b
    ctx_rb = lambda b, h: N_LAT // CTX_LEN + b
    in_specs = [
        pl.BlockSpec(memory_space=pltpu.SMEM),
        pl.BlockSpec((SEQ, DQK), lambda b, h: (lat_rb(b, h), h)),
        pl.BlockSpec((SEQ, DQK), lambda b, h: (lat_rb(b, h), HEADS + h)),
        pl.BlockSpec((SEQ, DV), lambda b, h: (lat_rb(b, h), HEADS + h)),
        pl.BlockSpec((SEQ, DV), lambda b, h: (lat_rb(b, h), 2 * HEADS + h)),
        pl.BlockSpec((CTX_LEN, DQK), lambda b, h: (ctx_rb(b, h), h)),
        pl.BlockSpec((CTX_LEN, DQK), lambda b, h: (ctx_rb(b, h), HEADS + h)),
        pl.BlockSpec((CTX_LEN, DV), lambda b, h: (ctx_rb(b, h), HEADS + h)),
        pl.BlockSpec((CTX_LEN, DV), lambda b, h: (ctx_rb(b, h), 2 * HEADS + h)),
        pl.BlockSpec((None, SEQ // L, L, 4), lambda b, h: (h, b, 0, 0)),
        pl.BlockSpec((None, 4, SEQ // L, L), lambda b, h: (h, 0, b, 0)),
        pl.BlockSpec((None, N_CTX // L, L, 4), lambda b, h: (h, 0, 0, 0)),
        pl.BlockSpec((None, 4, N_CTX // L, L), lambda b, h: (h, 0, 0, 0)),
        pl.BlockSpec((1, DV), lambda b, h: (0, h)),
    ]
    out_specs = [pl.BlockSpec((SEQ, DV), lambda b, h: (b, h))]
    out_shape = [jax.ShapeDtypeStruct((N_LAT, D_MLSTM), bf16)]
    scratch = [pltpu.VMEM((2, DQK, DV), f32), pltpu.VMEM((2, 1, DQK), f32),
               pltpu.VMEM((SEQ, DV), f32), pltpu.VMEM((SEQ, DV), f32)]
    if has_ctx_out:
        out_specs.append(pl.BlockSpec((CTX_LEN, DV), lambda b, h: (b, h)))
        out_shape.append(jax.ShapeDtypeStruct((N_CTX, D_MLSTM), bf16))
        scratch += [pltpu.VMEM((CTX_LEN, DV), f32), pltpu.VMEM((CTX_LEN, DV), f32)]
    res = pl.pallas_call(
        functools.partial(_mlstm_kernel, has_ctx_out=has_ctx_out),
        grid=(BATCH, HEADS),
        in_specs=in_specs,
        out_specs=out_specs,
        out_shape=out_shape,
        scratch_shapes=scratch,
        compiler_params=_params(("parallel", "parallel")),
        name="mlstm",
    )(bias, proj, proj, proj, proj, proj, proj, proj, proj, gcol_l, grow_l, gcol_c, grow_c, gain)
    return res if has_ctx_out else (res[0], None)


def _top2_rows(vals):
    best = vals[0]
    bi = jnp.zeros(best.shape, i32)
    for j in range(1, len(vals)):
        take = vals[j] > best
        best = jnp.where(take, vals[j], best)
        bi = jnp.where(take, j, bi)
    sec = None
    si = None
    for j in range(len(vals)):
        cand = jnp.where(bi == j, -jnp.inf, vals[j])
        if sec is None:
            sec, si = cand, jnp.zeros(best.shape, i32)
        else:
            take = cand > sec
            sec = jnp.where(take, cand, sec)
            si = jnp.where(take, j, si)
    return bi, si


def _mix_kernel(x_ref, mod_ref, ml_ref, mc_ref, u_ref, bg_ref, cg_ref, ut_ref, ct_ref, ub_ref, cb_ref,
                cw_ref, cbias_ref, wo_ref, g1_ref, b1_ref, wr_ref, br_ref,
                xo_ref, hp_ref, idx_ref, wt_ref, m_scr, y_scr, *, n_lat_blocks, has_ctx):
    tm = TM_MIX
    i = pl.program_id(0)
    gate = mod_ref[2:3, :]
    shift2 = mod_ref[3:4, :]
    scale2 = mod_ref[4:5, :]
    cw = cw_ref[...]
    cbias = cbias_ref[...]
    row = lax.broadcasted_iota(i32, (tm, 1), 0)

    def shifted(z, first, last):
        prev = jnp.where(first, 0.0, pltpu.roll(z, 1, 0))
        nxt = jnp.where(last, 0.0, pltpu.roll(z, tm - 1, 0))
        return prev, nxt

    def lat_branch():
        z = cg_ref[...].astype(f32) * u_ref[...].astype(f32)
        col = row % GRID_W
        zh = z[:, :D_CONV_H]
        prev, nxt = shifted(zh, col == 0, col == GRID_W - 1)
        yh = cw[0:1, :D_CONV_H] * prev + cw[1:2, :D_CONV_H] * zh + cw[2:3, :D_CONV_H] * nxt
        bpb = SEQ // tm
        top_ok = (i % bpb != 0).astype(f32)
        bot_ok = (i % bpb != bpb - 1).astype(f32)
        zt = ct_ref[...].astype(f32) * ut_ref[...].astype(f32) * top_ok
        zb = cb_ref[...].astype(f32) * ub_ref[...].astype(f32) * bot_ok
        zv = z[:, D_CONV_H:]
        zext = jnp.concatenate([zt, zv, zb], axis=0)
        yv = cw[0:1, D_CONV_H:] * zext[0:tm] + cw[1:2, D_CONV_H:] * zv \
            + cw[2:3, D_CONV_H:] * zext[2 * GRID_W:2 * GRID_W + tm]
        y = jnp.concatenate([yh, yv], axis=1) + cbias
        y_scr[...] = (bg_ref[...].astype(f32) * y).astype(bf16)
        m_scr[...] = ml_ref[...]

    def ctx_branch():
        z = cg_ref[...].astype(f32) * u_ref[...].astype(f32)
        pos = row % CTX_LEN
        prev, nxt = shifted(z, pos == 0, pos == CTX_LEN - 1)
        y = cw[0:1, :] * prev + cw[1:2, :] * z + cw[2:3, :] * nxt + cbias
        y_scr[...] = (bg_ref[...].astype(f32) * y).astype(bf16)
        m_scr[...] = mc_ref[...]

    if has_ctx:
        pl.when(i < n_lat_blocks)(lat_branch)
        pl.when(i >= n_lat_blocks)(ctx_branch)
    else:
        lat_branch()

    out = jnp.dot(m_scr[...], wo_ref[0:D_MLSTM, :], preferred_element_type=f32) \
        + jnp.dot(y_scr[...], wo_ref[D_MLSTM:, :], preferred_element_type=f32)
    xn = _ln(ALPHA * x_ref[...] + gate * out) * g1_ref[...] + b1_ref[...]
    xo_ref[...] = xn
    h2 = _ln(xn) * (1.0 + scale2) + shift2

    bits = lax.bitcast_convert_type(h2, u32)
    r = bits + jnp.uint32(0x7FFF) + ((bits >> 16) & jnp.uint32(1))
    hp_ref[...] = (r[:, :HALF] >> 16) | (r[:, HALF:] & jnp.uint32(0xFFFF0000))

    logits = jnp.dot(h2, wr_ref[...], precision=HIGHEST, preferred_element_type=f32)
    lt = logits.T
    s = _sigmoid(lt[0:N_EXPERTS, :])
    sb = s + br_ref[...]
    sb_rows = [sb[e:e + 1, :] for e in range(N_EXPERTS)]
    s_rows = [s[e:e + 1, :] for e in range(N_EXPERTS)]
    gscores = []
    for g in range(N_GROUPS):
        a_, b_, c_, d_ = sb_rows[EPG * g:EPG * g + EPG]
        hi1, lo1 = jnp.maximum(a_, b_), jnp.minimum(a_, b_)
        hi2, lo2 = jnp.maximum(c_, d_), jnp.minimum(c_, d_)
        top = jnp.maximum(hi1, hi2)
        second = jnp.maximum(jnp.minimum(hi1, hi2), jnp.maximum(lo1, lo2))
        gscores.append(top + second)
    gbest = gscores[0]
    gsel = jnp.zeros(gbest.shape, i32)
    for g in range(1, N_GROUPS):
        take = gscores[g] > gbest
        gbest = jnp.where(take, gscores[g], gbest)
        gsel = jnp.where(take, g, gsel)

    def pick_group(rows, j):
        v = rows[j]
        for g in range(1, N_GROUPS):
            v = jnp.where(gsel == g, rows[EPG * g + j], v)
        return v

    cand_b = [pick_group(sb_rows, j) for j in range(EPG)]
    cand_s = [pick_group(s_rows, j) for j in range(EPG)]
    i1, i2 = _top2_rows(cand_b)

    def pick_idx(rows, idx):
        v = rows[0]
        for j in range(1, EPG):
            v = jnp.where(idx == j, rows[j], v)
        return v

    s1 = pick_idx(cand_s, i1)
    s2 = pick_idx(cand_s, i2)
    tot = s1 + s2
    w1 = s1 / tot
    w2 = s2 / tot
    e1 = gsel * EPG + i1
    e2 = gsel * EPG + i2
    r8 = lax.broadcasted_iota(i32, (8, tm), 0)
    idx_ref[...] = jnp.where(r8 == 0, e1, jnp.where(r8 == 1, e2, 0))
    r128 = lax.broadcasted_iota(i32, (128, tm), 0)
    wmat = jnp.where(r128 == 0, w1, jnp.where(r128 == 1, w2, 0.0))
    wt_ref[...] = wmat.T


def _mix(x, mod_l, m_lat, m_ctx, proj, conv_w, conv_b, w_out, ln_g, ln_b, w_router, b_router, has_ctx):
    tm = TM_MIX
    m_rows = N_ALL if has_ctx else N_LAT
    nlb = N_LAT // tm
    nblocks = m_rows // tm
    hb = tm // GRID_W
    n_hblocks = N_ALL // GRID_W
    if m_ctx is None:
        m_ctx = m_lat
    ncb = m_ctx.shape[0] // tm
    in_specs = [
        pl.BlockSpec((tm, D_MODEL), lambda i: (i, 0)),
        pl.BlockSpec((None, 6, D_MODEL), lambda i: (_mod_row(i, tm), 0, 0)),
        pl.BlockSpec((tm, D_MLSTM), lambda i: (jnp.minimum(i, nlb - 1), 0)),
        pl.BlockSpec((tm, D_MLSTM), lambda i: (jnp.clip(i - nlb, 0, ncb - 1), 0)),
        pl.BlockSpec((tm, D_CONV), lambda i: (i, 3)),
        pl.BlockSpec((tm, D_CONV), lambda i: (i, 4)),
        pl.BlockSpec((tm, D_CONV), lambda i: (i, 5)),
        pl.BlockSpec((GRID_W, D_CONV_H), lambda i: (jnp.maximum(i * hb - 1, 0), 7)),
        pl.BlockSpec((GRID_W, D_CONV_H), lambda i: (jnp.maximum(i * hb - 1, 0), 11)),
        pl.BlockSpec((GRID_W, D_CONV_H), lambda i: (jnp.minimum((i + 1) * hb, n_hblocks - 1), 7)),
        pl.BlockSpec((GRID_W, D_CONV_H), lambda i: (jnp.minimum((i + 1) * hb, n_hblocks - 1), 11)),
        pl.BlockSpec((3, D_CONV), lambda i: (0, 0)),
        pl.BlockSpec((1, D_CONV), lambda i: (0, 0)),
        pl.BlockSpec((D_MODEL, D_MODEL), lambda i: (0, 0)),
        pl.BlockSpec((1, D_MODEL), lambda i: (0, 0)),
        pl.BlockSpec((1, D_MODEL), lambda i: (0, 0)),
        pl.BlockSpec((D_MODEL, 128), lambda i: (0, 0)),
        pl.BlockSpec((N_EXPERTS, 1), lambda i: (0, 0)),
    ]
    out_specs = [
        pl.BlockSpec((tm, D_MODEL), lambda i: (i, 0)),
        pl.BlockSpec((tm, HALF), lambda i: (i, 0)),
        pl.BlockSpec((8, tm), lambda i: (0, i)),
        pl.BlockSpec((tm, 128), lambda i: (i, 0)),
    ]
    out_shape = [
        jax.ShapeDtypeStruct((m_rows, D_MODEL), f32),
        jax.ShapeDtypeStruct((m_rows, HALF), u32),
        jax.ShapeDtypeStruct((8, m_rows), i32),
        jax.ShapeDtypeStruct((m_rows, 128), f32),
    ]
    return pl.pallas_call(
        functools.partial(_mix_kernel, n_lat_blocks=nlb, has_ctx=has_ctx),
        grid=(nblocks,),
        in_specs=in_specs,
        out_specs=out_specs,
        out_shape=out_shape,
        scratch_shapes=[pltpu.VMEM((tm, D_MLSTM), bf16), pltpu.VMEM((tm, D_CONV), bf16)],
        compiler_params=_params(("parallel",)),
        name="mix",
    )(x, mod_l, m_lat, m_ctx, proj, proj, proj, proj, proj, proj, proj,
      conv_w, conv_b, w_out, ln_g, ln_b, w_router, b_router)


def _route(idx8, m_rows):
    tm = TM_EXP
    n_tiles = 2 * m_rows // tm + N_EXPERTS
    ef = idx8[:2, :].reshape(-1)
    oh = (ef[:, None] == jnp.arange(N_EXPERTS, dtype=i32)[None, :]).astype(i32)
    cs = jnp.cumsum(oh, axis=0)
    rank = jnp.take_along_axis(cs, ef[:, None], axis=1)[:, 0] - 1
    counts = cs[-1]
    ntile = (counts + tm - 1) // tm
    tend = jnp.cumsum(ntile)
    tstart = tend - ntile
    pos = (tstart[ef] * tm + rank).astype(i32)
    n_used = tend[-1].astype(i32)
    tidx = jnp.arange(n_tiles, dtype=i32)
    te = jnp.searchsorted(tend, tidx, side="right").astype(i32)
    te_last = jnp.searchsorted(tend, n_used - 1, side="right").astype(i32)
    te = jnp.minimum(jnp.where(tidx < n_used, te, te_last), N_EXPERTS - 1)
    tok = jnp.tile(jnp.arange(m_rows, dtype=i32), 2)
    src = jnp.zeros((n_tiles * tm,), i32).at[pos].set(tok)
    return pos, src, te, n_used.reshape(1)


def _gather_kernel(src_ref, hp_ref, xs_ref, sem, *, rows):
    base = pl.program_id(0) * rows

    def issue(j, c):
        t = src_ref[base + j]
        pltpu.make_async_copy(hp_ref.at[pl.ds(t, 1), :], xs_ref.at[pl.ds(base + j, 1), :], sem).start()
        return c

    lax.fori_loop(0, rows, issue, 0, unroll=8)

    def drain(j, c):
        pltpu.make_async_copy(hp_ref.at[pl.ds(0, 1), :], xs_ref.at[pl.ds(0, 1), :], sem).wait()
        return c

    lax.fori_loop(0, rows, drain, 0, unroll=8)


def _gather_rows(src, hp):
    rows = 1024
    p = src.shape[0]
    return pl.pallas_call(
        functools.partial(_gather_kernel, rows=rows),
        grid_spec=pltpu.PrefetchScalarGridSpec(
            num_scalar_prefetch=1,
            grid=(p // rows,),
            in_specs=[pl.BlockSpec(memory_space=pl.ANY)],
            out_specs=pl.BlockSpec(memory_space=pl.ANY),
            scratch_shapes=[pltpu.SemaphoreType.DMA(())],
        ),
        out_shape=jax.ShapeDtypeStruct((p, hp.shape[1]), hp.dtype),
        compiler_params=_params(("arbitrary",)),
        name="moe_gather",
    )(src, hp)


def _expert_kernel(te_ref, nu_ref, xs_ref, w1_ref, w3_ref, w2_ref, y_ref):
    i = pl.program_id(0)

    @pl.when(i < nu_ref[0])
    def _():
        p = xs_ref[...]
        lo = lax.bitcast_convert_type(p << 16, f32).astype(bf16)
        hi = lax.bitcast_convert_type(p & jnp.uint32(0xFFFF0000), f32).astype(bf16)
        a1 = jnp.dot(lo, w1_ref[0:HALF, :], preferred_element_type=f32) \
            + jnp.dot(hi, w1_ref[HALF:, :], preferred_element_type=f32)
        a3 = jnp.dot(lo, w3_ref[0:HALF, :], preferred_element_type=f32) \
            + jnp.dot(hi, w3_ref[HALF:, :], preferred_element_type=f32)
        act = (a1 * _sigmoid(a1) * a3).astype(bf16)
        y_ref[...] = jnp.dot(act, w2_ref[...], preferred_element_type=f32)

    @pl.when(i >= nu_ref[0])
    def _():
        y_ref[...] = jnp.zeros_like(y_ref)


def _experts(te, n_used, xs, w1, w3, w2):
    tm = TM_EXP
    n_tiles = xs.shape[0] // tm
    return pl.pallas_call(
        _expert_kernel,
        grid_spec=pltpu.PrefetchScalarGridSpec(
            num_scalar_prefetch=2,
            grid=(n_tiles,),
            in_specs=[
                pl.BlockSpec((tm, HALF), lambda i, te, nu: (i, 0)),
                pl.BlockSpec((None, D_MODEL, D_FF), lambda i, te, nu: (te[i], 0, 0)),
                pl.BlockSpec((None, D_MODEL, D_FF), lambda i, te, nu: (te[i], 0, 0)),
                pl.BlockSpec((None, D_FF, D_MODEL), lambda i, te, nu: (te[i], 0, 0)),
            ],
            out_specs=pl.BlockSpec((tm, D_MODEL), lambda i, te, nu: (i, 0)),
        ),
        out_shape=jax.ShapeDtypeStruct((xs.shape[0], D_MODEL), f32),
        compiler_params=_params(("arbitrary",)),
        name="experts",
    )(te, n_used, xs, w1, w3, w2)


def _combine_kernel(pos_ref, x_ref, mod_ref, wt_ref, g_ref, b_ref, y_ref, o_ref, buf, sem, *,
                    m_rows, nblocks):
    tm = TM_CMB
    i = pl.program_id(0)

    def issue(blk, slot):
        base = blk * tm

        def it(j, c):
            p0 = pos_ref[base + j]
            p1 = pos_ref[m_rows + base + j]
            pltpu.make_async_copy(y_ref.at[pl.ds(p0, 1), :], buf.at[slot, 0, pl.ds(j, 1), :],
                                  sem.at[slot]).start()
            pltpu.make_async_copy(y_ref.at[pl.ds(p1, 1), :], buf.at[slot, 1, pl.ds(j, 1), :],
                                  sem.at[slot]).start()
            return c

        lax.fori_loop(0, tm, it, 0, unroll=8)

    @pl.when(i == 0)
    def _():
        issue(0, 0)

    @pl.when(i + 1 < nblocks)
    def _():
        issue(i + 1, (i + 1) % 2)

    slot = i % 2

    def drain(j, c):
        pltpu.make_async_copy(y_ref.at[pl.ds(0, 1), :], buf.at[slot, 0, pl.ds(0, 1), :],
                              sem.at[slot]).wait()
        return c

    lax.fori_loop(0, 2 * tm, drain, 0, unroll=8)

    wt = wt_ref[...]
    moe = wt[:, 0:1] * buf[slot, 0] + wt[:, 1:2] * buf[slot, 1]
    gate = mod_ref[5:6, :]
    o_ref[...] = _ln(ALPHA * x_ref[...] + gate * moe) * g_ref[...] + b_ref[...]


def _combine(pos, x, mod_l, wts, ln_g, ln_b, y):
    tm = TM_CMB
    m_rows = x.shape[0]
    nblocks = m_rows // tm
    return pl.pallas_call(
        functools.partial(_combine_kernel, m_rows=m_rows, nblocks=nblocks),
        grid_spec=pltpu.PrefetchScalarGridSpec(
            num_scalar_prefetch=1,
            grid=(nblocks,),
            in_specs=[
                pl.BlockSpec((tm, D_MODEL), lambda i, p: (i, 0)),
                pl.BlockSpec((None, 6, D_MODEL), lambda i, p: (_mod_row(i, tm), 0, 0)),
                pl.BlockSpec((tm, 128), lambda i, p: (i, 0)),
                pl.BlockSpec((1, D_MODEL), lambda i, p: (0, 0)),
                pl.BlockSpec((1, D_MODEL), lambda i, p: (0, 0)),
                pl.BlockSpec(memory_space=pl.ANY),
            ],
            out_specs=pl.BlockSpec((tm, D_MODEL), lambda i, p: (i, 0)),
            scratch_shapes=[pltpu.VMEM((2, 2, tm, D_MODEL), f32), pltpu.SemaphoreType.DMA((2,))],
        ),
        out_shape=jax.ShapeDtypeStruct((m_rows, D_MODEL), f32),
        compiler_params=_params(("arbitrary",)),
        name="combine",
    )(pos, x, mod_l, wts, ln_g, ln_b, y)


def kernel(x, c, ctx, c_ctx, w_ada, b_ada, w_in, b_igate, b_fgate, mh_norm_g, conv_w, conv_b, w_out,
           ln1_g, ln1_b, w_router, b_router, w1, w3, w2, ln2_g, ln2_b):
    n_state = 2 * HEADS * DQK + D_MLSTM
    cond_raw = jnp.zeros((8, D_MODEL), f32).at[:BATCH].set(c).at[BATCH].set(c_ctx)
    mod = _ada(cond_raw, w_ada, b_ada).reshape(DEPTH, 8, 6, D_MODEL)

    xa = jnp.concatenate([x.reshape(N_LAT, D_MODEL), ctx.reshape(N_CTX, D_MODEL)], axis=0)
    w_router_p = jnp.zeros((D_MODEL, 128), f32).at[:, :N_EXPERTS].set(w_router)
    b_router_c = b_router.reshape(N_EXPERTS, 1)

    for l in range(DEPTH):
        last = l == DEPTH - 1
        w_main = jnp.concatenate([w_in[l][:, :n_state], w_in[l][:, n_state + N_GATE:]], axis=1).astype(bf16)
        w_gate = jnp.zeros((D_MODEL, 128), bf16).at[:, :N_GATE].set(
            w_in[l][:, n_state:n_state + N_GATE].astype(bf16))
        proj, gates = _inproj(xa, mod[l], w_main, w_gate)

        bias = jnp.concatenate([b_igate[l], b_fgate[l]]).astype(f32)
        m_lat, m_ctx = _mlstm(proj, gates, bias, mh_norm_g[l].reshape(1, D_MLSTM), not last)

        xn, hp, idx8, wts = _mix(xa, mod[l], m_lat, m_ctx, proj, conv_w[l], conv_b[l].reshape(1, D_CONV),
                                 w_out[l].astype(bf16), ln1_g[l].reshape(1, D_MODEL),
                                 ln1_b[l].reshape(1, D_MODEL), w_router_p, b_router_c, not last)

        m_rows = xn.shape[0]
        pos, src, te, n_used = _route(idx8, m_rows)
        xs = _gather_rows(src, hp)
        y = _experts(te, n_used, xs, w1[l].astype(bf16), w3[l].astype(bf16), w2[l].astype(bf16))
        xa = _combine(pos, xn, mod[l], wts, ln2_g[l].reshape(1, D_MODEL), ln2_b[l].reshape(1, D_MODEL), y)

    return xa.reshape(BATCH, SEQ, D_MODEL)
```

```python
import functools

import jax
import jax.numpy as jnp
from jax import lax
from jax.experimental import pallas as pl
from jax.experimental.pallas import tpu as pltpu

f32 = jnp.float32
bf16 = jnp.bfloat16
i32 = jnp.int32
u32 = jnp.uint32

D_MODEL = 2048
BATCH = 4
SEQ = 2048
DEPTH = 4
GRID_W = 64
CTX_LEN = 256
D_MLSTM = 1024
HEADS = 4
DV = 256
DQK = 128
CHUNK = 64
D_CONV = 1024
D_CONV_H = 512
N_EXPERTS = 16
N_GROUPS = 4
EPG = 4
D_FF = 1024
ALPHA = (2 * DEPTH) ** 0.25
LN_EPS = 1e-6
QK_SCALE = DQK ** -0.5

N_LAT = BATCH * SEQ
N_CTX = BATCH * CTX_LEN
N_ALL = N_LAT + N_CTX
D_PROJ = 6144
N_GATE = 16
HALF = D_MODEL // 2

TM_IN = 1024
TN_IN = 512
TM_MIX = 256
TM_EXP = 256
TM_CMB = 256
VMEM_LIMIT = 56 * 1024 * 1024

HIGHEST = lax.Precision.HIGHEST


def _sigmoid(x):
    return 1.0 / (1.0 + jnp.exp(-x))


def _log_sigmoid(x):
    return jnp.minimum(x, 0.0) - jnp.log1p(jnp.exp(-jnp.abs(x)))


def _ln(x):
    mu = jnp.mean(x, axis=-1, keepdims=True)
    xc = x - mu
    var = jnp.mean(xc * xc, axis=-1, keepdims=True)
    return xc * lax.rsqrt(var + LN_EPS)


def _mod_row(i, tm):
    return jnp.minimum((i * tm) // SEQ, BATCH)


def _params(sem, vmem=VMEM_LIMIT):
    return pltpu.CompilerParams(dimension_semantics=sem, vmem_limit_bytes=vmem)


def _ada_kernel(c_ref, w_ref, b_ref, o_ref):
    c = c_ref[...]
    cond = c * _sigmoid(c)
    o_ref[...] = jnp.dot(cond, w_ref[...], preferred_element_type=f32) + b_ref[...]


def _ada(cond_raw, w_ada, b_ada):
    tn = 1024
    n = 6 * D_MODEL
    return pl.pallas_call(
        _ada_kernel,
        grid=(DEPTH, n // tn),
        in_specs=[
            pl.BlockSpec((8, D_MODEL), lambda l, j: (0, 0)),
            pl.BlockSpec((None, D_MODEL, tn), lambda l, j: (l, 0, j)),
            pl.BlockSpec((None, 1, tn), lambda l, j: (l, 0, j)),
        ],
        out_specs=pl.BlockSpec((None, 8, tn), lambda l, j: (l, 0, j)),
        out_shape=jax.ShapeDtypeStruct((DEPTH, 8, n), f32),
        compiler_params=_params(("parallel", "parallel")),
        name="ada",
    )(cond_raw, w_ada, b_ada.reshape(DEPTH, 1, n))


def _inproj_kernel(x_ref, mod_ref, w_ref, wg_ref, o_ref, g_ref, h_scr, *, slab):
    j = pl.program_id(1)

    @pl.when(j == 0)
    def _():
        shift = mod_ref[0:1, :]
        scale = mod_ref[1:2, :]

        def body(s, c):
            r0 = pl.multiple_of(s * slab, slab)
            h = _ln(x_ref[pl.ds(r0, slab), :]) * (1.0 + scale) + shift
            hb = h.astype(bf16)
            h_scr[pl.ds(r0, slab), :] = hb
            g_ref[pl.ds(r0, slab), :] = jnp.dot(hb, wg_ref[...], preferred_element_type=f32)
            return c

        lax.fori_loop(0, TM_IN // slab, body, 0)

    o_ref[...] = jnp.dot(h_scr[...], w_ref[...], preferred_element_type=f32).astype(o_ref.dtype)


def _inproj(x, mod_l, w_main, w_gate):
    m = x.shape[0]
    return pl.pallas_call(
        functools.partial(_inproj_kernel, slab=128),
        grid=(m // TM_IN, D_PROJ // TN_IN),
        in_specs=[
            pl.BlockSpec((TM_IN, D_MODEL), lambda i, j: (i, 0)),
            pl.BlockSpec((None, 6, D_MODEL), lambda i, j: (_mod_row(i, TM_IN), 0, 0)),
            pl.BlockSpec((D_MODEL, TN_IN), lambda i, j: (0, j)),
            pl.BlockSpec((D_MODEL, 128), lambda i, j: (0, 0)),
        ],
        out_specs=[
            pl.BlockSpec((TM_IN, TN_IN), lambda i, j: (i, j)),
            pl.BlockSpec((TM_IN, 128), lambda i, j: (i, 0)),
        ],
        out_shape=[
            jax.ShapeDtypeStruct((m, D_PROJ), bf16),
            jax.ShapeDtypeStruct((m, 128), f32),
        ],
        scratch_shapes=[pltpu.VMEM((TM_IN, D_MODEL), bf16)],
        compiler_params=_params(("parallel", "arbitrary")),
        name="inproj",
    )(x, mod_l, w_main, w_gate)


def _mlstm_chunk(q, k, v, fc_raw, ic_raw, fr_raw, ir_raw, b_i, b_f, ct_ref, n_ref, d, m, fwd):
    L = CHUNK
    lf_c = _log_sigmoid(fc_raw + b_f)
    i_c = ic_raw + b_i
    lf_r = _log_sigmoid(fr_raw + b_f)
    i_r = ir_raw + b_i
    rr = lax.broadcasted_iota(i32, (L, L), 0)
    cc = lax.broadcasted_iota(i32, (L, L), 1)
    lo = rr >= cc
    up = rr <= cc
    mask = lo if fwd else up
    a_mat = mask.astype(f32)
    a_t = (up if fwd else lo).astype(f32)
    bcol = jnp.dot(a_mat, jnp.broadcast_to(lf_c, (L, L)), precision=HIGHEST,
                   preferred_element_type=f32)
    brow = jnp.dot(jnp.broadcast_to(lf_r, (L, L)), a_t, precision=HIGHEST,
                   preferred_element_type=f32)
    dm = jnp.where(mask, bcol - brow + i_r, -jnp.inf)
    mloc = jnp.max(dm, axis=1, keepdims=True)
    wloc = jnp.exp(dm - mloc)
    b1 = bcol[:, 0:1]
    m_t = jnp.maximum(b1 + m, mloc)
    inter = jnp.exp(b1 + m - m_t)
    a = jnp.exp(mloc - m_t)
    qk = lax.dot_general(q, k, (((1,), (1,)), ((), ())), preferred_element_type=f32) * QK_SCALE
    sloc = qk * wloc
    ct = ct_ref[d]
    n = n_ref[d]
    num = inter * jnp.dot(q, ct.astype(bf16), preferred_element_type=f32) \
        + a * jnp.dot(sloc.astype(bf16), v, preferred_element_type=f32)
    den = inter * jnp.sum(q.astype(f32) * n, axis=1, keepdims=True) \
        + a * jnp.sum(sloc, axis=1, keepdims=True)
    hout = num / jnp.maximum(jnp.abs(den), jnp.exp(-m_t))
    last = L - 1 if fwd else 0
    b_last = b1[last:last + 1, :]
    wl = jnp.exp(b_last - b1 + i_c - mloc[last:last + 1, :])
    kw = k.astype(f32) * (wl * QK_SCALE)
    u = lax.dot_general(kw.astype(bf16), v, (((0,), (0,)), ((), ())), preferred_element_type=f32)
    decay = inter[last:last + 1, :]
    a_l = a[last:last + 1, :]
    ct_ref[d] = decay * ct + a_l * u
    n_ref[d] = decay * n + a_l * jnp.sum(kw, axis=0, keepdims=True)
    return hout, m_t[last:last + 1, :]


def _mlstm_kernel(bias_ref, ql_ref, kl_ref, vl_ref, ol_ref, qc_ref, kc_ref, vc_ref, oc_ref,
                  gcl_ref, grl_ref, gcc_ref, grc_ref, gain_ref, *rest, has_ctx_out):
    if has_ctx_out:
        ml_ref, mc_ref, ct_ref, n_ref, hfl, hbl, hfc, hbc = rest
    else:
        ml_ref, ct_ref, n_ref, hfl, hbl = rest
        mc_ref = hfc = hbc = None
    b = pl.program_id(0)
    h = pl.program_id(1)
    L = CHUNK
    bi = (bias_ref[h], bias_ref[HEADS + h])
    bf = (bias_ref[2 * HEADS + h], bias_ref[3 * HEADS + h])
    ct_ref[...] = jnp.zeros_like(ct_ref)
    n_ref[...] = jnp.zeros_like(n_ref)

    def run(q_ref, k_ref, v_ref, gc_ref, gr_ref, c_base, nchunks, hf, hb, carry):
        def body(j, carry):
            m_f, m_b = carry
            outs = []
            for d, fwd in ((0, True), (1, False)):
                c = j if fwd else nchunks - 1 - j
                r0 = pl.multiple_of(c * L, L)
                q = q_ref[pl.ds(r0, L), :]
                k = k_ref[pl.ds(r0, L), :]
                v = v_ref[pl.ds(r0, L), :]
                gcol = gc_ref[c_base + c]
                ic = gcol[:, d:d + 1]
                fc = gcol[:, 2 + d:3 + d]
                ir = gr_ref[d, pl.ds(c_base + c, 1), :]
                fr = gr_ref[2 + d, pl.ds(c_base + c, 1), :]
                hout, m_new = _mlstm_chunk(q, k, v, fc, ic, fr, ir, bi[d], bf[d], ct_ref, n_ref, d,
                                           m_f if fwd else m_b, fwd)
                if hf is not None:
                    (hf if fwd else hb)[pl.ds(r0, L), :] = hout
                outs.append(m_new)
            return tuple(outs)

        return lax.fori_loop(0, nchunks, body, carry)

    zero = jnp.zeros((1, 1), f32)
    carry = run(qc_ref, kc_ref, vc_ref, gcc_ref, grc_ref, b * (CTX_LEN // L), CTX_LEN // L, hfc, hbc,
                (zero, zero))
    run(ql_ref, kl_ref, vl_ref, gcl_ref, grl_ref, 0, SEQ // L, hfl, hbl, carry)

    gain = gain_ref[...]

    def finish(hf, hb, o_ref, out_ref, rows):
        slab = 256

        def ep(s, c):
            r0 = pl.multiple_of(s * slab, slab)
            hs = hf[pl.ds(r0, slab), :] + hb[pl.ds(r0, slab), :]
            r = lax.rsqrt(jnp.mean(hs * hs, axis=-1, keepdims=True) + LN_EPS)
            o = o_ref[pl.ds(r0, slab), :].astype(f32)
            out_ref[pl.ds(r0, slab), :] = (hs * r * gain * _sigmoid(o)).astype(out_ref.dtype)
            return c

        lax.fori_loop(0, rows // slab, ep, 0)

    finish(hfl, hbl, ol_ref, ml_ref, SEQ)
    if has_ctx_out:
        finish(hfc, hbc, oc_ref, mc_ref, CTX_LEN)


def _mlstm(proj, gates, bias, gain, has_ctx_out):
    L = CHUNK
    g = gates[:, :N_GATE]
    g4 = jnp.stack([g[:, 0:4], g[:, 4:8], g[:, 8:12], g[:, 12:16]], axis=-1)
    gh = jnp.transpose(g4, (1, 0, 2))
    gcol_l = gh[:, :N_LAT].reshape(HEADS, N_LAT // L, L, 4)
    gcol_c = gh[:, N_LAT:].reshape(HEADS, N_CTX // L, L, 4)
    gr = jnp.transpose(g4, (1, 2, 0))
    grow_l = gr[:, :, :N_LAT].reshape(HEADS, 4, N_LAT // L, L)
    grow_c = gr[:, :, N_LAT:].reshape(HEADS, 4, N_CTX // L, L)

    lat_rb = lambda b, h: b
    ctx_rb = lambda b, h: N_LAT // CTX_LEN + b
    in_specs = [
        pl.BlockSpec(memory_space=pltpu.SMEM),
        pl.BlockSpec((SEQ, DQK), lambda b, h: (lat_rb(b, h), h)),
        pl.BlockSpec((SEQ, DQK), lambda b, h: (lat_rb(b, h), HEADS + h)),
        pl.BlockSpec((SEQ, DV), lambda b, h: (lat_rb(b, h), HEADS + h)),
        pl.BlockSpec((SEQ, DV), lambda b, h: (lat_rb(b, h), 2 * HEADS + h)),
        pl.BlockSpec((CTX_LEN, DQK), lambda b, h: (ctx_rb(b, h), h)),
        pl.BlockSpec((CTX_LEN, DQK), lambda b, h: (ctx_rb(b, h), HEADS + h)),
        pl.BlockSpec((CTX_LEN, DV), lambda b, h: (ctx_rb(b, h), HEADS + h)),
        pl.BlockSpec((CTX_LEN, DV), lambda b, h: (ctx_rb(b, h), 2 * HEADS + h)),
        pl.BlockSpec((None, SEQ // L, L, 4), lambda b, h: (h, b, 0, 0)),
        pl.BlockSpec((None, 4, SEQ // L, L), lambda b, h: (h, 0, b, 0)),
        pl.BlockSpec((None, N_CTX // L, L, 4), lambda b, h: (h, 0, 0, 0)),
        pl.BlockSpec((None, 4, N_CTX // L, L), lambda b, h: (h, 0, 0, 0)),
        pl.BlockSpec((1, DV), lambda b, h: (0, h)),
    ]
    out_specs = [pl.BlockSpec((SEQ, DV), lambda b, h: (b, h))]
    out_shape = [jax.ShapeDtypeStruct((N_LAT, D_MLSTM), bf16)]
    scratch = [pltpu.VMEM((2, DQK, DV), f32), pltpu.VMEM((2, 1, DQK), f32),
               pltpu.VMEM((SEQ, DV), f32), pltpu.VMEM((SEQ, DV), f32)]
    if has_ctx_out:
        out_specs.append(pl.BlockSpec((CTX_LEN, DV), lambda b, h: (b, h)))
        out_shape.append(jax.ShapeDtypeStruct((N_CTX, D_MLSTM), bf16))
        scratch += [pltpu.VMEM((CTX_LEN, DV), f32), pltpu.VMEM((CTX_LEN, DV), f32)]
    res = pl.pallas_call(
        functools.partial(_mlstm_kernel, has_ctx_out=has_ctx_out),
        grid=(BATCH, HEADS),
        in_specs=in_specs,
        out_specs=out_specs,
        out_shape=out_shape,
        scratch_shapes=scratch,
        compiler_params=_params(("parallel", "parallel")),
        name="mlstm",
    )(bias, proj, proj, proj, proj, proj, proj, proj, proj, gcol_l, grow_l, gcol_c, grow_c, gain)
    return res if has_ctx_out else (res[0], None)


GROUP = 4
N_CHUNK_CTX = CTX_LEN // CHUNK
N_CHUNK_LAT = SEQ // CHUNK
N_CHUNK = N_CHUNK_CTX + N_CHUNK_LAT
N_GROUP_LAT = N_CHUNK_LAT // GROUP
LAT_ROW0 = 8
D_AUG = DV + 128


def _mlstm3_kernel(bias_ref, ql_ref, kl_ref, vl_ref, ol_ref, qc_ref, kc_ref, vc_ref, oc_ref,
                   gcl_ref, grl_ref, gcc_ref, grc_ref, gain_ref, *rest, has_ctx_out):
    if has_ctx_out:
        ml_ref, mc_ref, s_ref, st_ref, msc, rows_s, cols_s = rest
    else:
        ml_ref, s_ref, st_ref, msc, rows_s, cols_s = rest
        mc_ref = None
    L = CHUNK
    b = pl.program_id(0)
    h = pl.program_id(1)
    bi = (bias_ref[h], bias_ref[HEADS + h])
    bf = (bias_ref[2 * HEADS + h], bias_ref[3 * HEADS + h])

    rr = lax.broadcasted_iota(i32, (L, L), 0)
    cc = lax.broadcasted_iota(i32, (L, L), 1)
    lo_mask = rr >= cc
    up_mask = rr <= cc
    masks = (lo_mask, up_mask)

    def row_forms(gr_ref, sl, dst0, n):
        for d in range(2):
            tri = (up_mask if d == 0 else lo_mask).astype(f32)
            i_r = gr_ref[d, sl, :] + bi[d]
            lf_r = _log_sigmoid(gr_ref[2 + d, sl, :] + bf[d])
            b_r = jnp.dot(lf_r, tri, precision=HIGHEST, preferred_element_type=f32)
            rows_s[d, dst0:dst0 + n, :] = i_r - b_r

    row_forms(grc_ref, pl.ds(b * N_CHUNK_CTX, N_CHUNK_CTX), 0, N_CHUNK_CTX)
    row_forms(grl_ref, slice(None), LAT_ROW0, N_CHUNK_LAT)

    kind = lax.broadcasted_iota(i32, (1, 4 * GROUP), 1) % 4
    bias_v = jnp.where(kind == 0, bi[0], jnp.where(kind == 1, bi[1], jnp.where(kind == 2, bf[0], bf[1])))
    tpos = lax.broadcasted_iota(i32, (L, 4 * GROUP), 0)

    def col_forms(x):
        y = x + bias_v
        y = jnp.where(kind >= 2, _log_sigmoid(y), y)
        pre = y
        suf = y
        s = 1
        while s < L:
            pre = pre + jnp.where(tpos >= s, pltpu.roll(pre, s, 0), 0.0)
            suf = suf + jnp.where(tpos < L - s, pltpu.roll(suf, L - s, 0), 0.0)
            s *= 2
        return jnp.where(kind == 2, pre, jnp.where(kind == 3, suf, y))

    cols_s[0] = col_forms(gcc_ref[...])

    def col_body(g, c):
        cols_s[g + 1] = col_forms(gcl_ref[g])
        return c

    lax.fori_loop(0, N_GROUP_LAT, col_body, 0)

    ones_col = (lax.broadcasted_iota(i32, (L, 128), 1) == 0).astype(bf16)

    s_ref[...] = jnp.zeros_like(s_ref)

    def state_step(d, c, row, col, j, k, v, m):
        i_c = col[:, 4 * j + d:4 * j + d + 1]
        b1 = col[:, 4 * j + 2 + d:4 * j + 3 + d]
        last = L - 1 if d == 0 else 0
        b_last = b1[last:last + 1, :]
        ct = i_c - b1
        mx = jnp.max(ct, axis=0, keepdims=True)
        wl = jnp.exp(ct - mx)
        mloc_last = b_last + mx
        m_new = jnp.maximum(b_last + m, mloc_last)
        decay = jnp.exp(b_last + m - m_new)
        a_l = jnp.exp(mloc_last - m_new)
        s_old = s_ref[d]
        st_ref[d, c] = s_old.astype(bf16)
        msc[d, pl.ds(row, 1), :] = jnp.broadcast_to(m, (1, 128))
        kw = (k.astype(f32) * (wl * QK_SCALE)).astype(bf16)
        vaug = jnp.concatenate([v, ones_col], axis=1)
        u = lax.dot_general(kw, vaug, (((0,), (0,)), ((), ())), preferred_element_type=f32)
        s_ref[d] = decay * s_old + a_l * u
        return m_new

    zero = jnp.zeros((1, 1), f32)
    m_f = m_b = zero
    col0 = cols_s[0]
    for step in range(GROUP):
        jf, jb = step, GROUP - 1 - step
        m_f = state_step(0, jf, jf, col0, jf, kc_ref[jf * L:(jf + 1) * L, :], vc_ref[jf * L:(jf + 1) * L, :], m_f)
        m_b = state_step(1, jb, jb, col0, jb, kc_ref[jb * L:(jb + 1) * L, :], vc_ref[jb * L:(jb + 1) * L, :], m_b)

    def state_body(it, carry):
        m_f, m_b = carry
        gf = it
        gb = N_GROUP_LAT + 1 - it
        colf = cols_s[gf]
        colb = cols_s[gb]
        for step in range(GROUP):
            jf, jb = step, GROUP - 1 - step
            clf = (gf - 1) * GROUP + jf
            clb = (gb - 1) * GROUP + jb
            rf = pl.multiple_of(clf * L, L)
            rb = pl.multiple_of(clb * L, L)
            m_f = state_step(0, N_CHUNK_CTX + clf, LAT_ROW0 + clf, colf, jf,
                             kl_ref[pl.ds(rf, L), :], vl_ref[pl.ds(rf, L), :], m_f)
            m_b = state_step(1, N_CHUNK_CTX + clb, LAT_ROW0 + clb, colb, jb,
                             kl_ref[pl.ds(rb, L), :], vl_ref[pl.ds(rb, L), :], m_b)
        return m_f, m_b

    lax.fori_loop(1, N_GROUP_LAT + 1, state_body, (m_f, m_b))

    gain = gain_ref[...]

    def out_chunk(q, k, v, o, col, j, c, row, out_ref, r0):
        qk = lax.dot_general(q, k, (((1,), (1,)), ((), ())), preferred_element_type=f32) * QK_SCALE
        vaug = jnp.concatenate([v, ones_col], axis=1)
        sl = []
        per = []
        for d in range(2):
            b1 = col[:, 4 * j + 2 + d:4 * j + 3 + d]
            dm = jnp.where(masks[d], b1 + rows_s[d, pl.ds(row, 1), :], -jnp.inf)
            mloc = jnp.max(dm, axis=1, keepdims=True)
            wloc = jnp.exp(dm - mloc)
            m_prev = msc[d, pl.ds(row, 1), :][:, 0:1]
            m_t = jnp.maximum(b1 + m_prev, mloc)
            inter = jnp.exp(b1 + m_prev - m_t)
            a = jnp.exp(mloc - m_t)
            sl.append((qk * wloc).astype(bf16))
            per.append((m_t, inter, a))
        x = jnp.dot(jnp.concatenate(sl, axis=0), vaug, preferred_element_type=f32)
        hs = None
        for d in range(2):
            y = jnp.dot(q, st_ref[d, c], preferred_element_type=f32)
            m_t, inter, a = per[d]
            xd = x[d * L:(d + 1) * L, :]
            num = inter * y[:, :DV] + a * xd[:, :DV]
            den = inter * y[:, DV:DV + 1] + a * xd[:, DV:DV + 1]
            hd = num * (1.0 / jnp.maximum(jnp.abs(den), jnp.exp(-m_t)))
            hs = hd if hs is None else hs + hd
        r = lax.rsqrt(jnp.mean(hs * hs, axis=-1, keepdims=True) + LN_EPS)
        out_ref[pl.ds(r0, L), :] = (hs * r * gain * _sigmoid(o.astype(f32))).astype(out_ref.dtype)

    if has_ctx_out:
        for j in range(GROUP):
            sl_ = slice(j * L, (j + 1) * L)
            out_chunk(qc_ref[sl_, :], kc_ref[sl_, :], vc_ref[sl_, :], oc_ref[sl_, :], col0, j, j, j,
                      mc_ref, j * L)

    def out_body(g, carry):
        col = cols_s[g + 1]
        for j in range(GROUP):
            cl = g * GROUP + j
            r0 = pl.multiple_of(cl * L, L)
            out_chunk(ql_ref[pl.ds(r0, L), :], kl_ref[pl.ds(r0, L), :], vl_ref[pl.ds(r0, L), :],
                      ol_ref[pl.ds(r0, L), :], col, j, N_CHUNK_CTX + cl, LAT_ROW0 + cl, ml_ref, r0)
        return carry

    lax.fori_loop(0, N_GROUP_LAT, out_body, 0)


def _mlstm3(proj, gates, bias, gain, has_ctx_out):
    L = CHUNK
    g = gates[:, :N_GATE]
    g4 = jnp.stack([g[:, 0:4], g[:, 4:8], g[:, 8:12], g[:, 12:16]], axis=-1)
    gh = jnp.transpose(g4, (1, 0, 2))
    gcol_l = gh[:, :N_LAT].reshape(HEADS, BATCH * N_GROUP_LAT, GROUP, L, 4)
    gcol_l = jnp.transpose(gcol_l, (0, 1, 3, 2, 4)).reshape(HEADS, BATCH * N_GROUP_LAT, L, 4 * GROUP)
    gcol_c = gh[:, N_LAT:].reshape(HEADS, BATCH, GROUP, L, 4)
    gcol_c = jnp.transpose(gcol_c, (0, 1, 3, 2, 4)).reshape(HEADS, BATCH, L, 4 * GROUP)
    gr = jnp.transpose(g4, (1, 2, 0))
    grow_l = gr[:, :, :N_LAT].reshape(HEADS, 4, N_LAT // L, L)
    grow_c = gr[:, :, N_LAT:].reshape(HEADS, 4, N_CTX // L, L)

    ctx_rb = lambda b, h: N_LAT // CTX_LEN + b
    in_specs = [
        pl.BlockSpec(memory_space=pltpu.SMEM),
        pl.BlockSpec((SEQ, DQK), lambda b, h: (b, h)),
        pl.BlockSpec((SEQ, DQK), lambda b, h: (b, HEADS + h)),
        pl.BlockSpec((SEQ, DV), lambda b, h: (b, HEADS + h)),
        pl.BlockSpec((SEQ, DV), lambda b, h: (b, 2 * HEADS + h)),
        pl.BlockSpec((CTX_LEN, DQK), lambda b, h: (ctx_rb(b, h), h)),
        pl.BlockSpec((CTX_LEN, DQK), lambda b, h: (ctx_rb(b, h), HEADS + h)),
        pl.BlockSpec((CTX_LEN, DV), lambda b, h: (ctx_rb(b, h), HEADS + h)),
        pl.BlockSpec((CTX_LEN, DV), lambda b, h: (ctx_rb(b, h), 2 * HEADS + h)),
        pl.BlockSpec((None, N_GROUP_LAT, L, 4 * GROUP), lambda b, h: (h, b, 0, 0)),
        pl.BlockSpec((None, 4, N_CHUNK_LAT, L), lambda b, h: (h, 0, b, 0)),
        pl.BlockSpec((None, None, L, 4 * GROUP), lambda b, h: (h, b, 0, 0)),
        pl.BlockSpec((None, 4, N_CTX // L, L), lambda b, h: (h, 0, 0, 0)),
        pl.BlockSpec((1, DV), lambda b, h: (0, h)),
    ]
    out_specs = [pl.BlockSpec((SEQ, DV), lambda b, h: (b, h))]
    out_shape = [jax.ShapeDtypeStruct((N_LAT, D_MLSTM), bf16)]
    if has_ctx_out:
        out_specs.append(pl.BlockSpec((CTX_LEN, DV), lambda b, h: (b, h)))
        out_shape.append(jax.ShapeDtypeStruct((N_CTX, D_MLSTM), bf16))
    scratch = [
        pltpu.VMEM((2, DQK, D_AUG), f32),
        pltpu.VMEM((2, N_CHUNK, DQK, D_AUG), bf16),
        pltpu.VMEM((2, LAT_ROW0 + N_CHUNK_LAT, 128), f32),
        pltpu.VMEM((2, LAT_ROW0 + N_CHUNK_LAT, L), f32),
        pltpu.VMEM((N_GROUP_LAT + 1, L, 4 * GROUP), f32),
    ]
    res = pl.pallas_call(
        functools.partial(_mlstm3_kernel, has_ctx_out=has_ctx_out),
        grid=(BATCH, HEADS),
        in_specs=in_specs,
        out_specs=out_specs,
        out_shape=out_shape,
        scratch_shapes=scratch,
        compiler_params=_params(("parallel", "parallel")),
        name="mlstm",
    )(bias, proj, proj, proj, proj, proj, proj, proj, proj, gcol_l, grow_l, gcol_c, grow_c, gain)
    return res if has_ctx_out else (res[0], None)


def _top2_rows(vals):
    best = vals[0]
    bi = jnp.zeros(best.shape, i32)
    for j in range(1, len(vals)):
        take = vals[j] > best
        best = jnp.where(take, vals[j], best)
        bi = jnp.where(take, j, bi)
    sec = None
    si = None
    for j in range(len(vals)):
        cand = jnp.where(bi == j, -jnp.inf, vals[j])
        if sec is None:
            sec, si = cand, jnp.zeros(best.shape, i32)
        else:
            take = cand > sec
            sec = jnp.where(take, cand, sec)
            si = jnp.where(take, j, si)
    return bi, si


def _mix_kernel(x_ref, mod_ref, ml_ref, mc_ref, u_ref, bg_ref, cg_ref, ut_ref, ct_ref, ub_ref, cb_ref,
                cw_ref, cbias_ref, wo_ref, g1_ref, b1_ref, wr_ref, br_ref,
                xo_ref, hp_ref, idx_ref, wt_ref, m_scr, y_scr, *, n_lat_blocks, has_ctx):
    tm = TM_MIX
    i = pl.program_id(0)
    gate = mod_ref[2:3, :]
    shift2 = mod_ref[3:4, :]
    scale2 = mod_ref[4:5, :]
    cw = cw_ref[...]
    cbias = cbias_ref[...]
    row = lax.broadcasted_iota(i32, (tm, 1), 0)

    def shifted(z, first, last):
        prev = jnp.where(first, 0.0, pltpu.roll(z, 1, 0))
        nxt = jnp.where(last, 0.0, pltpu.roll(z, tm - 1, 0))
        return prev, nxt

    def lat_branch():
        z = cg_ref[...].astype(f32) * u_ref[...].astype(f32)
        col = row % GRID_W
        zh = z[:, :D_CONV_H]
        prev, nxt = shifted(zh, col == 0, col == GRID_W - 1)
        yh = cw[0:1, :D_CONV_H] * prev + cw[1:2, :D_CONV_H] * zh + cw[2:3, :D_CONV_H] * nxt
        bpb = SEQ // tm
        top_ok = (i % bpb != 0).astype(f32)
        bot_ok = (i % bpb != bpb - 1).astype(f32)
        zt = ct_ref[...].astype(f32) * ut_ref[...].astype(f32) * top_ok
        zb = cb_ref[...].astype(f32) * ub_ref[...].astype(f32) * bot_ok
        zv = z[:, D_CONV_H:]
        zext = jnp.concatenate([zt, zv, zb], axis=0)
        yv = cw[0:1, D_CONV_H:] * zext[0:tm] + cw[1:2, D_CONV_H:] * zv \
            + cw[2:3, D_CONV_H:] * zext[2 * GRID_W:2 * GRID_W + tm]
        y = jnp.concatenate([yh, yv], axis=1) + cbias
        y_scr[...] = (bg_ref[...].astype(f32) * y).astype(bf16)
        m_scr[...] = ml_ref[...]

    def ctx_branch():
        z = cg_ref[...].astype(f32) * u_ref[...].astype(f32)
        pos = row % CTX_LEN
        prev, nxt = shifted(z, pos == 0, pos == CTX_LEN - 1)
        y = cw[0:1, :] * prev + cw[1:2, :] * z + cw[2:3, :] * nxt + cbias
        y_scr[...] = (bg_ref[...].astype(f32) * y).astype(bf16)
        m_scr[...] = mc_ref[...]

    if has_ctx:
        pl.when(i < n_lat_blocks)(lat_branch)
        pl.when(i >= n_lat_blocks)(ctx_branch)
    else:
        lat_branch()

    out = jnp.dot(m_scr[...], wo_ref[0:D_MLSTM, :], preferred_element_type=f32) \
        + jnp.dot(y_scr[...], wo_ref[D_MLSTM:, :], preferred_element_type=f32)
    xn = _ln(ALPHA * x_ref[...] + gate * out) * g1_ref[...] + b1_ref[...]
    xo_ref[...] = xn
    h2 = _ln(xn) * (1.0 + scale2) + shift2

    bits = lax.bitcast_convert_type(h2, u32)
    r = bits + jnp.uint32(0x7FFF) + ((bits >> 16) & jnp.uint32(1))
    hp_ref[...] = (r[:, :HALF] >> 16) | (r[:, HALF:] & jnp.uint32(0xFFFF0000))

    h_hi = lax.bitcast_convert_type(r & jnp.uint32(0xFFFF0000), f32)
    hs = jnp.concatenate([h_hi.astype(bf16), (h2 - h_hi).astype(bf16)], axis=0)
    pr = jnp.dot(hs, wr_ref[...], preferred_element_type=f32)
    logits = pr[:tm, :128] + (pr[:tm, 128:] + pr[tm:, :128])
    lt = logits.T
    s = _sigmoid(lt[0:N_EXPERTS, :])
    sb = s + br_ref[...]
    sb_rows = [sb[e:e + 1, :] for e in range(N_EXPERTS)]
    s_rows = [s[e:e + 1, :] for e in range(N_EXPERTS)]
    gscores = []
    for g in range(N_GROUPS):
        a_, b_, c_, d_ = sb_rows[EPG * g:EPG * g + EPG]
        hi1, lo1 = jnp.maximum(a_, b_), jnp.minimum(a_, b_)
        hi2, lo2 = jnp.maximum(c_, d_), jnp.minimum(c_, d_)
        top = jnp.maximum(hi1, hi2)
        second = jnp.maximum(jnp.minimum(hi1, hi2), jnp.maximum(lo1, lo2))
        gscores.append(top + second)
    gbest = gscores[0]
    gsel = jnp.zeros(gbest.shape, i32)
    for g in range(1, N_GROUPS):
        take = gscores[g] > gbest
        gbest = jnp.where(take, gscores[g], gbest)
        gsel = jnp.where(take, g, gsel)

    def pick_group(rows, j):
        v = rows[j]
        for g in range(1, N_GROUPS):
            v = jnp.where(gsel == g, rows[EPG * g + j], v)
        return v

    cand_b = [pick_group(sb_rows, j) for j in range(EPG)]
    cand_s = [pick_group(s_rows, j) for j in range(EPG)]
    i1, i2 = _top2_rows(cand_b)

    def pick_idx(rows, idx):
        v = rows[0]
        for j in range(1, EPG):
            v = jnp.where(idx == j, rows[j], v)
        return v

    s1 = pick_idx(cand_s, i1)
    s2 = pick_idx(cand_s, i2)
    tot = s1 + s2
    w1 = s1 / tot
    w2 = s2 / tot
    e1 = gsel * EPG + i1
    e2 = gsel * EPG + i2
    r8 = lax.broadcasted_iota(i32, (8, tm), 0)
    idx_ref[...] = jnp.where(r8 == 0, e1, jnp.where(r8 == 1, e2, 0))
    r128 = lax.broadcasted_iota(i32, (128, tm), 0)
    wmat = jnp.where(r128 == 0, w1, jnp.where(r128 == 1, w2, 0.0))
    wt_ref[...] = wmat.T


def _mix(x, mod_l, m_lat, m_ctx, proj, conv_w, conv_b, w_out, ln_g, ln_b, w_router, b_router, has_ctx):
    tm = TM_MIX
    m_rows = N_ALL if has_ctx else N_LAT
    nlb = N_LAT // tm
    nblocks = m_rows // tm
    hb = tm // GRID_W
    n_hblocks = N_ALL // GRID_W
    if m_ctx is None:
        m_ctx = m_lat
    ncb = m_ctx.shape[0] // tm
    in_specs = [
        pl.BlockSpec((tm, D_MODEL), lambda i: (i, 0)),
        pl.BlockSpec((None, 6, D_MODEL), lambda i: (_mod_row(i, tm), 0, 0)),
        pl.BlockSpec((tm, D_MLSTM), lambda i: (jnp.minimum(i, nlb - 1), 0)),
        pl.BlockSpec((tm, D_MLSTM), lambda i: (jnp.clip(i - nlb, 0, ncb - 1), 0)),
        pl.BlockSpec((tm, D_CONV), lambda i: (i, 3)),
        pl.BlockSpec((tm, D_CONV), lambda i: (i, 4)),
        pl.BlockSpec((tm, D_CONV), lambda i: (i, 5)),
        pl.BlockSpec((GRID_W, D_CONV_H), lambda i: (jnp.maximum(i * hb - 1, 0), 7)),
        pl.BlockSpec((GRID_W, D_CONV_H), lambda i: (jnp.maximum(i * hb - 1, 0), 11)),
        pl.BlockSpec((GRID_W, D_CONV_H), lambda i: (jnp.minimum((i + 1) * hb, n_hblocks - 1), 7)),
        pl.BlockSpec((GRID_W, D_CONV_H), lambda i: (jnp.minimum((i + 1) * hb, n_hblocks - 1), 11)),
        pl.BlockSpec((3, D_CONV), lambda i: (0, 0)),
        pl.BlockSpec((1, D_CONV), lambda i: (0, 0)),
        pl.BlockSpec((D_MODEL, D_MODEL), lambda i: (0, 0)),
        pl.BlockSpec((1, D_MODEL), lambda i: (0, 0)),
        pl.BlockSpec((1, D_MODEL), lambda i: (0, 0)),
        pl.BlockSpec((D_MODEL, 256), lambda i: (0, 0)),
        pl.BlockSpec((N_EXPERTS, 1), lambda i: (0, 0)),
    ]
    out_specs = [
        pl.BlockSpec((tm, D_MODEL), lambda i: (i, 0)),
        pl.BlockSpec((tm, HALF), lambda i: (i, 0)),
        pl.BlockSpec((8, tm), lambda i: (0, i)),
        pl.BlockSpec((tm, 128), lambda i: (i, 0)),
    ]
    out_shape = [
        jax.ShapeDtypeStruct((m_rows, D_MODEL), f32),
        jax.ShapeDtypeStruct((m_rows, HALF), u32),
        jax.ShapeDtypeStruct((8, m_rows), i32),
        jax.ShapeDtypeStruct((m_rows, 128), f32),
    ]
    return pl.pallas_call(
        functools.partial(_mix_kernel, n_lat_blocks=nlb, has_ctx=has_ctx),
        grid=(nblocks,),
        in_specs=in_specs,
        out_specs=out_specs,
        out_shape=out_shape,
        scratch_shapes=[pltpu.VMEM((tm, D_MLSTM), bf16), pltpu.VMEM((tm, D_CONV), bf16)],
        compiler_params=_params(("parallel",)),
        name="mix",
    )(x, mod_l, m_lat, m_ctx, proj, proj, proj, proj, proj, proj, proj,
      conv_w, conv_b, w_out, ln_g, ln_b, w_router, b_router)


def _route(idx8, m_rows):
    tm = TM_EXP
    n_tiles = 2 * m_rows // tm + N_EXPERTS
    ef = idx8[:2, :].reshape(-1)
    oh = (ef[:, None] == jnp.arange(N_EXPERTS, dtype=i32)[None, :]).astype(i32)
    cs = jnp.cumsum(oh, axis=0)
    rank = jnp.take_along_axis(cs, ef[:, None], axis=1)[:, 0] - 1
    counts = cs[-1]
    ntile = (counts + tm - 1) // tm
    tend = jnp.cumsum(ntile)
    tstart = tend - ntile
    pos = (tstart[ef] * tm + rank).astype(i32)
    n_used = tend[-1].astype(i32)
    tidx = jnp.arange(n_tiles, dtype=i32)
    te = jnp.sum((tend[None, :] <= jnp.minimum(tidx, n_used - 1)[:, None]).astype(i32), axis=1)
    te = jnp.minimum(te, N_EXPERTS - 1)
    tok = jnp.tile(jnp.arange(m_rows, dtype=i32), 2)
    src = jnp.zeros((n_tiles * tm,), i32).at[pos].set(tok)
    return pos, src, te, n_used.reshape(1)


def _expert_kernel(te_ref, nu_ref, src_ref, hp_ref, w1_ref, w3_ref, w2_ref, y_ref, xbuf, sem):
    tm = TM_EXP
    i = pl.program_id(0)
    n_used = nu_ref[0]

    def issue(tile, slot):
        base = tile * tm

        def it(j, c):
            t = src_ref[base + j]
            pltpu.make_async_copy(hp_ref.at[pl.ds(t, 1), :], xbuf.at[slot, pl.ds(j, 1), :],
                                  sem.at[slot]).start()
            return c

        lax.fori_loop(0, tm, it, 0, unroll=8)

    @pl.when(i == 0)
    def _():
        issue(0, 0)

    @pl.when(i + 1 < n_used)
    def _():
        issue(i + 1, (i + 1) % 2)

    @pl.when(i < n_used)
    def _():
        slot = i % 2

        def drain(j, c):
            pltpu.make_async_copy(hp_ref.at[pl.ds(0, 1), :], xbuf.at[slot, pl.ds(0, 1), :],
                                  sem.at[slot]).wait()
            return c

        lax.fori_loop(0, tm, drain, 0, unroll=8)
        p = xbuf[slot]
        lo = lax.bitcast_convert_type(p << 16, f32).astype(bf16)
        hi = lax.bitcast_convert_type(p & jnp.uint32(0xFFFF0000), f32).astype(bf16)
        a1 = jnp.dot(lo, w1_ref[0:HALF, :], preferred_element_type=f32) \
            + jnp.dot(hi, w1_ref[HALF:, :], preferred_element_type=f32)
        a3 = jnp.dot(lo, w3_ref[0:HALF, :], preferred_element_type=f32) \
            + jnp.dot(hi, w3_ref[HALF:, :], preferred_element_type=f32)
        act = (a1 * _sigmoid(a1) * a3).astype(bf16)
        y_ref[...] = jnp.dot(act, w2_ref[...], preferred_element_type=f32)

    @pl.when(i >= n_used)
    def _():
        y_ref[...] = jnp.zeros_like(y_ref)


def _experts(te, n_used, src, hp, w1, w3, w2):
    tm = TM_EXP
    n_tiles = src.shape[0] // tm
    return pl.pallas_call(
        _expert_kernel,
        grid_spec=pltpu.PrefetchScalarGridSpec(
            num_scalar_prefetch=3,
            grid=(n_tiles,),
            in_specs=[
                pl.BlockSpec(memory_space=pl.ANY),
                pl.BlockSpec((None, D_MODEL, D_FF), lambda i, te, nu, src: (te[i], 0, 0)),
                pl.BlockSpec((None, D_MODEL, D_FF), lambda i, te, nu, src: (te[i], 0, 0)),
                pl.BlockSpec((None, D_FF, D_MODEL), lambda i, te, nu, src: (te[i], 0, 0)),
            ],
            out_specs=pl.BlockSpec((tm, D_MODEL), lambda i, te, nu, src: (i, 0)),
            scratch_shapes=[pltpu.VMEM((2, tm, HALF), u32), pltpu.SemaphoreType.DMA((2,))],
        ),
        out_shape=jax.ShapeDtypeStruct((src.shape[0], D_MODEL), f32),
        compiler_params=_params(("arbitrary",)),
        name="experts",
    )(te, n_used, src, hp, w1, w3, w2)


def _combine_kernel(pos_ref, x_ref, mod_ref, wt_ref, g_ref, b_ref, y_ref, o_ref, buf, sem, *,
                    m_rows, nblocks):
    tm = TM_CMB
    i = pl.program_id(0)

    def issue(blk, slot):
        base = blk * tm

        def it(j, c):
            p0 = pos_ref[base + j]
            p1 = pos_ref[m_rows + base + j]
            pltpu.make_async_copy(y_ref.at[pl.ds(p0, 1), :], buf.at[slot, 0, pl.ds(j, 1), :],
                                  sem.at[slot]).start()
            pltpu.make_async_copy(y_ref.at[pl.ds(p1, 1), :], buf.at[slot, 1, pl.ds(j, 1), :],
                                  sem.at[slot]).start()
            return c

        lax.fori_loop(0, tm, it, 0, unroll=8)

    @pl.when(i == 0)
    def _():
        issue(0, 0)

    @pl.when(i + 1 < nblocks)
    def _():
        issue(i + 1, (i + 1) % 2)

    slot = i % 2

    def drain(j, c):
        pltpu.make_async_copy(y_ref.at[pl.ds(0, 1), :], buf.at[slot, 0, pl.ds(0, 1), :],
                              sem.at[slot]).wait()
        return c

    lax.fori_loop(0, 2 * tm, drain, 0, unroll=8)

    wt = wt_ref[...]
    moe = wt[:, 0:1] * buf[slot, 0] + wt[:, 1:2] * buf[slot, 1]
    gate = mod_ref[5:6, :]
    o_ref[...] = _ln(ALPHA * x_ref[...] + gate * moe) * g_ref[...] + b_ref[...]


def _combine(pos, x, mod_l, wts, ln_g, ln_b, y):
    tm = TM_CMB
    m_rows = x.shape[0]
    nblocks = m_rows // tm
    return pl.pallas_call(
        functools.partial(_combine_kernel, m_rows=m_rows, nblocks=nblocks),
        grid_spec=pltpu.PrefetchScalarGridSpec(
            num_scalar_prefetch=1,
            grid=(nblocks,),
            in_specs=[
                pl.BlockSpec((tm, D_MODEL), lambda i, p: (i, 0)),
                pl.BlockSpec((None, 6, D_MODEL), lambda i, p: (_mod_row(i, tm), 0, 0)),
                pl.BlockSpec((tm, 128), lambda i, p: (i, 0)),
                pl.BlockSpec((1, D_MODEL), lambda i, p: (0, 0)),
                pl.BlockSpec((1, D_MODEL), lambda i, p: (0, 0)),
                pl.BlockSpec(memory_space=pl.ANY),
            ],
            out_specs=pl.BlockSpec((tm, D_MODEL), lambda i, p: (i, 0)),
            scratch_shapes=[pltpu.VMEM((2, 2, tm, D_MODEL), f32), pltpu.SemaphoreType.DMA((2,))],
        ),
        out_shape=jax.ShapeDtypeStruct((m_rows, D_MODEL), f32),
        compiler_params=_params(("arbitrary",)),
        name="combine",
    )(pos, x, mod_l, wts, ln_g, ln_b, y)


def kernel(x, c, ctx, c_ctx, w_ada, b_ada, w_in, b_igate, b_fgate, mh_norm_g, conv_w, conv_b, w_out,
           ln1_g, ln1_b, w_router, b_router, w1, w3, w2, ln2_g, ln2_b):
    n_state = 2 * HEADS * DQK + D_MLSTM
    cond_raw = jnp.zeros((8, D_MODEL), f32).at[:BATCH].set(c).at[BATCH].set(c_ctx)
    mod = _ada(cond_raw, w_ada, b_ada).reshape(DEPTH, 8, 6, D_MODEL)

    xa = jnp.concatenate([x.reshape(N_LAT, D_MODEL), ctx.reshape(N_CTX, D_MODEL)], axis=0)
    w_router_p = jnp.zeros((D_MODEL, 128), f32).at[:, :N_EXPERTS].set(w_router)
    w_router_hi = w_router_p.astype(bf16)
    w_router_lo = (w_router_p - w_router_hi.astype(f32)).astype(bf16)
    w_router_p = jnp.concatenate([w_router_hi, w_router_lo], axis=1)
    b_router_c = b_router.reshape(N_EXPERTS, 1)

    for l in range(DEPTH):
        last = l == DEPTH - 1
        w_main = jnp.concatenate([w_in[l][:, :n_state], w_in[l][:, n_state + N_GATE:]], axis=1).astype(bf16)
        w_gate = jnp.zeros((D_MODEL, 128), bf16).at[:, :N_GATE].set(
            w_in[l][:, n_state:n_state + N_GATE].astype(bf16))
        proj, gates = _inproj(xa, mod[l], w_main, w_gate)

        bias = jnp.concatenate([b_igate[l], b_fgate[l]]).astype(f32)
        m_lat, m_ctx = _mlstm3(proj, gates, bias, mh_norm_g[l].reshape(1, D_MLSTM), not last)

        xn, hp, idx8, wts = _mix(xa, mod[l], m_lat, m_ctx, proj, conv_w[l], conv_b[l].reshape(1, D_CONV),
                                 w_out[l].astype(bf16), ln1_g[l].reshape(1, D_MODEL),
                                 ln1_b[l].reshape(1, D_MODEL), w_router_p, b_router_c, not last)

        m_rows = xn.shape[0]
        pos, src, te, n_used = _route(idx8, m_rows)
        y = _experts(te, n_used, src, hp, w1[l].astype(bf16), w3[l].astype(bf16), w2[l].astype(bf16))
        xa = _combine(pos, xn, mod[l], wts, ln2_g[l].reshape(1, D_MODEL), ln2_b[l].reshape(1, D_MODEL), y)

    return xa.reshape(BATCH, SEQ, D_MODEL)
```

```python
import functools

import jax
import jax.numpy as jnp
from jax import lax
from jax.experimental import pallas as pl
from jax.experimental.pallas import tpu as pltpu

f32 = jnp.float32
bf16 = jnp.bfloat16
i32 = jnp.int32
u32 = jnp.uint32

D_MODEL = 2048
BATCH = 4
SEQ = 2048
DEPTH = 4
GRID_W = 64
CTX_LEN = 256
D_MLSTM = 1024
HEADS = 4
DV = 256
DQK = 128
CHUNK = 64
D_CONV = 1024
D_CONV_H = 512
N_EXPERTS = 16
N_GROUPS = 4
EPG = 4
D_FF = 1024
ALPHA = (2 * DEPTH) ** 0.25
LN_EPS = 1e-6
QK_SCALE = DQK ** -0.5

N_LAT = BATCH * SEQ
N_CTX = BATCH * CTX_LEN
N_ALL = N_LAT + N_CTX
D_PROJ = 6144
N_GATE = 16
HALF = D_MODEL // 2

TM_IN = 1024
TN_IN = 512
TM_MIX = 256
TM_EXP = 256
TM_CMB = 256
VMEM_LIMIT = 56 * 1024 * 1024

HIGHEST = lax.Precision.HIGHEST


def _sigmoid(x):
    return 1.0 / (1.0 + jnp.exp(-x))


def _log_sigmoid(x):
    return jnp.minimum(x, 0.0) - jnp.log1p(jnp.exp(-jnp.abs(x)))


def _ln(x):
    mu = jnp.mean(x, axis=-1, keepdims=True)
    xc = x - mu
    var = jnp.mean(xc * xc, axis=-1, keepdims=True)
    return xc * lax.rsqrt(var + LN_EPS)


def _mod_row(i, tm):
    return jnp.minimum((i * tm) // SEQ, BATCH)


def _params(sem, vmem=VMEM_LIMIT):
    return pltpu.CompilerParams(dimension_semantics=sem, vmem_limit_bytes=vmem)


def _ada_kernel(c_ref, w_ref, b_ref, o_ref):
    c = c_ref[...]
    cond = c * _sigmoid(c)
    o_ref[...] = jnp.dot(cond, w_ref[...], preferred_element_type=f32) + b_ref[...]


def _ada(cond_raw, w_ada, b_ada):
    tn = 1024
    n = 6 * D_MODEL
    return pl.pallas_call(
        _ada_kernel,
        grid=(DEPTH, n // tn),
        in_specs=[
            pl.BlockSpec((8, D_MODEL), lambda l, j: (0, 0)),
            pl.BlockSpec((None, D_MODEL, tn), lambda l, j: (l, 0, j)),
            pl.BlockSpec((None, 1, tn), lambda l, j: (l, 0, j)),
        ],
        out_specs=pl.BlockSpec((None, 8, tn), lambda l, j: (l, 0, j)),
        out_shape=jax.ShapeDtypeStruct((DEPTH, 8, n), f32),
        compiler_params=_params(("parallel", "parallel")),
        name="ada",
    )(cond_raw, w_ada, b_ada.reshape(DEPTH, 1, n))


def _inproj_kernel(x_ref, mod_ref, wa_ref, wb_ref, wg_ref, o_ref, g_ref, h_scr, *, slab):
    j = pl.program_id(1)

    @pl.when(j == 0)
    def _():
        shift = mod_ref[0:1, :]
        scale = mod_ref[1:2, :]

        def body(s, c):
            r0 = pl.multiple_of(s * slab, slab)
            h = _ln(x_ref[pl.ds(r0, slab), :]) * (1.0 + scale) + shift
            hb = h.astype(bf16)
            h_scr[pl.ds(r0, slab), :] = hb
            g_ref[pl.ds(r0, slab), :] = jnp.dot(hb, wg_ref[...], preferred_element_type=f32)
            return c

        lax.fori_loop(0, TM_IN // slab, body, 0)

    @pl.when(j < N_STATE_BLOCKS)
    def _():
        o_ref[...] = jnp.dot(h_scr[...], wa_ref[...].astype(bf16),
                             preferred_element_type=f32).astype(o_ref.dtype)

    @pl.when(j >= N_STATE_BLOCKS)
    def _():
        o_ref[...] = jnp.dot(h_scr[...], wb_ref[...], preferred_element_type=f32).astype(o_ref.dtype)


N_STATE_BLOCKS = (2 * HEADS * DQK + D_MLSTM) // TN_IN


def _inproj(x, mod_l, w_in, w_tail, w_gate, l):
    m = x.shape[0]
    nsb = N_STATE_BLOCKS
    return pl.pallas_call(
        functools.partial(_inproj_kernel, slab=128),
        grid=(m // TM_IN, D_PROJ // TN_IN),
        in_specs=[
            pl.BlockSpec((TM_IN, D_MODEL), lambda i, j: (i, 0)),
            pl.BlockSpec((None, 6, D_MODEL), lambda i, j: (_mod_row(i, TM_IN), 0, 0)),
            pl.BlockSpec((None, D_MODEL, TN_IN), lambda i, j: (l, 0, jnp.minimum(j, nsb - 1))),
            pl.BlockSpec((None, D_MODEL, TN_IN), lambda i, j: (l, 0, jnp.maximum(j - nsb, 0))),
            pl.BlockSpec((None, D_MODEL, 128), lambda i, j: (l, 0, 0)),
        ],
        out_specs=[
            pl.BlockSpec((TM_IN, TN_IN), lambda i, j: (i, j)),
            pl.BlockSpec((TM_IN, 128), lambda i, j: (i, 0)),
        ],
        out_shape=[
            jax.ShapeDtypeStruct((m, D_PROJ), bf16),
            jax.ShapeDtypeStruct((m, 128), f32),
        ],
        scratch_shapes=[pltpu.VMEM((TM_IN, D_MODEL), bf16)],
        compiler_params=_params(("parallel", "arbitrary")),
        name="inproj",
    )(x, mod_l, w_in, w_tail, w_gate)


def _mlstm_chunk(q, k, v, fc_raw, ic_raw, fr_raw, ir_raw, b_i, b_f, ct_ref, n_ref, d, m, fwd):
    L = CHUNK
    lf_c = _log_sigmoid(fc_raw + b_f)
    i_c = ic_raw + b_i
    lf_r = _log_sigmoid(fr_raw + b_f)
    i_r = ir_raw + b_i
    rr = lax.broadcasted_iota(i32, (L, L), 0)
    cc = lax.broadcasted_iota(i32, (L, L), 1)
    lo = rr >= cc
    up = rr <= cc
    mask = lo if fwd else up
    a_mat = mask.astype(f32)
    a_t = (up if fwd else lo).astype(f32)
    bcol = jnp.dot(a_mat, jnp.broadcast_to(lf_c, (L, L)), precision=HIGHEST,
                   preferred_element_type=f32)
    brow = jnp.dot(jnp.broadcast_to(lf_r, (L, L)), a_t, precision=HIGHEST,
                   preferred_element_type=f32)
    dm = jnp.where(mask, bcol - brow + i_r, -jnp.inf)
    mloc = jnp.max(dm, axis=1, keepdims=True)
    wloc = jnp.exp(dm - mloc)
    b1 = bcol[:, 0:1]
    m_t = jnp.maximum(b1 + m, mloc)
    inter = jnp.exp(b1 + m - m_t)
    a = jnp.exp(mloc - m_t)
    qk = lax.dot_general(q, k, (((1,), (1,)), ((), ())), preferred_element_type=f32) * QK_SCALE
    sloc = qk * wloc
    ct = ct_ref[d]
    n = n_ref[d]
    num = inter * jnp.dot(q, ct.astype(bf16), preferred_element_type=f32) \
        + a * jnp.dot(sloc.astype(bf16), v, preferred_element_type=f32)
    den = inter * jnp.sum(q.astype(f32) * n, axis=1, keepdims=True) \
        + a * jnp.sum(sloc, axis=1, keepdims=True)
    hout = num / jnp.maximum(jnp.abs(den), jnp.exp(-m_t))
    last = L - 1 if fwd else 0
    b_last = b1[last:last + 1, :]
    wl = jnp.exp(b_last - b1 + i_c - mloc[last:last + 1, :])
    kw = k.astype(f32) * (wl * QK_SCALE)
    u = lax.dot_general(kw.astype(bf16), v, (((0,), (0,)), ((), ())), preferred_element_type=f32)
    decay = inter[last:last + 1, :]
    a_l = a[last:last + 1, :]
    ct_ref[d] = decay * ct + a_l * u
    n_ref[d] = decay * n + a_l * jnp.sum(kw, axis=0, keepdims=True)
    return hout, m_t[last:last + 1, :]


def _mlstm_kernel(bias_ref, ql_ref, kl_ref, vl_ref, ol_ref, qc_ref, kc_ref, vc_ref, oc_ref,
                  gcl_ref, grl_ref, gcc_ref, grc_ref, gain_ref, *rest, has_ctx_out):
    if has_ctx_out:
        ml_ref, mc_ref, ct_ref, n_ref, hfl, hbl, hfc, hbc = rest
    else:
        ml_ref, ct_ref, n_ref, hfl, hbl = rest
        mc_ref = hfc = hbc = None
    b = pl.program_id(0)
    h = pl.program_id(1)
    L = CHUNK
    bi = (bias_ref[h], bias_ref[HEADS + h])
    bf = (bias_ref[2 * HEADS + h], bias_ref[3 * HEADS + h])
    ct_ref[...] = jnp.zeros_like(ct_ref)
    n_ref[...] = jnp.zeros_like(n_ref)

    def run(q_ref, k_ref, v_ref, gc_ref, gr_ref, c_base, nchunks, hf, hb, carry):
        def body(j, carry):
            m_f, m_b = carry
            outs = []
            for d, fwd in ((0, True), (1, False)):
                c = j if fwd else nchunks - 1 - j
                r0 = pl.multiple_of(c * L, L)
                q = q_ref[pl.ds(r0, L), :]
                k = k_ref[pl.ds(r0, L), :]
                v = v_ref[pl.ds(r0, L), :]
                gcol = gc_ref[c_base + c]
                ic = gcol[:, d:d + 1]
                fc = gcol[:, 2 + d:3 + d]
                ir = gr_ref[d, pl.ds(c_base + c, 1), :]
                fr = gr_ref[2 + d, pl.ds(c_base + c, 1), :]
                hout, m_new = _mlstm_chunk(q, k, v, fc, ic, fr, ir, bi[d], bf[d], ct_ref, n_ref, d,
                                           m_f if fwd else m_b, fwd)
                if hf is not None:
                    (hf if fwd else hb)[pl.ds(r0, L), :] = hout
                outs.append(m_new)
            return tuple(outs)

        return lax.fori_loop(0, nchunks, body, carry)

    zero = jnp.zeros((1, 1), f32)
    carry = run(qc_ref, kc_ref, vc_ref, gcc_ref, grc_ref, b * (CTX_LEN // L), CTX_LEN // L, hfc, hbc,
                (zero, zero))
    run(ql_ref, kl_ref, vl_ref, gcl_ref, grl_ref, 0, SEQ // L, hfl, hbl, carry)

    gain = gain_ref[...]

    def finish(hf, hb, o_ref, out_ref, rows):
        slab = 256

        def ep(s, c):
            r0 = pl.multiple_of(s * slab, slab)
            hs = hf[pl.ds(r0, slab), :] + hb[pl.ds(r0, slab), :]
            r = lax.rsqrt(jnp.mean(hs * hs, axis=-1, keepdims=True) + LN_EPS)
            o = o_ref[pl.ds(r0, slab), :].astype(f32)
            out_ref[pl.ds(r0, slab), :] = (hs * r * gain * _sigmoid(o)).astype(out_ref.dtype)
            return c

        lax.fori_loop(0, rows // slab, ep, 0)

    finish(hfl, hbl, ol_ref, ml_ref, SEQ)
    if has_ctx_out:
        finish(hfc, hbc, oc_ref, mc_ref, CTX_LEN)


def _mlstm(proj, gates, bias, gain, has_ctx_out):
    L = CHUNK
    g = gates[:, :N_GATE]
    g4 = jnp.stack([g[:, 0:4], g[:, 4:8], g[:, 8:12], g[:, 12:16]], axis=-1)
    gh = jnp.transpose(g4, (1, 0, 2))
    gcol_l = gh[:, :N_LAT].reshape(HEADS, N_LAT // L, L, 4)
    gcol_c = gh[:, N_LAT:].reshape(HEADS, N_CTX // L, L, 4)
    gr = jnp.transpose(g4, (1, 2, 0))
    grow_l = gr[:, :, :N_LAT].reshape(HEADS, 4, N_LAT // L, L)
    grow_c = gr[:, :, N_LAT:].reshape(HEADS, 4, N_CTX // L, L)

    lat_rb = lambda b, h: b
    ctx_rb = lambda b, h: N_LAT // CTX_LEN + b
    in_specs = [
        pl.BlockSpec(memory_space=pltpu.SMEM),
        pl.BlockSpec((SEQ, DQK), lambda b, h: (lat_rb(b, h), h)),
        pl.BlockSpec((SEQ, DQK), lambda b, h: (lat_rb(b, h), HEADS + h)),
        pl.BlockSpec((SEQ, DV), lambda b, h: (lat_rb(b, h), HEADS + h)),
        pl.BlockSpec((SEQ, DV), lambda b, h: (lat_rb(b, h), 2 * HEADS + h)),
        pl.BlockSpec((CTX_LEN, DQK), lambda b, h: (ctx_rb(b, h), h)),
        pl.BlockSpec((CTX_LEN, DQK), lambda b, h: (ctx_rb(b, h), HEADS + h)),
        pl.BlockSpec((CTX_LEN, DV), lambda b, h: (ctx_rb(b, h), HEADS + h)),
        pl.BlockSpec((CTX_LEN, DV), lambda b, h: (ctx_rb(b, h), 2 * HEADS + h)),
        pl.BlockSpec((None, SEQ // L, L, 4), lambda b, h: (h, b, 0, 0)),
        pl.BlockSpec((None, 4, SEQ // L, L), lambda b, h: (h, 0, b, 0)),
        pl.BlockSpec((None, N_CTX // L, L, 4), lambda b, h: (h, 0, 0, 0)),
        pl.BlockSpec((None, 4, N_CTX // L, L), lambda b, h: (h, 0, 0, 0)),
        pl.BlockSpec((1, DV), lambda b, h: (0, h)),
    ]
    out_specs = [pl.BlockSpec((SEQ, DV), lambda b, h: (b, h))]
    out_shape = [jax.ShapeDtypeStruct((N_LAT, D_MLSTM), bf16)]
    scratch = [pltpu.VMEM((2, DQK, DV), f32), pltpu.VMEM((2, 1, DQK), f32),
               pltpu.VMEM((SEQ, DV), f32), pltpu.VMEM((SEQ, DV), f32)]
    if has_ctx_out:
        out_specs.append(pl.BlockSpec((CTX_LEN, DV), lambda b, h: (b, h)))
        out_shape.append(jax.ShapeDtypeStruct((N_CTX, D_MLSTM), bf16))
        scratch += [pltpu.VMEM((CTX_LEN, DV), f32), pltpu.VMEM((CTX_LEN, DV), f32)]
    res = pl.pallas_call(
        functools.partial(_mlstm_kernel, has_ctx_out=has_ctx_out),
        grid=(BATCH, HEADS),
        in_specs=in_specs,
        out_specs=out_specs,
        out_shape=out_shape,
        scratch_shapes=scratch,
        compiler_params=_params(("parallel", "parallel")),
        name="mlstm",
    )(bias, proj, proj, proj, proj, proj, proj, proj, proj, gcol_l, grow_l, gcol_c, grow_c, gain)
    return res if has_ctx_out else (res[0], None)


GROUP = 4
N_CHUNK_CTX = CTX_LEN // CHUNK
N_CHUNK_LAT = SEQ // CHUNK
N_CHUNK = N_CHUNK_CTX + N_CHUNK_LAT
N_GROUP_LAT = N_CHUNK_LAT // GROUP
LAT_ROW0 = 8
D_AUG = DV + 128


def _mlstm3_kernel(bias_ref, ql_ref, kl_ref, vl_ref, ol_ref, qc_ref, kc_ref, vc_ref, oc_ref,
                   gcl_ref, grl_ref, gcc_ref, grc_ref, gain_ref, *rest, has_ctx_out):
    if has_ctx_out:
        ml_ref, mc_ref, s_ref, st_ref, msc, rows_s, cols_s = rest
    else:
        ml_ref, s_ref, st_ref, msc, rows_s, cols_s = rest
        mc_ref = None
    L = CHUNK
    b = pl.program_id(0)
    h = pl.program_id(1)
    bi = (bias_ref[h], bias_ref[HEADS + h])
    bf = (bias_ref[2 * HEADS + h], bias_ref[3 * HEADS + h])

    rr = lax.broadcasted_iota(i32, (L, L), 0)
    cc = lax.broadcasted_iota(i32, (L, L), 1)
    lo_mask = rr >= cc
    up_mask = rr <= cc
    masks = (lo_mask, up_mask)

    def row_forms(gr_ref, sl, dst0, n):
        for d in range(2):
            tri = (up_mask if d == 0 else lo_mask).astype(f32)
            i_r = gr_ref[d, sl, :] + bi[d]
            lf_r = _log_sigmoid(gr_ref[2 + d, sl, :] + bf[d])
            b_r = jnp.dot(lf_r, tri, precision=HIGHEST, preferred_element_type=f32)
            rows_s[d, dst0:dst0 + n, :] = i_r - b_r

    row_forms(grc_ref, pl.ds(b * N_CHUNK_CTX, N_CHUNK_CTX), 0, N_CHUNK_CTX)
    row_forms(grl_ref, slice(None), LAT_ROW0, N_CHUNK_LAT)

    kind = lax.broadcasted_iota(i32, (1, 4 * GROUP), 1) % 4
    bias_v = jnp.where(kind == 0, bi[0], jnp.where(kind == 1, bi[1], jnp.where(kind == 2, bf[0], bf[1])))
    tpos = lax.broadcasted_iota(i32, (L, 4 * GROUP), 0)

    def col_forms(x):
        y = x + bias_v
        y = jnp.where(kind >= 2, _log_sigmoid(y), y)
        pre = y
        suf = y
        s = 1
        while s < L:
            pre = pre + jnp.where(tpos >= s, pltpu.roll(pre, s, 0), 0.0)
            suf = suf + jnp.where(tpos < L - s, pltpu.roll(suf, L - s, 0), 0.0)
            s *= 2
        return jnp.where(kind == 2, pre, jnp.where(kind == 3, suf, y))

    cols_s[0] = col_forms(gcc_ref[...])

    def col_body(g, c):
        cols_s[g + 1] = col_forms(gcl_ref[g])
        return c

    lax.fori_loop(0, N_GROUP_LAT, col_body, 0)

    ones_col = (lax.broadcasted_iota(i32, (L, 128), 1) == 0).astype(bf16)

    s_ref[...] = jnp.zeros_like(s_ref)

    def state_step(d, c, row, col, j, k, v, m):
        i_c = col[:, 4 * j + d:4 * j + d + 1]
        b1 = col[:, 4 * j + 2 + d:4 * j + 3 + d]
        last = L - 1 if d == 0 else 0
        b_last = b1[last:last + 1, :]
        ct = i_c - b1
        mx = jnp.max(ct, axis=0, keepdims=True)
        wl = jnp.exp(ct - mx)
        mloc_last = b_last + mx
        m_new = jnp.maximum(b_last + m, mloc_last)
        decay = jnp.exp(b_last + m - m_new)
        a_l = jnp.exp(mloc_last - m_new)
        s_old = s_ref[d]
        st_ref[d, c] = s_old.astype(bf16)
        msc[d, pl.ds(row, 1), :] = jnp.broadcast_to(m, (1, 128))
        kw = (k.astype(f32) * (wl * QK_SCALE)).astype(bf16)
        vaug = jnp.concatenate([v, ones_col], axis=1)
        u = lax.dot_general(kw, vaug, (((0,), (0,)), ((), ())), preferred_element_type=f32)
        s_ref[d] = decay * s_old + a_l * u
        return m_new

    zero = jnp.zeros((1, 1), f32)
    m_f = m_b = zero
    col0 = cols_s[0]
    for step in range(GROUP):
        jf, jb = step, GROUP - 1 - step
        m_f = state_step(0, jf, jf, col0, jf, kc_ref[jf * L:(jf + 1) * L, :], vc_ref[jf * L:(jf + 1) * L, :], m_f)
        m_b = state_step(1, jb, jb, col0, jb, kc_ref[jb * L:(jb + 1) * L, :], vc_ref[jb * L:(jb + 1) * L, :], m_b)

    def state_body(it, carry):
        m_f, m_b = carry
        gf = it
        gb = N_GROUP_LAT + 1 - it
        colf = cols_s[gf]
        colb = cols_s[gb]
        for step in range(GROUP):
            jf, jb = step, GROUP - 1 - step
            clf = (gf - 1) * GROUP + jf
            clb = (gb - 1) * GROUP + jb
            rf = pl.multiple_of(clf * L, L)
            rb = pl.multiple_of(clb * L, L)
            m_f = state_step(0, N_CHUNK_CTX + clf, LAT_ROW0 + clf, colf, jf,
                             kl_ref[pl.ds(rf, L), :], vl_ref[pl.ds(rf, L), :], m_f)
            m_b = state_step(1, N_CHUNK_CTX + clb, LAT_ROW0 + clb, colb, jb,
                             kl_ref[pl.ds(rb, L), :], vl_ref[pl.ds(rb, L), :], m_b)
        return m_f, m_b

    lax.fori_loop(1, N_GROUP_LAT + 1, state_body, (m_f, m_b))

    gain = gain_ref[...]

    def out_chunk(q, k, v, o, col, j, c, row, out_ref, r0):
        qk = lax.dot_general(q, k, (((1,), (1,)), ((), ())), preferred_element_type=f32) * QK_SCALE
        vaug = jnp.concatenate([v, ones_col], axis=1)
        sl = []
        per = []
        for d in range(2):
            b1 = col[:, 4 * j + 2 + d:4 * j + 3 + d]
            dm = jnp.where(masks[d], b1 + rows_s[d, pl.ds(row, 1), :], -jnp.inf)
            mloc = jnp.max(dm, axis=1, keepdims=True)
            wloc = jnp.exp(dm - mloc)
            m_prev = msc[d, pl.ds(row, 1), :][:, 0:1]
            m_t = jnp.maximum(b1 + m_prev, mloc)
            inter = jnp.exp(b1 + m_prev - m_t)
            a = jnp.exp(mloc - m_t)
            sl.append((qk * wloc).astype(bf16))
            per.append((m_t, inter, a))
        x = jnp.dot(jnp.concatenate(sl, axis=0), vaug, preferred_element_type=f32)
        hs = None
        for d in range(2):
            y = jnp.dot(q, st_ref[d, c], preferred_element_type=f32)
            m_t, inter, a = per[d]
            xd = x[d * L:(d + 1) * L, :]
            den = inter * y[:, DV:DV + 1] + a * xd[:, DV:DV + 1]
            rinv = 1.0 / jnp.maximum(jnp.abs(den), jnp.exp(-m_t))
            hd = (inter * rinv) * y[:, :DV] + (a * rinv) * xd[:, :DV]
            hs = hd if hs is None else hs + hd
        r = lax.rsqrt(jnp.mean(hs * hs, axis=-1, keepdims=True) + LN_EPS)
        out_ref[pl.ds(r0, L), :] = (hs * r * gain * _sigmoid(o.astype(f32))).astype(out_ref.dtype)

    if has_ctx_out:
        for j in range(GROUP):
            sl_ = slice(j * L, (j + 1) * L)
            out_chunk(qc_ref[sl_, :], kc_ref[sl_, :], vc_ref[sl_, :], oc_ref[sl_, :], col0, j, j, j,
                      mc_ref, j * L)

    def out_body(g, carry):
        col = cols_s[g + 1]
        for j in range(GROUP):
            cl = g * GROUP + j
            r0 = pl.multiple_of(cl * L, L)
            out_chunk(ql_ref[pl.ds(r0, L), :], kl_ref[pl.ds(r0, L), :], vl_ref[pl.ds(r0, L), :],
                      ol_ref[pl.ds(r0, L), :], col, j, N_CHUNK_CTX + cl, LAT_ROW0 + cl, ml_ref, r0)
        return carry

    lax.fori_loop(0, N_GROUP_LAT, out_body, 0)


def _mlstm3(proj, gates, bias, gain, has_ctx_out):
    L = CHUNK
    g = gates[:, :N_GATE]
    g4 = jnp.stack([g[:, 0:4], g[:, 4:8], g[:, 8:12], g[:, 12:16]], axis=-1)
    gh = jnp.transpose(g4, (1, 0, 2))
    gcol_l = gh[:, :N_LAT].reshape(HEADS, BATCH * N_GROUP_LAT, GROUP, L, 4)
    gcol_l = jnp.transpose(gcol_l, (0, 1, 3, 2, 4)).reshape(HEADS, BATCH * N_GROUP_LAT, L, 4 * GROUP)
    gcol_c = gh[:, N_LAT:].reshape(HEADS, BATCH, GROUP, L, 4)
    gcol_c = jnp.transpose(gcol_c, (0, 1, 3, 2, 4)).reshape(HEADS, BATCH, L, 4 * GROUP)
    gr = jnp.transpose(g4, (1, 2, 0))
    grow_l = gr[:, :, :N_LAT].reshape(HEADS, 4, N_LAT // L, L)
    grow_c = gr[:, :, N_LAT:].reshape(HEADS, 4, N_CTX // L, L)

    ctx_rb = lambda b, h: N_LAT // CTX_LEN + b
    in_specs = [
        pl.BlockSpec(memory_space=pltpu.SMEM),
        pl.BlockSpec((SEQ, DQK), lambda b, h: (b, h)),
        pl.BlockSpec((SEQ, DQK), lambda b, h: (b, HEADS + h)),
        pl.BlockSpec((SEQ, DV), lambda b, h: (b, HEADS + h)),
        pl.BlockSpec((SEQ, DV), lambda b, h: (b, 2 * HEADS + h)),
        pl.BlockSpec((CTX_LEN, DQK), lambda b, h: (ctx_rb(b, h), h)),
        pl.BlockSpec((CTX_LEN, DQK), lambda b, h: (ctx_rb(b, h), HEADS + h)),
        pl.BlockSpec((CTX_LEN, DV), lambda b, h: (ctx_rb(b, h), HEADS + h)),
        pl.BlockSpec((CTX_LEN, DV), lambda b, h: (ctx_rb(b, h), 2 * HEADS + h)),
        pl.BlockSpec((None, N_GROUP_LAT, L, 4 * GROUP), lambda b, h: (h, b, 0, 0)),
        pl.BlockSpec((None, 4, N_CHUNK_LAT, L), lambda b, h: (h, 0, b, 0)),
        pl.BlockSpec((None, None, L, 4 * GROUP), lambda b, h: (h, b, 0, 0)),
        pl.BlockSpec((None, 4, N_CTX // L, L), lambda b, h: (h, 0, 0, 0)),
        pl.BlockSpec((1, DV), lambda b, h: (0, h)),
    ]
    out_specs = [pl.BlockSpec((SEQ, DV), lambda b, h: (b, h))]
    out_shape = [jax.ShapeDtypeStruct((N_LAT, D_MLSTM), bf16)]
    if has_ctx_out:
        out_specs.append(pl.BlockSpec((CTX_LEN, DV), lambda b, h: (b, h)))
        out_shape.append(jax.ShapeDtypeStruct((N_CTX, D_MLSTM), bf16))
    scratch = [
        pltpu.VMEM((2, DQK, D_AUG), f32),
        pltpu.VMEM((2, N_CHUNK, DQK, D_AUG), bf16),
        pltpu.VMEM((2, LAT_ROW0 + N_CHUNK_LAT, 128), f32),
        pltpu.VMEM((2, LAT_ROW0 + N_CHUNK_LAT, L), f32),
        pltpu.VMEM((N_GROUP_LAT + 1, L, 4 * GROUP), f32),
    ]
    res = pl.pallas_call(
        functools.partial(_mlstm3_kernel, has_ctx_out=has_ctx_out),
        grid=(BATCH, HEADS),
        in_specs=in_specs,
        out_specs=out_specs,
        out_shape=out_shape,
        scratch_shapes=scratch,
        compiler_params=_params(("parallel", "parallel")),
        name="mlstm",
    )(bias, proj, proj, proj, proj, proj, proj, proj, proj, gcol_l, grow_l, gcol_c, grow_c, gain)
    return res if has_ctx_out else (res[0], None)


def _top2_rows(vals):
    best = vals[0]
    bi = jnp.zeros(best.shape, i32)
    for j in range(1, len(vals)):
        take = vals[j] > best
        best = jnp.where(take, vals[j], best)
        bi = jnp.where(take, j, bi)
    sec = None
    si = None
    for j in range(len(vals)):
        cand = jnp.where(bi == j, -jnp.inf, vals[j])
        if sec is None:
            sec, si = cand, jnp.zeros(best.shape, i32)
        else:
            take = cand > sec
            sec = jnp.where(take, cand, sec)
            si = jnp.where(take, j, si)
    return bi, si


def _mix_kernel(x_ref, mod_ref, ml_ref, mc_ref, u_ref, bg_ref, cg_ref, ut_ref, ct_ref, ub_ref, cb_ref,
                cw_ref, cbias_ref, wo_ref, g1_ref, b1_ref, wr_ref, br_ref,
                xo_ref, hp_ref, idx_ref, wt_ref, m_scr, y_scr, *, n_lat_blocks, has_ctx):
    tm = TM_MIX
    i = pl.program_id(0)
    gate = mod_ref[2:3, :]
    shift2 = mod_ref[3:4, :]
    scale2 = mod_ref[4:5, :]
    cw = cw_ref[...]
    cbias = cbias_ref[...]
    row = lax.broadcasted_iota(i32, (tm, 1), 0)

    def shifted(z, first, last):
        prev = jnp.where(first, 0.0, pltpu.roll(z, 1, 0))
        nxt = jnp.where(last, 0.0, pltpu.roll(z, tm - 1, 0))
        return prev, nxt

    def lat_branch():
        z = cg_ref[...].astype(f32) * u_ref[...].astype(f32)
        col = row % GRID_W
        zh = z[:, :D_CONV_H]
        prev, nxt = shifted(zh, col == 0, col == GRID_W - 1)
        yh = cw[0:1, :D_CONV_H] * prev + cw[1:2, :D_CONV_H] * zh + cw[2:3, :D_CONV_H] * nxt
        bpb = SEQ // tm
        top_ok = (i % bpb != 0).astype(f32)
        bot_ok = (i % bpb != bpb - 1).astype(f32)
        zt = ct_ref[...].astype(f32) * ut_ref[...].astype(f32) * top_ok
        zb = cb_ref[...].astype(f32) * ub_ref[...].astype(f32) * bot_ok
        zv = z[:, D_CONV_H:]
        zext = jnp.concatenate([zt, zv, zb], axis=0)
        yv = cw[0:1, D_CONV_H:] * zext[0:tm] + cw[1:2, D_CONV_H:] * zv \
            + cw[2:3, D_CONV_H:] * zext[2 * GRID_W:2 * GRID_W + tm]
        y = jnp.concatenate([yh, yv], axis=1) + cbias
        y_scr[...] = (bg_ref[...].astype(f32) * y).astype(bf16)
        m_scr[...] = ml_ref[...]

    def ctx_branch():
        z = cg_ref[...].astype(f32) * u_ref[...].astype(f32)
        pos = row % CTX_LEN
        prev, nxt = shifted(z, pos == 0, pos == CTX_LEN - 1)
        y = cw[0:1, :] * prev + cw[1:2, :] * z + cw[2:3, :] * nxt + cbias
        y_scr[...] = (bg_ref[...].astype(f32) * y).astype(bf16)
        m_scr[...] = mc_ref[...]

    if has_ctx:
        pl.when(i < n_lat_blocks)(lat_branch)
        pl.when(i >= n_lat_blocks)(ctx_branch)
    else:
        lat_branch()

    out = jnp.dot(m_scr[...], wo_ref[0:D_MLSTM, :], preferred_element_type=f32) \
        + jnp.dot(y_scr[...], wo_ref[D_MLSTM:, :], preferred_element_type=f32)
    xn = _ln(ALPHA * x_ref[...] + gate * out) * g1_ref[...] + b1_ref[...]
    xo_ref[...] = xn
    h2 = _ln(xn) * (1.0 + scale2) + shift2

    hp_ref[...] = h2

    h_hi = h2.astype(bf16)
    hs = jnp.concatenate([h_hi, (h2 - h_hi.astype(f32)).astype(bf16)], axis=0)
    pr = jnp.dot(hs, wr_ref[...], preferred_element_type=f32)
    logits = pr[:tm, :128] + (pr[:tm, 128:] + pr[tm:, :128])
    lt = logits.T
    s = _sigmoid(lt[0:N_EXPERTS, :])
    sb = s + br_ref[...]
    sb_rows = [sb[e:e + 1, :] for e in range(N_EXPERTS)]
    s_rows = [s[e:e + 1, :] for e in range(N_EXPERTS)]
    gscores = []
    for g in range(N_GROUPS):
        a_, b_, c_, d_ = sb_rows[EPG * g:EPG * g + EPG]
        hi1, lo1 = jnp.maximum(a_, b_), jnp.minimum(a_, b_)
        hi2, lo2 = jnp.maximum(c_, d_), jnp.minimum(c_, d_)
        top = jnp.maximum(hi1, hi2)
        second = jnp.maximum(jnp.minimum(hi1, hi2), jnp.maximum(lo1, lo2))
        gscores.append(top + second)
    gbest = gscores[0]
    gsel = jnp.zeros(gbest.shape, i32)
    for g in range(1, N_GROUPS):
        take = gscores[g] > gbest
        gbest = jnp.where(take, gscores[g], gbest)
        gsel = jnp.where(take, g, gsel)

    def pick_group(rows, j):
        v = rows[j]
        for g in range(1, N_GROUPS):
            v = jnp.where(gsel == g, rows[EPG * g + j], v)
        return v

    cand_b = [pick_group(sb_rows, j) for j in range(EPG)]
    cand_s = [pick_group(s_rows, j) for j in range(EPG)]
    i1, i2 = _top2_rows(cand_b)

    def pick_idx(rows, idx):
        v = rows[0]
        for j in range(1, EPG):
            v = jnp.where(idx == j, rows[j], v)
        return v

    s1 = pick_idx(cand_s, i1)
    s2 = pick_idx(cand_s, i2)
    tot = s1 + s2
    w1 = s1 / tot
    w2 = s2 / tot
    e1 = gsel * EPG + i1
    e2 = gsel * EPG + i2
    r8 = lax.broadcasted_iota(i32, (8, tm), 0)
    idx_ref[...] = jnp.where(r8 == 0, e1, jnp.where(r8 == 1, e2, 0))
    r128 = lax.broadcasted_iota(i32, (128, tm), 0)
    wmat = jnp.where(r128 == 0, w1, jnp.where(r128 == 1, w2, 0.0))
    wt_ref[...] = wmat.T


def _mix(x, mod_l, m_lat, m_ctx, proj, conv_w, conv_b, w_out, ln_g, ln_b, w_router, b_router, has_ctx):
    tm = TM_MIX
    m_rows = N_ALL if has_ctx else N_LAT
    nlb = N_LAT // tm
    nblocks = m_rows // tm
    hb = tm // GRID_W
    n_hblocks = N_ALL // GRID_W
    if m_ctx is None:
        m_ctx = m_lat
    ncb = m_ctx.shape[0] // tm
    in_specs = [
        pl.BlockSpec((tm, D_MODEL), lambda i: (i, 0)),
        pl.BlockSpec((None, 6, D_MODEL), lambda i: (_mod_row(i, tm), 0, 0)),
        pl.BlockSpec((tm, D_MLSTM), lambda i: (jnp.minimum(i, nlb - 1), 0)),
        pl.BlockSpec((tm, D_MLSTM), lambda i: (jnp.clip(i - nlb, 0, ncb - 1), 0)),
        pl.BlockSpec((tm, D_CONV), lambda i: (i, 3)),
        pl.BlockSpec((tm, D_CONV), lambda i: (i, 4)),
        pl.BlockSpec((tm, D_CONV), lambda i: (i, 5)),
        pl.BlockSpec((GRID_W, D_CONV_H), lambda i: (jnp.maximum(i * hb - 1, 0), 7)),
        pl.BlockSpec((GRID_W, D_CONV_H), lambda i: (jnp.maximum(i * hb - 1, 0), 11)),
        pl.BlockSpec((GRID_W, D_CONV_H), lambda i: (jnp.minimum((i + 1) * hb, n_hblocks - 1), 7)),
        pl.BlockSpec((GRID_W, D_CONV_H), lambda i: (jnp.minimum((i + 1) * hb, n_hblocks - 1), 11)),
        pl.BlockSpec((3, D_CONV), lambda i: (0, 0)),
        pl.BlockSpec((1, D_CONV), lambda i: (0, 0)),
        pl.BlockSpec((D_MODEL, D_MODEL), lambda i: (0, 0)),
        pl.BlockSpec((1, D_MODEL), lambda i: (0, 0)),
        pl.BlockSpec((1, D_MODEL), lambda i: (0, 0)),
        pl.BlockSpec((D_MODEL, 256), lambda i: (0, 0)),
        pl.BlockSpec((N_EXPERTS, 1), lambda i: (0, 0)),
    ]
    out_specs = [
        pl.BlockSpec((tm, D_MODEL), lambda i: (i, 0)),
        pl.BlockSpec((tm, D_MODEL), lambda i: (i, 0)),
        pl.BlockSpec((8, tm), lambda i: (0, i)),
        pl.BlockSpec((tm, 128), lambda i: (i, 0)),
    ]
    out_shape = [
        jax.ShapeDtypeStruct((m_rows, D_MODEL), f32),
        jax.ShapeDtypeStruct((m_rows, D_MODEL), f32),
        jax.ShapeDtypeStruct((8, m_rows), i32),
        jax.ShapeDtypeStruct((m_rows, 128), f32),
    ]
    return pl.pallas_call(
        functools.partial(_mix_kernel, n_lat_blocks=nlb, has_ctx=has_ctx),
        grid=(nblocks,),
        in_specs=in_specs,
        out_specs=out_specs,
        out_shape=out_shape,
        scratch_shapes=[pltpu.VMEM((tm, D_MLSTM), bf16), pltpu.VMEM((tm, D_CONV), bf16)],
        compiler_params=_params(("parallel",)),
        name="mix",
    )(x, mod_l, m_lat, m_ctx, proj, proj, proj, proj, proj, proj, proj,
      conv_w, conv_b, w_out, ln_g, ln_b, w_router, b_router)


def _route(idx8, m_rows):
    tm = TM_EXP
    n_tiles = 2 * m_rows // tm + N_EXPERTS
    ef = idx8[:2, :].reshape(-1)
    oh = (ef[:, None] == jnp.arange(N_EXPERTS, dtype=i32)[None, :]).astype(i32)
    cs = jnp.cumsum(oh, axis=0)
    rank = jnp.take_along_axis(cs, ef[:, None], axis=1)[:, 0] - 1
    counts = cs[-1]
    ntile = (counts + tm - 1) // tm
    tend = jnp.cumsum(ntile)
    tstart = tend - ntile
    pos = (tstart[ef] * tm + rank).astype(i32)
    n_used = tend[-1].astype(i32)
    tidx = jnp.arange(n_tiles, dtype=i32)
    te = jnp.sum((tend[None, :] <= jnp.minimum(tidx, n_used - 1)[:, None]).astype(i32), axis=1)
    te = jnp.minimum(te, N_EXPERTS - 1)
    tok = jnp.tile(jnp.arange(m_rows, dtype=i32), 2)
    src = jnp.zeros((n_tiles * tm,), i32).at[pos].set(tok)
    return pos, src, te, n_used.reshape(1)


def _expert_kernel(te_ref, nu_ref, src_ref, hp_ref, w1_ref, w3_ref, w2_ref, y_ref, xbuf, sem):
    tm = TM_EXP
    i = pl.program_id(0)
    n_used = nu_ref[0]

    def issue(tile, slot):
        base = tile * tm

        def it(j, c):
            t = src_ref[base + j]
            pltpu.make_async_copy(hp_ref.at[pl.ds(t, 1), :], xbuf.at[slot, pl.ds(j, 1), :],
                                  sem.at[slot]).start()
            return c

        lax.fori_loop(0, tm, it, 0, unroll=8)

    @pl.when(i == 0)
    def _():
        issue(0, 0)

    @pl.when(i + 1 < n_used)
    def _():
        issue(i + 1, (i + 1) % 2)

    @pl.when(i < n_used)
    def _():
        slot = i % 2
        pltpu.make_async_copy(hp_ref.at[pl.ds(0, tm), :], xbuf.at[slot], sem.at[slot]).wait()
        xb = xbuf[slot].astype(bf16)
        a1 = jnp.dot(xb, w1_ref[...], preferred_element_type=f32)
        a3 = jnp.dot(xb, w3_ref[...], preferred_element_type=f32)
        act = (a1 * _sigmoid(a1) * a3).astype(bf16)
        y_ref[...] = jnp.dot(act, w2_ref[...], preferred_element_type=f32)

    @pl.when(i >= n_used)
    def _():
        y_ref[...] = jnp.zeros_like(y_ref)


def _experts(te, n_used, src, hp, w1, w3, w2):
    tm = TM_EXP
    n_tiles = src.shape[0] // tm
    return pl.pallas_call(
        _expert_kernel,
        grid_spec=pltpu.PrefetchScalarGridSpec(
            num_scalar_prefetch=3,
            grid=(n_tiles,),
            in_specs=[
                pl.BlockSpec(memory_space=pl.ANY),
                pl.BlockSpec((None, D_MODEL, D_FF), lambda i, te, nu, src: (te[i], 0, 0)),
                pl.BlockSpec((None, D_MODEL, D_FF), lambda i, te, nu, src: (te[i], 0, 0)),
                pl.BlockSpec((None, D_FF, D_MODEL), lambda i, te, nu, src: (te[i], 0, 0)),
            ],
            out_specs=pl.BlockSpec((tm, D_MODEL), lambda i, te, nu, src: (i, 0)),
            scratch_shapes=[pltpu.VMEM((2, tm, D_MODEL), f32), pltpu.SemaphoreType.DMA((2,))],
        ),
        out_shape=jax.ShapeDtypeStruct((src.shape[0], D_MODEL), f32),
        compiler_params=_params(("arbitrary",)),
        name="experts",
    )(te, n_used, src, hp, w1, w3, w2)


def _combine_kernel(pos_ref, x_ref, mod_ref, wt_ref, g_ref, b_ref, y_ref, o_ref, buf, sem, *,
                    m_rows, nblocks):
    tm = TM_CMB
    i = pl.program_id(0)

    def issue(blk, slot):
        base = blk * tm

        def it(j, c):
            p0 = pos_ref[base + j]
            p1 = pos_ref[m_rows + base + j]
            pltpu.make_async_copy(y_ref.at[pl.ds(p0, 1), :], buf.at[slot, 0, pl.ds(j, 1), :],
                                  sem.at[slot]).start()
            pltpu.make_async_copy(y_ref.at[pl.ds(p1, 1), :], buf.at[slot, 1, pl.ds(j, 1), :],
                                  sem.at[slot]).start()
            return c

        lax.fori_loop(0, tm, it, 0, unroll=8)

    @pl.when(i == 0)
    def _():
        issue(0, 0)

    @pl.when(i + 1 < nblocks)
    def _():
        issue(i + 1, (i + 1) % 2)

    slot = i % 2
    for k in range(2):
        pltpu.make_async_copy(y_ref.at[pl.ds(0, tm), :], buf.at[slot, k], sem.at[slot]).wait()

    wt = wt_ref[...]
    moe = wt[:, 0:1] * buf[slot, 0] + wt[:, 1:2] * buf[slot, 1]
    gate = mod_ref[5:6, :]
    o_ref[...] = _ln(ALPHA * x_ref[...] + gate * moe) * g_ref[...] + b_ref[...]


def _combine(pos, x, mod_l, wts, ln_g, ln_b, y):
    tm = TM_CMB
    m_rows = x.shape[0]
    nblocks = m_rows // tm
    return pl.pallas_call(
        functools.partial(_combine_kernel, m_rows=m_rows, nblocks=nblocks),
        grid_spec=pltpu.PrefetchScalarGridSpec(
            num_scalar_prefetch=1,
            grid=(nblocks,),
            in_specs=[
                pl.BlockSpec((tm, D_MODEL), lambda i, p: (i, 0)),
                pl.BlockSpec((None, 6, D_MODEL), lambda i, p: (_mod_row(i, tm), 0, 0)),
                pl.BlockSpec((tm, 128), lambda i, p: (i, 0)),
                pl.BlockSpec((1, D_MODEL), lambda i, p: (0, 0)),
                pl.BlockSpec((1, D_MODEL), lambda i, p: (0, 0)),
                pl.BlockSpec(memory_space=pl.ANY),
            ],
            out_specs=pl.BlockSpec((tm, D_MODEL), lambda i, p: (i, 0)),
            scratch_shapes=[pltpu.VMEM((2, 2, tm, D_MODEL), f32), pltpu.SemaphoreType.DMA((2,))],
        ),
        out_shape=jax.ShapeDtypeStruct((m_rows, D_MODEL), f32),
        compiler_params=_params(("arbitrary",)),
        name="combine",
    )(pos, x, mod_l, wts, ln_g, ln_b, y)


def kernel(x, c, ctx, c_ctx, w_ada, b_ada, w_in, b_igate, b_fgate, mh_norm_g, conv_w, conv_b, w_out,
           ln1_g, ln1_b, w_router, b_router, w1, w3, w2, ln2_g, ln2_b):
    n_state = 2 * HEADS * DQK + D_MLSTM
    cond_raw = jnp.zeros((8, D_MODEL), f32).at[:BATCH].set(c).at[BATCH].set(c_ctx)
    mod = _ada(cond_raw, w_ada, b_ada).reshape(DEPTH, 8, 6, D_MODEL)

    xa = jnp.concatenate([x.reshape(N_LAT, D_MODEL), ctx.reshape(N_CTX, D_MODEL)], axis=0)
    w_router_p = jnp.zeros((D_MODEL, 128), f32).at[:, :N_EXPERTS].set(w_router)
    w_router_hi = w_router_p.astype(bf16)
    w_router_lo = (w_router_p - w_router_hi.astype(f32)).astype(bf16)
    w_router_p = jnp.concatenate([w_router_hi, w_router_lo], axis=1)
    b_router_c = b_router.reshape(N_EXPERTS, 1)
    w_tail = w_in[:, :, n_state + N_GATE:].astype(bf16)
    w_gate = jnp.pad(w_in[:, :, n_state:n_state + N_GATE].astype(bf16), ((0, 0), (0, 0), (0, 128 - N_GATE)))

    for l in range(DEPTH):
        last = l == DEPTH - 1
        proj, gates = _inproj(xa, mod[l], w_in, w_tail, w_gate, l)

        bias = jnp.concatenate([b_igate[l], b_fgate[l]]).astype(f32)
        m_lat, m_ctx = _mlstm3(proj, gates, bias, mh_norm_g[l].reshape(1, D_MLSTM), not last)

        xn, hp, idx8, wts = _mix(xa, mod[l], m_lat, m_ctx, proj, conv_w[l], conv_b[l].reshape(1, D_CONV),
                                 w_out[l].astype(bf16), ln1_g[l].reshape(1, D_MODEL),
                                 ln1_b[l].reshape(1, D_MODEL), w_router_p, b_router_c, not last)

        m_rows = xn.shape[0]
        pos, src, te, n_used = _route(idx8, m_rows)
        y = _experts(te, n_used, src, hp, w1[l].astype(bf16), w3[l].astype(bf16), w2[l].astype(bf16))
        xa = _combine(pos, xn, mod[l], wts, ln2_g[l].reshape(1, D_MODEL), ln2_b[l].reshape(1, D_MODEL), y)

    return xa.reshape(BATCH, SEQ, D_MODEL)
```

```python
import functools

import jax
import jax.numpy as jnp
from jax import lax
from jax.experimental import pallas as pl
from jax.experimental.pallas import tpu as pltpu

f32 = jnp.float32
bf16 = jnp.bfloat16
i32 = jnp.int32
u32 = jnp.uint32

D_MODEL = 2048
BATCH = 4
SEQ = 2048
DEPTH = 4
GRID_W = 64
CTX_LEN = 256
D_MLSTM = 1024
HEADS = 4
DV = 256
DQK = 128
CHUNK = 64
D_CONV = 1024
D_CONV_H = 512
N_EXPERTS = 16
N_GROUPS = 4
EPG = 4
D_FF = 1024
ALPHA = (2 * DEPTH) ** 0.25
LN_EPS = 1e-6
QK_SCALE = DQK ** -0.5

N_LAT = BATCH * SEQ
N_CTX = BATCH * CTX_LEN
N_ALL = N_LAT + N_CTX
D_PROJ = 6144
N_GATE = 16
HALF = D_MODEL // 2

TM_IN = 1024
TN_IN = 512
TM_MIX = 256
TM_EXP = 256
TM_CMB = 256
VMEM_LIMIT = 56 * 1024 * 1024

HIGHEST = lax.Precision.HIGHEST


def _sigmoid(x):
    return 1.0 / (1.0 + jnp.exp(-x))


def _log_sigmoid(x):
    return jnp.minimum(x, 0.0) - jnp.log1p(jnp.exp(-jnp.abs(x)))


def _ln(x):
    mu = jnp.mean(x, axis=-1, keepdims=True)
    xc = x - mu
    var = jnp.mean(xc * xc, axis=-1, keepdims=True)
    return xc * lax.rsqrt(var + LN_EPS)


def _mod_row(i, tm):
    return jnp.minimum((i * tm) // SEQ, BATCH)


def _params(sem, vmem=VMEM_LIMIT):
    return pltpu.CompilerParams(dimension_semantics=sem, vmem_limit_bytes=vmem)


def _ada_kernel(c_ref, w_ref, b_ref, o_ref):
    c = c_ref[...]
    cond = c * _sigmoid(c)
    o_ref[...] = jnp.dot(cond, w_ref[...], preferred_element_type=f32) + b_ref[...]


def _ada(cond_raw, w_ada, b_ada):
    tn = 1024
    n = 6 * D_MODEL
    return pl.pallas_call(
        _ada_kernel,
        grid=(DEPTH, n // tn),
        in_specs=[
            pl.BlockSpec((8, D_MODEL), lambda l, j: (0, 0)),
            pl.BlockSpec((None, D_MODEL, tn), lambda l, j: (l, 0, j)),
            pl.BlockSpec((None, 1, tn), lambda l, j: (l, 0, j)),
        ],
        out_specs=pl.BlockSpec((None, 8, tn), lambda l, j: (l, 0, j)),
        out_shape=jax.ShapeDtypeStruct((DEPTH, 8, n), f32),
        compiler_params=_params(("parallel", "parallel")),
        name="ada",
    )(cond_raw, w_ada, b_ada.reshape(DEPTH, 1, n))


def _inproj_kernel(x_ref, mod_ref, wa_ref, wb_ref, wg_ref, o_ref, g_ref, h_scr, *, slab):
    j = pl.program_id(1)

    @pl.when(j == 0)
    def _():
        shift = mod_ref[0:1, :]
        scale = mod_ref[1:2, :]

        def body(s, c):
            r0 = pl.multiple_of(s * slab, slab)
            h = _ln(x_ref[pl.ds(r0, slab), :]) * (1.0 + scale) + shift
            hb = h.astype(bf16)
            h_scr[pl.ds(r0, slab), :] = hb
            g_ref[pl.ds(r0, slab), :] = jnp.dot(hb, wg_ref[...], preferred_element_type=f32)
            return c

        lax.fori_loop(0, TM_IN // slab, body, 0)

    @pl.when(j < N_STATE_BLOCKS)
    def _():
        o_ref[...] = jnp.dot(h_scr[...], wa_ref[...].astype(bf16),
                             preferred_element_type=f32).astype(o_ref.dtype)

    @pl.when(j >= N_STATE_BLOCKS)
    def _():
        o_ref[...] = jnp.dot(h_scr[...], wb_ref[...], preferred_element_type=f32).astype(o_ref.dtype)


N_STATE = 2 * HEADS * DQK + D_MLSTM
N_STATE_BLOCKS = N_STATE // TN_IN
D_TAIL = D_PROJ - N_STATE
D_IN = N_STATE + N_GATE + D_TAIL


def _wprep_kernel(w_ref, tail_ref, gate_ref):
    w = w_ref[...]
    tail_ref[...] = w[:, N_STATE + N_GATE:].astype(bf16)
    lane = lax.broadcasted_iota(i32, (1, 128), 1)
    gate_ref[...] = jnp.where(lane < N_GATE, w[:, N_STATE:N_STATE + 128], 0.0).astype(bf16)


def _wprep(w_in):
    tr = 256
    return pl.pallas_call(
        _wprep_kernel,
        grid=(DEPTH, D_MODEL // tr),
        in_specs=[pl.BlockSpec((None, tr, D_IN), lambda l, i: (l, i, 0))],
        out_specs=[pl.BlockSpec((None, tr, D_TAIL), lambda l, i: (l, i, 0)),
                   pl.BlockSpec((None, tr, 128), lambda l, i: (l, i, 0))],
        out_shape=[jax.ShapeDtypeStruct((DEPTH, D_MODEL, D_TAIL), bf16),
                   jax.ShapeDtypeStruct((DEPTH, D_MODEL, 128), bf16)],
        compiler_params=_params(("parallel", "parallel")),
        name="wprep",
    )(w_in)


def _inproj(x, mod_l, w_in, w_tail, w_gate, l):
    m = x.shape[0]
    nsb = N_STATE_BLOCKS
    return pl.pallas_call(
        functools.partial(_inproj_kernel, slab=128),
        grid=(m // TM_IN, D_PROJ // TN_IN),
        in_specs=[
            pl.BlockSpec((TM_IN, D_MODEL), lambda i, j: (i, 0)),
            pl.BlockSpec((None, 6, D_MODEL), lambda i, j: (_mod_row(i, TM_IN), 0, 0)),
            pl.BlockSpec((None, D_MODEL, TN_IN), lambda i, j: (l, 0, jnp.minimum(j, nsb - 1))),
            pl.BlockSpec((None, D_MODEL, TN_IN), lambda i, j: (l, 0, jnp.maximum(j - nsb, 0))),
            pl.BlockSpec((None, D_MODEL, 128), lambda i, j: (l, 0, 0)),
        ],
        out_specs=[
            pl.BlockSpec((TM_IN, TN_IN), lambda i, j: (i, j)),
            pl.BlockSpec((TM_IN, 128), lambda i, j: (i, 0)),
        ],
        out_shape=[
            jax.ShapeDtypeStruct((m, D_PROJ), bf16),
            jax.ShapeDtypeStruct((m, 128), f32),
        ],
        scratch_shapes=[pltpu.VMEM((TM_IN, D_MODEL), bf16)],
        compiler_params=_params(("parallel", "arbitrary")),
        name="inproj",
    )(x, mod_l, w_in, w_tail, w_gate)


def _mlstm_chunk(q, k, v, fc_raw, ic_raw, fr_raw, ir_raw, b_i, b_f, ct_ref, n_ref, d, m, fwd):
    L = CHUNK
    lf_c = _log_sigmoid(fc_raw + b_f)
    i_c = ic_raw + b_i
    lf_r = _log_sigmoid(fr_raw + b_f)
    i_r = ir_raw + b_i
    rr = lax.broadcasted_iota(i32, (L, L), 0)
    cc = lax.broadcasted_iota(i32, (L, L), 1)
    lo = rr >= cc
    up = rr <= cc
    mask = lo if fwd else up
    a_mat = mask.astype(f32)
    a_t = (up if fwd else lo).astype(f32)
    bcol = jnp.dot(a_mat, jnp.broadcast_to(lf_c, (L, L)), precision=HIGHEST,
                   preferred_element_type=f32)
    brow = jnp.dot(jnp.broadcast_to(lf_r, (L, L)), a_t, precision=HIGHEST,
                   preferred_element_type=f32)
    dm = jnp.where(mask, bcol - brow + i_r, -jnp.inf)
    mloc = jnp.max(dm, axis=1, keepdims=True)
    wloc = jnp.exp(dm - mloc)
    b1 = bcol[:, 0:1]
    m_t = jnp.maximum(b1 + m, mloc)
    inter = jnp.exp(b1 + m - m_t)
    a = jnp.exp(mloc - m_t)
    qk = lax.dot_general(q, k, (((1,), (1,)), ((), ())), preferred_element_type=f32) * QK_SCALE
    sloc = qk * wloc
    ct = ct_ref[d]
    n = n_ref[d]
    num = inter * jnp.dot(q, ct.astype(bf16), preferred_element_type=f32) \
        + a * jnp.dot(sloc.astype(bf16), v, preferred_element_type=f32)
    den = inter * jnp.sum(q.astype(f32) * n, axis=1, keepdims=True) \
        + a * jnp.sum(sloc, axis=1, keepdims=True)
    hout = num / jnp.maximum(jnp.abs(den), jnp.exp(-m_t))
    last = L - 1 if fwd else 0
    b_last = b1[last:last + 1, :]
    wl = jnp.exp(b_last - b1 + i_c - mloc[last:last + 1, :])
    kw = k.astype(f32) * (wl * QK_SCALE)
    u = lax.dot_general(kw.astype(bf16), v, (((0,), (0,)), ((), ())), preferred_element_type=f32)
    decay = inter[last:last + 1, :]
    a_l = a[last:last + 1, :]
    ct_ref[d] = decay * ct + a_l * u
    n_ref[d] = decay * n + a_l * jnp.sum(kw, axis=0, keepdims=True)
    return hout, m_t[last:last + 1, :]


def _mlstm_kernel(bias_ref, ql_ref, kl_ref, vl_ref, ol_ref, qc_ref, kc_ref, vc_ref, oc_ref,
                  gcl_ref, grl_ref, gcc_ref, grc_ref, gain_ref, *rest, has_ctx_out):
    if has_ctx_out:
        ml_ref, mc_ref, ct_ref, n_ref, hfl, hbl, hfc, hbc = rest
    else:
        ml_ref, ct_ref, n_ref, hfl, hbl = rest
        mc_ref = hfc = hbc = None
    b = pl.program_id(0)
    h = pl.program_id(1)
    L = CHUNK
    bi = (bias_ref[h], bias_ref[HEADS + h])
    bf = (bias_ref[2 * HEADS + h], bias_ref[3 * HEADS + h])
    ct_ref[...] = jnp.zeros_like(ct_ref)
    n_ref[...] = jnp.zeros_like(n_ref)

    def run(q_ref, k_ref, v_ref, gc_ref, gr_ref, c_base, nchunks, hf, hb, carry):
        def body(j, carry):
            m_f, m_b = carry
            outs = []
            for d, fwd in ((0, True), (1, False)):
                c = j if fwd else nchunks - 1 - j
                r0 = pl.multiple_of(c * L, L)
                q = q_ref[pl.ds(r0, L), :]
                k = k_ref[pl.ds(r0, L), :]
                v = v_ref[pl.ds(r0, L), :]
                gcol = gc_ref[c_base + c]
                ic = gcol[:, d:d + 1]
                fc = gcol[:, 2 + d:3 + d]
                ir = gr_ref[d, pl.ds(c_base + c, 1), :]
                fr = gr_ref[2 + d, pl.ds(c_base + c, 1), :]
                hout, m_new = _mlstm_chunk(q, k, v, fc, ic, fr, ir, bi[d], bf[d], ct_ref, n_ref, d,
                                           m_f if fwd else m_b, fwd)
                if hf is not None:
                    (hf if fwd else hb)[pl.ds(r0, L), :] = hout
                outs.append(m_new)
            return tuple(outs)

        return lax.fori_loop(0, nchunks, body, carry)

    zero = jnp.zeros((1, 1), f32)
    carry = run(qc_ref, kc_ref, vc_ref, gcc_ref, grc_ref, b * (CTX_LEN // L), CTX_LEN // L, hfc, hbc,
                (zero, zero))
    run(ql_ref, kl_ref, vl_ref, gcl_ref, grl_ref, 0, SEQ // L, hfl, hbl, carry)

    gain = gain_ref[...]

    def finish(hf, hb, o_ref, out_ref, rows):
        slab = 256

        def ep(s, c):
            r0 = pl.multiple_of(s * slab, slab)
            hs = hf[pl.ds(r0, slab), :] + hb[pl.ds(r0, slab), :]
            r = lax.rsqrt(jnp.mean(hs * hs, axis=-1, keepdims=True) + LN_EPS)
            o = o_ref[pl.ds(r0, slab), :].astype(f32)
            out_ref[pl.ds(r0, slab), :] = (hs * r * gain * _sigmoid(o)).astype(out_ref.dtype)
            return c

        lax.fori_loop(0, rows // slab, ep, 0)

    finish(hfl, hbl, ol_ref, ml_ref, SEQ)
    if has_ctx_out:
        finish(hfc, hbc, oc_ref, mc_ref, CTX_LEN)


def _mlstm(proj, gates, bias, gain, has_ctx_out):
    L = CHUNK
    g = gates[:, :N_GATE]
    g4 = jnp.stack([g[:, 0:4], g[:, 4:8], g[:, 8:12], g[:, 12:16]], axis=-1)
    gh = jnp.transpose(g4, (1, 0, 2))
    gcol_l = gh[:, :N_LAT].reshape(HEADS, N_LAT // L, L, 4)
    gcol_c = gh[:, N_LAT:].reshape(HEADS, N_CTX // L, L, 4)
    gr = jnp.transpose(g4, (1, 2, 0))
    grow_l = gr[:, :, :N_LAT].reshape(HEADS, 4, N_LAT // L, L)
    grow_c = gr[:, :, N_LAT:].reshape(HEADS, 4, N_CTX // L, L)

    lat_rb = lambda b, h: b
    ctx_rb = lambda b, h: N_LAT // CTX_LEN + b
    in_specs = [
        pl.BlockSpec(memory_space=pltpu.SMEM),
        pl.BlockSpec((SEQ, DQK), lambda b, h: (lat_rb(b, h), h)),
        pl.BlockSpec((SEQ, DQK), lambda b, h: (lat_rb(b, h), HEADS + h)),
        pl.BlockSpec((SEQ, DV), lambda b, h: (lat_rb(b, h), HEADS + h)),
        pl.BlockSpec((SEQ, DV), lambda b, h: (lat_rb(b, h), 2 * HEADS + h)),
        pl.BlockSpec((CTX_LEN, DQK), lambda b, h: (ctx_rb(b, h), h)),
        pl.BlockSpec((CTX_LEN, DQK), lambda b, h: (ctx_rb(b, h), HEADS + h)),
        pl.BlockSpec((CTX_LEN, DV), lambda b, h: (ctx_rb(b, h), HEADS + h)),
        pl.BlockSpec((CTX_LEN, DV), lambda b, h: (ctx_rb(b, h), 2 * HEADS + h)),
        pl.BlockSpec((None, SEQ // L, L, 4), lambda b, h: (h, b, 0, 0)),
        pl.BlockSpec((None, 4, SEQ // L, L), lambda b, h: (h, 0, b, 0)),
        pl.BlockSpec((None, N_CTX // L, L, 4), lambda b, h: (h, 0, 0, 0)),
        pl.BlockSpec((None, 4, N_CTX // L, L), lambda b, h: (h, 0, 0, 0)),
        pl.BlockSpec((1, DV), lambda b, h: (0, h)),
    ]
    out_specs = [pl.BlockSpec((SEQ, DV), lambda b, h: (b, h))]
    out_shape = [jax.ShapeDtypeStruct((N_LAT, D_MLSTM), bf16)]
    scratch = [pltpu.VMEM((2, DQK, DV), f32), pltpu.VMEM((2, 1, DQK), f32),
               pltpu.VMEM((SEQ, DV), f32), pltpu.VMEM((SEQ, DV), f32)]
    if has_ctx_out:
        out_specs.append(pl.BlockSpec((CTX_LEN, DV), lambda b, h: (b, h)))
        out_shape.append(jax.ShapeDtypeStruct((N_CTX, D_MLSTM), bf16))
        scratch += [pltpu.VMEM((CTX_LEN, DV), f32), pltpu.VMEM((CTX_LEN, DV), f32)]
    res = pl.pallas_call(
        functools.partial(_mlstm_kernel, has_ctx_out=has_ctx_out),
        grid=(BATCH, HEADS),
        in_specs=in_specs,
        out_specs=out_specs,
        out_shape=out_shape,
        scratch_shapes=scratch,
        compiler_params=_params(("parallel", "parallel")),
        name="mlstm",
    )(bias, proj, proj, proj, proj, proj, proj, proj, proj, gcol_l, grow_l, gcol_c, grow_c, gain)
    return res if has_ctx_out else (res[0], None)


GROUP = 4
N_CHUNK_CTX = CTX_LEN // CHUNK
N_CHUNK_LAT = SEQ // CHUNK
N_CHUNK = N_CHUNK_CTX + N_CHUNK_LAT
N_GROUP_LAT = N_CHUNK_LAT // GROUP
LAT_ROW0 = 8
D_AUG = DV + 128


def _mlstm3_kernel(bias_ref, ql_ref, kl_ref, vl_ref, ol_ref, qc_ref, kc_ref, vc_ref, oc_ref,
                   gcl_ref, grl_ref, gcc_ref, grc_ref, gain_ref, *rest, has_ctx_out):
    if has_ctx_out:
        ml_ref, mc_ref, s_ref, st_ref, msc, rows_s, cols_s = rest
    else:
        ml_ref, s_ref, st_ref, msc, rows_s, cols_s = rest
        mc_ref = None
    L = CHUNK
    b = pl.program_id(0)
    h = pl.program_id(1)
    bi = (bias_ref[h], bias_ref[HEADS + h])
    bf = (bias_ref[2 * HEADS + h], bias_ref[3 * HEADS + h])

    rr = lax.broadcasted_iota(i32, (L, L), 0)
    cc = lax.broadcasted_iota(i32, (L, L), 1)
    lo_mask = rr >= cc
    up_mask = rr <= cc
    masks = (lo_mask, up_mask)

    def row_forms(gr_ref, sl, dst0, n):
        for d in range(2):
            tri = (up_mask if d == 0 else lo_mask).astype(f32)
            i_r = gr_ref[d, sl, :] + bi[d]
            lf_r = _log_sigmoid(gr_ref[2 + d, sl, :] + bf[d])
            b_r = jnp.dot(lf_r, tri, precision=HIGHEST, preferred_element_type=f32)
            rows_s[d, dst0:dst0 + n, :] = i_r - b_r

    row_forms(grc_ref, pl.ds(b * N_CHUNK_CTX, N_CHUNK_CTX), 0, N_CHUNK_CTX)
    row_forms(grl_ref, slice(None), LAT_ROW0, N_CHUNK_LAT)

    kind = lax.broadcasted_iota(i32, (1, 4 * GROUP), 1) % 4
    bias_v = jnp.where(kind == 0, bi[0], jnp.where(kind == 1, bi[1], jnp.where(kind == 2, bf[0], bf[1])))
    tpos = lax.broadcasted_iota(i32, (L, 4 * GROUP), 0)

    def col_forms(x):
        y = x + bias_v
        y = jnp.where(kind >= 2, _log_sigmoid(y), y)
        pre = y
        suf = y
        s = 1
        while s < L:
            pre = pre + jnp.where(tpos >= s, pltpu.roll(pre, s, 0), 0.0)
            suf = suf + jnp.where(tpos < L - s, pltpu.roll(suf, L - s, 0), 0.0)
            s *= 2
        return jnp.where(kind == 2, pre, jnp.where(kind == 3, suf, y))

    cols_s[0] = col_forms(gcc_ref[...])

    def col_body(g, c):
        cols_s[g + 1] = col_forms(gcl_ref[g])
        return c

    lax.fori_loop(0, N_GROUP_LAT, col_body, 0)

    ones_col = (lax.broadcasted_iota(i32, (L, 128), 1) == 0).astype(bf16)

    s_ref[...] = jnp.zeros_like(s_ref)

    def state_step(d, c, row, col, j, k, v, m):
        i_c = col[:, 4 * j + d:4 * j + d + 1]
        b1 = col[:, 4 * j + 2 + d:4 * j + 3 + d]
        last = L - 1 if d == 0 else 0
        b_last = b1[last:last + 1, :]
        ct = i_c - b1
        mx = jnp.max(ct, axis=0, keepdims=True)
        wl = jnp.exp(ct - mx)
        mloc_last = b_last + mx
        m_new = jnp.maximum(b_last + m, mloc_last)
        decay = jnp.exp(b_last + m - m_new)
        a_l = jnp.exp(mloc_last - m_new)
        s_old = s_ref[d]
        st_ref[d, c] = s_old.astype(bf16)
        msc[d, pl.ds(row, 1), :] = jnp.broadcast_to(m, (1, 128))
        kw = (k.astype(f32) * (wl * QK_SCALE)).astype(bf16)
        vaug = jnp.concatenate([v, ones_col], axis=1)
        u = lax.dot_general(kw, vaug, (((0,), (0,)), ((), ())), preferred_element_type=f32)
        s_ref[d] = decay * s_old + a_l * u
        return m_new

    zero = jnp.zeros((1, 1), f32)
    m_f = m_b = zero
    col0 = cols_s[0]
    for step in range(GROUP):
        jf, jb = step, GROUP - 1 - step
        m_f = state_step(0, jf, jf, col0, jf, kc_ref[jf * L:(jf + 1) * L, :], vc_ref[jf * L:(jf + 1) * L, :], m_f)
        m_b = state_step(1, jb, jb, col0, jb, kc_ref[jb * L:(jb + 1) * L, :], vc_ref[jb * L:(jb + 1) * L, :], m_b)

    def state_body(it, carry):
        m_f, m_b = carry
        gf = it
        gb = N_GROUP_LAT + 1 - it
        colf = cols_s[gf]
        colb = cols_s[gb]
        for step in range(GROUP):
            jf, jb = step, GROUP - 1 - step
            clf = (gf - 1) * GROUP + jf
            clb = (gb - 1) * GROUP + jb
            rf = pl.multiple_of(clf * L, L)
            rb = pl.multiple_of(clb * L, L)
            m_f = state_step(0, N_CHUNK_CTX + clf, LAT_ROW0 + clf, colf, jf,
                             kl_ref[pl.ds(rf, L), :], vl_ref[pl.ds(rf, L), :], m_f)
            m_b = state_step(1, N_CHUNK_CTX + clb, LAT_ROW0 + clb, colb, jb,
                             kl_ref[pl.ds(rb, L), :], vl_ref[pl.ds(rb, L), :], m_b)
        return m_f, m_b

    lax.fori_loop(1, N_GROUP_LAT + 1, state_body, (m_f, m_b))

    gain = gain_ref[...]

    def out_chunk(q, k, v, o, col, j, c, row, out_ref, r0):
        qk = lax.dot_general(q, k, (((1,), (1,)), ((), ())), preferred_element_type=f32) * QK_SCALE
        vaug = jnp.concatenate([v, ones_col], axis=1)
        sl = []
        per = []
        for d in range(2):
            b1 = col[:, 4 * j + 2 + d:4 * j + 3 + d]
            dm = jnp.where(masks[d], b1 + rows_s[d, pl.ds(row, 1), :], -jnp.inf)
            mloc = jnp.max(dm, axis=1, keepdims=True)
            wloc = jnp.exp(dm - mloc)
            m_prev = msc[d, pl.ds(row, 1), :][:, 0:1]
            m_t = jnp.maximum(b1 + m_prev, mloc)
            inter = jnp.exp(b1 + m_prev - m_t)
            a = jnp.exp(mloc - m_t)
            sl.append((qk * wloc).astype(bf16))
            per.append((m_t, inter, a))
        x = jnp.dot(jnp.concatenate(sl, axis=0), vaug, preferred_element_type=f32)
        hs = None
        for d in range(2):
            y = jnp.dot(q, st_ref[d, c], preferred_element_type=f32)
            m_t, inter, a = per[d]
            xd = x[d * L:(d + 1) * L, :]
            den = inter * y[:, DV:DV + 1] + a * xd[:, DV:DV + 1]
            rinv = 1.0 / jnp.maximum(jnp.abs(den), jnp.exp(-m_t))
            hd = (inter * rinv) * y[:, :DV] + (a * rinv) * xd[:, :DV]
            hs = hd if hs is None else hs + hd
        r = lax.rsqrt(jnp.mean(hs * hs, axis=-1, keepdims=True) + LN_EPS)
        out_ref[pl.ds(r0, L), :] = (hs * r * gain * _sigmoid(o.astype(f32))).astype(out_ref.dtype)

    if has_ctx_out:
        for j in range(GROUP):
            sl_ = slice(j * L, (j + 1) * L)
            out_chunk(qc_ref[sl_, :], kc_ref[sl_, :], vc_ref[sl_, :], oc_ref[sl_, :], col0, j, j, j,
                      mc_ref, j * L)

    def out_body(g, carry):
        col = cols_s[g + 1]
        for j in range(GROUP):
            cl = g * GROUP + j
            r0 = pl.multiple_of(cl * L, L)
            out_chunk(ql_ref[pl.ds(r0, L), :], kl_ref[pl.ds(r0, L), :], vl_ref[pl.ds(r0, L), :],
                      ol_ref[pl.ds(r0, L), :], col, j, N_CHUNK_CTX + cl, LAT_ROW0 + cl, ml_ref, r0)
        return carry

    lax.fori_loop(0, N_GROUP_LAT, out_body, 0)


def _mlstm3(proj, gates, bias, gain, has_ctx_out):
    L = CHUNK
    g = gates[:, :N_GATE]
    g4 = jnp.stack([g[:, 0:4], g[:, 4:8], g[:, 8:12], g[:, 12:16]], axis=-1)
    gh = jnp.transpose(g4, (1, 0, 2))
    gcol_l = gh[:, :N_LAT].reshape(HEADS, BATCH * N_GROUP_LAT, GROUP, L, 4)
    gcol_l = jnp.transpose(gcol_l, (0, 1, 3, 2, 4)).reshape(HEADS, BATCH * N_GROUP_LAT, L, 4 * GROUP)
    gcol_c = gh[:, N_LAT:].reshape(HEADS, BATCH, GROUP, L, 4)
    gcol_c = jnp.transpose(gcol_c, (0, 1, 3, 2, 4)).reshape(HEADS, BATCH, L, 4 * GROUP)
    gr = jnp.transpose(g4, (1, 2, 0))
    grow_l = gr[:, :, :N_LAT].reshape(HEADS, 4, N_LAT // L, L)
    grow_c = gr[:, :, N_LAT:].reshape(HEADS, 4, N_CTX // L, L)

    ctx_rb = lambda b, h: N_LAT // CTX_LEN + b
    in_specs = [
        pl.BlockSpec(memory_space=pltpu.SMEM),
        pl.BlockSpec((SEQ, DQK), lambda b, h: (b, h)),
        pl.BlockSpec((SEQ, DQK), lambda b, h: (b, HEADS + h)),
        pl.BlockSpec((SEQ, DV), lambda b, h: (b, HEADS + h)),
        pl.BlockSpec((SEQ, DV), lambda b, h: (b, 2 * HEADS + h)),
        pl.BlockSpec((CTX_LEN, DQK), lambda b, h: (ctx_rb(b, h), h)),
        pl.BlockSpec((CTX_LEN, DQK), lambda b, h: (ctx_rb(b, h), HEADS + h)),
        pl.BlockSpec((CTX_LEN, DV), lambda b, h: (ctx_rb(b, h), HEADS + h)),
        pl.BlockSpec((CTX_LEN, DV), lambda b, h: (ctx_rb(b, h), 2 * HEADS + h)),
        pl.BlockSpec((None, N_GROUP_LAT, L, 4 * GROUP), lambda b, h: (h, b, 0, 0)),
        pl.BlockSpec((None, 4, N_CHUNK_LAT, L), lambda b, h: (h, 0, b, 0)),
        pl.BlockSpec((None, None, L, 4 * GROUP), lambda b, h: (h, b, 0, 0)),
        pl.BlockSpec((None, 4, N_CTX // L, L), lambda b, h: (h, 0, 0, 0)),
        pl.BlockSpec((1, DV), lambda b, h: (0, h)),
    ]
    out_specs = [pl.BlockSpec((SEQ, DV), lambda b, h: (b, h))]
    out_shape = [jax.ShapeDtypeStruct((N_LAT, D_MLSTM), bf16)]
    if has_ctx_out:
        out_specs.append(pl.BlockSpec((CTX_LEN, DV), lambda b, h: (b, h)))
        out_shape.append(jax.ShapeDtypeStruct((N_CTX, D_MLSTM), bf16))
    scratch = [
        pltpu.VMEM((2, DQK, D_AUG), f32),
        pltpu.VMEM((2, N_CHUNK, DQK, D_AUG), bf16),
        pltpu.VMEM((2, LAT_ROW0 + N_CHUNK_LAT, 128), f32),
        pltpu.VMEM((2, LAT_ROW0 + N_CHUNK_LAT, L), f32),
        pltpu.VMEM((N_GROUP_LAT + 1, L, 4 * GROUP), f32),
    ]
    res = pl.pallas_call(
        functools.partial(_mlstm3_kernel, has_ctx_out=has_ctx_out),
        grid=(BATCH, HEADS),
        in_specs=in_specs,
        out_specs=out_specs,
        out_shape=out_shape,
        scratch_shapes=scratch,
        compiler_params=_params(("parallel", "parallel")),
        name="mlstm",
    )(bias, proj, proj, proj, proj, proj, proj, proj, proj, gcol_l, grow_l, gcol_c, grow_c, gain)
    return res if has_ctx_out else (res[0], None)


def _top2_rows(vals):
    best = vals[0]
    bi = jnp.zeros(best.shape, i32)
    for j in range(1, len(vals)):
        take = vals[j] > best
        best = jnp.where(take, vals[j], best)
        bi = jnp.where(take, j, bi)
    sec = None
    si = None
    for j in range(len(vals)):
        cand = jnp.where(bi == j, -jnp.inf, vals[j])
        if sec is None:
            sec, si = cand, jnp.zeros(best.shape, i32)
        else:
            take = cand > sec
            sec = jnp.where(take, cand, sec)
            si = jnp.where(take, j, si)
    return bi, si


def _mix_kernel(x_ref, mod_ref, ml_ref, mc_ref, u_ref, bg_ref, cg_ref, ut_ref, ct_ref, ub_ref, cb_ref,
                cw_ref, cbias_ref, wo_ref, g1_ref, b1_ref, wr_ref, br_ref,
                xo_ref, hp_ref, idx_ref, wt_ref, m_scr, y_scr, *, n_lat_blocks, has_ctx):
    tm = TM_MIX
    i = pl.program_id(0)
    gate = mod_ref[2:3, :]
    shift2 = mod_ref[3:4, :]
    scale2 = mod_ref[4:5, :]
    cw = cw_ref[...]
    cbias = cbias_ref[...]
    row = lax.broadcasted_iota(i32, (tm, 1), 0)

    def shifted(z, first, last):
        prev = jnp.where(first, 0.0, pltpu.roll(z, 1, 0))
        nxt = jnp.where(last, 0.0, pltpu.roll(z, tm - 1, 0))
        return prev, nxt

    def lat_branch():
        z = cg_ref[...].astype(f32) * u_ref[...].astype(f32)
        col = row % GRID_W
        zh = z[:, :D_CONV_H]
        prev, nxt = shifted(zh, col == 0, col == GRID_W - 1)
        yh = cw[0:1, :D_CONV_H] * prev + cw[1:2, :D_CONV_H] * zh + cw[2:3, :D_CONV_H] * nxt
        bpb = SEQ // tm
        top_ok = (i % bpb != 0).astype(f32)
        bot_ok = (i % bpb != bpb - 1).astype(f32)
        zt = ct_ref[...].astype(f32) * ut_ref[...].astype(f32) * top_ok
        zb = cb_ref[...].astype(f32) * ub_ref[...].astype(f32) * bot_ok
        zv = z[:, D_CONV_H:]
        zext = jnp.concatenate([zt, zv, zb], axis=0)
        yv = cw[0:1, D_CONV_H:] * zext[0:tm] + cw[1:2, D_CONV_H:] * zv \
            + cw[2:3, D_CONV_H:] * zext[2 * GRID_W:2 * GRID_W + tm]
        y = jnp.concatenate([yh, yv], axis=1) + cbias
        y_scr[...] = (bg_ref[...].astype(f32) * y).astype(bf16)
        m_scr[...] = ml_ref[...]

    def ctx_branch():
        z = cg_ref[...].astype(f32) * u_ref[...].astype(f32)
        pos = row % CTX_LEN
        prev, nxt = shifted(z, pos == 0, pos == CTX_LEN - 1)
        y = cw[0:1, :] * prev + cw[1:2, :] * z + cw[2:3, :] * nxt + cbias
        y_scr[...] = (bg_ref[...].astype(f32) * y).astype(bf16)
        m_scr[...] = mc_ref[...]

    if has_ctx:
        pl.when(i < n_lat_blocks)(lat_branch)
        pl.when(i >= n_lat_blocks)(ctx_branch)
    else:
        lat_branch()

    out = jnp.dot(m_scr[...], wo_ref[0:D_MLSTM, :], preferred_element_type=f32) \
        + jnp.dot(y_scr[...], wo_ref[D_MLSTM:, :], preferred_element_type=f32)
    xn = _ln(ALPHA * x_ref[...] + gate * out) * g1_ref[...] + b1_ref[...]
    xo_ref[...] = xn
    h2 = _ln(xn) * (1.0 + scale2) + shift2

    hp_ref[...] = h2

    h_hi = h2.astype(bf16)
    hs = jnp.concatenate([h_hi, (h2 - h_hi.astype(f32)).astype(bf16)], axis=0)
    pr = jnp.dot(hs, wr_ref[...], preferred_element_type=f32)
    logits = pr[:tm, :128] + (pr[:tm, 128:] + pr[tm:, :128])
    lt = logits.T
    s = _sigmoid(lt[0:N_EXPERTS, :])
    sb = s + br_ref[...]
    sb_rows = [sb[e:e + 1, :] for e in range(N_EXPERTS)]
    s_rows = [s[e:e + 1, :] for e in range(N_EXPERTS)]
    gscores = []
    for g in range(N_GROUPS):
        a_, b_, c_, d_ = sb_rows[EPG * g:EPG * g + EPG]
        hi1, lo1 = jnp.maximum(a_, b_), jnp.minimum(a_, b_)
        hi2, lo2 = jnp.maximum(c_, d_), jnp.minimum(c_, d_)
        top = jnp.maximum(hi1, hi2)
        second = jnp.maximum(jnp.minimum(hi1, hi2), jnp.maximum(lo1, lo2))
        gscores.append(top + second)
    gbest = gscores[0]
    gsel = jnp.zeros(gbest.shape, i32)
    for g in range(1, N_GROUPS):
        take = gscores[g] > gbest
        gbest = jnp.where(take, gscores[g], gbest)
        gsel = jnp.where(take, g, gsel)

    def pick_group(rows, j):
        v = rows[j]
        for g in range(1, N_GROUPS):
            v = jnp.where(gsel == g, rows[EPG * g + j], v)
        return v

    cand_b = [pick_group(sb_rows, j) for j in range(EPG)]
    cand_s = [pick_group(s_rows, j) for j in range(EPG)]
    i1, i2 = _top2_rows(cand_b)

    def pick_idx(rows, idx):
        v = rows[0]
        for j in range(1, EPG):
            v = jnp.where(idx == j, rows[j], v)
        return v

    s1 = pick_idx(cand_s, i1)
    s2 = pick_idx(cand_s, i2)
    tot = s1 + s2
    w1 = s1 / tot
    w2 = s2 / tot
    e1 = gsel * EPG + i1
    e2 = gsel * EPG + i2
    r8 = lax.broadcasted_iota(i32, (8, tm), 0)
    idx_ref[...] = jnp.where(r8 == 0, e1, jnp.where(r8 == 1, e2, 0))
    r128 = lax.broadcasted_iota(i32, (128, tm), 0)
    wmat = jnp.where(r128 == 0, w1, jnp.where(r128 == 1, w2, 0.0))
    wt_ref[...] = wmat.T


def _mix(x, mod_l, m_lat, m_ctx, proj, conv_w, conv_b, w_out, ln_g, ln_b, w_router, b_router, has_ctx, l):
    tm = TM_MIX
    m_rows = N_ALL if has_ctx else N_LAT
    nlb = N_LAT // tm
    nblocks = m_rows // tm
    hb = tm // GRID_W
    n_hblocks = N_ALL // GRID_W
    if m_ctx is None:
        m_ctx = m_lat
    ncb = m_ctx.shape[0] // tm
    in_specs = [
        pl.BlockSpec((tm, D_MODEL), lambda i: (i, 0)),
        pl.BlockSpec((None, 6, D_MODEL), lambda i: (_mod_row(i, tm), 0, 0)),
        pl.BlockSpec((tm, D_MLSTM), lambda i: (jnp.minimum(i, nlb - 1), 0)),
        pl.BlockSpec((tm, D_MLSTM), lambda i: (jnp.clip(i - nlb, 0, ncb - 1), 0)),
        pl.BlockSpec((tm, D_CONV), lambda i: (i, 3)),
        pl.BlockSpec((tm, D_CONV), lambda i: (i, 4)),
        pl.BlockSpec((tm, D_CONV), lambda i: (i, 5)),
        pl.BlockSpec((GRID_W, D_CONV_H), lambda i: (jnp.maximum(i * hb - 1, 0), 7)),
        pl.BlockSpec((GRID_W, D_CONV_H), lambda i: (jnp.maximum(i * hb - 1, 0), 11)),
        pl.BlockSpec((GRID_W, D_CONV_H), lambda i: (jnp.minimum((i + 1) * hb, n_hblocks - 1), 7)),
        pl.BlockSpec((GRID_W, D_CONV_H), lambda i: (jnp.minimum((i + 1) * hb, n_hblocks - 1), 11)),
        pl.BlockSpec((3, D_CONV), lambda i: (0, 0)),
        pl.BlockSpec((1, D_CONV), lambda i: (0, 0)),
        pl.BlockSpec((None, D_MODEL, D_MODEL), lambda i: (l, 0, 0)),
        pl.BlockSpec((1, D_MODEL), lambda i: (0, 0)),
        pl.BlockSpec((1, D_MODEL), lambda i: (0, 0)),
        pl.BlockSpec((D_MODEL, 256), lambda i: (0, 0)),
        pl.BlockSpec((N_EXPERTS, 1), lambda i: (0, 0)),
    ]
    out_specs = [
        pl.BlockSpec((tm, D_MODEL), lambda i: (i, 0)),
        pl.BlockSpec((tm, D_MODEL), lambda i: (i, 0)),
        pl.BlockSpec((8, tm), lambda i: (0, i)),
        pl.BlockSpec((tm, 128), lambda i: (i, 0)),
    ]
    out_shape = [
        jax.ShapeDtypeStruct((m_rows, D_MODEL), f32),
        jax.ShapeDtypeStruct((m_rows, D_MODEL), f32),
        jax.ShapeDtypeStruct((8, m_rows), i32),
        jax.ShapeDtypeStruct((m_rows, 128), f32),
    ]
    return pl.pallas_call(
        functools.partial(_mix_kernel, n_lat_blocks=nlb, has_ctx=has_ctx),
        grid=(nblocks,),
        in_specs=in_specs,
        out_specs=out_specs,
        out_shape=out_shape,
        scratch_shapes=[pltpu.VMEM((tm, D_MLSTM), bf16), pltpu.VMEM((tm, D_CONV), bf16)],
        compiler_params=_params(("parallel",)),
        name="mix",
    )(x, mod_l, m_lat, m_ctx, proj, proj, proj, proj, proj, proj, proj,
      conv_w, conv_b, w_out, ln_g, ln_b, w_router, b_router)


def _route_kernel(ef_ref, pos_ref, meta_ref, *, rows):
    R = rows
    ef = ef_ref[...]
    li = lax.broadcasted_iota(i32, (128, 128), 0)
    lj = lax.broadcasted_iota(i32, (128, 128), 1)
    strict_up = (li < lj).astype(bf16)
    ri = lax.broadcasted_iota(i32, (R, R), 0)
    rj = lax.broadcasted_iota(i32, (R, R), 1)
    strict_lo = (rj < ri).astype(bf16)
    lane16 = lax.broadcasted_iota(i32, (R, N_EXPERTS), 1)

    ohs = [(ef == e) for e in range(N_EXPERTS)]
    within = [jnp.dot(oh.astype(bf16), strict_up, preferred_element_type=f32) for oh in ohs]
    rt = jnp.zeros((R, N_EXPERTS), f32)
    for e in range(N_EXPERTS):
        rt = jnp.where(lane16 == e, jnp.sum(ohs[e].astype(f32), axis=1, keepdims=True), rt)
    rp = jnp.dot(strict_lo, rt.astype(bf16), preferred_element_type=f32)
    counts = rp[R - 1:R, :] + rt[R - 1:R, :]
    ntile = jnp.floor((counts + (TM_EXP - 1)) * (1.0 / TM_EXP))
    ei = lax.broadcasted_iota(i32, (N_EXPERTS, N_EXPERTS), 0)
    ej = lax.broadcasted_iota(i32, (N_EXPERTS, N_EXPERTS), 1)
    tend = jnp.dot(ntile.astype(bf16), (ei <= ej).astype(bf16), preferred_element_type=f32)
    off = rp + (tend - ntile) * float(TM_EXP)
    pos = jnp.zeros((R, 128), f32)
    for e in range(N_EXPERTS):
        pos = jnp.where(ohs[e], within[e] + off[:, e:e + 1], pos)
    pos_ref[...] = pos.astype(i32)

    n_used = tend[:, N_EXPERTS - 1:N_EXPERTS]
    tile = jnp.minimum(lax.broadcasted_iota(i32, (1, 128), 1).astype(f32), n_used - 1.0)
    te = jnp.zeros((1, 128), f32)
    for e in range(N_EXPERTS):
        te = te + (tend[:, e:e + 1] <= tile).astype(f32)
    te = jnp.minimum(te, float(N_EXPERTS - 1))
    r8 = lax.broadcasted_iota(i32, (8, 128), 0)
    meta_ref[...] = jnp.where(r8 == 0, te, jnp.where(r8 == 1, n_used, 0.0)).astype(i32)


def _route(idx8, m_rows):
    rows = 2 * m_rows // 128
    n_tiles = 2 * m_rows // TM_EXP + N_EXPERTS
    assert n_tiles <= 128
    pos, meta = pl.pallas_call(
        functools.partial(_route_kernel, rows=rows),
        out_shape=[jax.ShapeDtypeStruct((rows, 128), i32), jax.ShapeDtypeStruct((8, 128), i32)],
        name="route",
    )(idx8[:2, :].reshape(rows, 128))
    return pos.reshape(-1), meta[0, :n_tiles], meta[1, 0:1]


def _expert_kernel(te_ref, nu_ref, pos_ref, hp_ref, w1_ref, w3_ref, w2_ref, y_ref, xbuf, sem, src_s, *,
                   m_rows):
    tm = TM_EXP
    i = pl.program_id(0)
    n_used = nu_ref[0]
    n_rows = src_s.shape[0]

    def issue(tile, slot):
        base = tile * tm
        for j in range(tm):
            t = src_s[base + j]
            pltpu.make_async_copy(hp_ref.at[pl.ds(t, 1), :], xbuf.at[slot, pl.ds(j, 1), :],
                                  sem.at[slot]).start()

    def wait(slot):
        pltpu.make_async_copy(hp_ref.at[pl.ds(0, tm), :], xbuf.at[slot], sem.at[slot]).wait()

    @pl.when(i == 0)
    def _():
        def clear(p, c):
            src_s[p] = 0
            return c

        lax.fori_loop(0, n_rows, clear, 0, unroll=16)

        def scatter(t, c):
            src_s[pos_ref[t]] = t
            src_s[pos_ref[m_rows + t]] = t
            return c

        lax.fori_loop(0, m_rows, scatter, 0, unroll=8)
        issue(0, 0)

    @pl.when(i < n_used)
    def _():
        slot = i % 2
        issue(i + 1, 1 - slot)
        wait(slot)
        xb = xbuf[slot].astype(bf16)
        a1 = jnp.dot(xb, w1_ref[...], preferred_element_type=f32)
        a3 = jnp.dot(xb, w3_ref[...], preferred_element_type=f32)
        act = (a1 * _sigmoid(a1) * a3).astype(bf16)
        y_ref[...] = jnp.dot(act, w2_ref[...], preferred_element_type=f32)

        @pl.when(i + 1 == n_used)
        def _():
            wait(1 - slot)

    @pl.when(i >= n_used)
    def _():
        y_ref[...] = jnp.zeros_like(y_ref)


def _experts(te, n_used, pos, hp, w1, w3, w2, l):
    tm = TM_EXP
    m_rows = hp.shape[0]
    n_tiles = te.shape[0]
    wspec = lambda r, c: pl.BlockSpec((None, None, r, c), lambda i, te, nu, pos: (l, te[i], 0, 0))
    return pl.pallas_call(
        functools.partial(_expert_kernel, m_rows=m_rows),
        grid_spec=pltpu.PrefetchScalarGridSpec(
            num_scalar_prefetch=3,
            grid=(n_tiles,),
            in_specs=[
                pl.BlockSpec(memory_space=pl.ANY),
                wspec(D_MODEL, D_FF),
                wspec(D_MODEL, D_FF),
                wspec(D_FF, D_MODEL),
            ],
            out_specs=pl.BlockSpec((tm, D_MODEL), lambda i, te, nu, pos: (i, 0)),
            scratch_shapes=[pltpu.VMEM((2, tm, D_MODEL), f32), pltpu.SemaphoreType.DMA((2,)),
                            pltpu.SMEM((n_tiles * tm,), i32)],
        ),
        out_shape=jax.ShapeDtypeStruct((n_tiles * tm, D_MODEL), f32),
        compiler_params=_params(("arbitrary",)),
        name="experts",
    )(te, n_used, pos, hp, w1, w3, w2)


def _combine_kernel(pos_ref, x_ref, mod_ref, wt_ref, g_ref, b_ref, y_ref, o_ref, buf, sem, *,
                    m_rows, nblocks):
    tm = TM_CMB
    i = pl.program_id(0)

    def issue(blk, slot):
        base = blk * tm

        def it(j, c):
            p0 = pos_ref[base + j]
            p1 = pos_ref[m_rows + base + j]
            pltpu.make_async_copy(y_ref.at[pl.ds(p0, 1), :], buf.at[slot, 0, pl.ds(j, 1), :],
                                  sem.at[slot]).start()
            pltpu.make_async_copy(y_ref.at[pl.ds(p1, 1), :], buf.at[slot, 1, pl.ds(j, 1), :],
                                  sem.at[slot]).start()
            return c

        lax.fori_loop(0, tm, it, 0, unroll=8)

    @pl.when(i == 0)
    def _():
        issue(0, 0)

    @pl.when(i + 1 < nblocks)
    def _():
        issue(i + 1, (i + 1) % 2)

    slot = i % 2
    for k in range(2):
        pltpu.make_async_copy(y_ref.at[pl.ds(0, tm), :], buf.at[slot, k], sem.at[slot]).wait()

    wt = wt_ref[...]
    moe = wt[:, 0:1] * buf[slot, 0] + wt[:, 1:2] * buf[slot, 1]
    gate = mod_ref[5:6, :]
    o_ref[...] = _ln(ALPHA * x_ref[...] + gate * moe) * g_ref[...] + b_ref[...]


def _combine(pos, x, mod_l, wts, ln_g, ln_b, y):
    tm = TM_CMB
    m_rows = x.shape[0]
    nblocks = m_rows // tm
    return pl.pallas_call(
        functools.partial(_combine_kernel, m_rows=m_rows, nblocks=nblocks),
        grid_spec=pltpu.PrefetchScalarGridSpec(
            num_scalar_prefetch=1,
            grid=(nblocks,),
            in_specs=[
                pl.BlockSpec((tm, D_MODEL), lambda i, p: (i, 0)),
                pl.BlockSpec((None, 6, D_MODEL), lambda i, p: (_mod_row(i, tm), 0, 0)),
                pl.BlockSpec((tm, 128), lambda i, p: (i, 0)),
                pl.BlockSpec((1, D_MODEL), lambda i, p: (0, 0)),
                pl.BlockSpec((1, D_MODEL), lambda i, p: (0, 0)),
                pl.BlockSpec(memory_space=pl.ANY),
            ],
            out_specs=pl.BlockSpec((tm, D_MODEL), lambda i, p: (i, 0)),
            scratch_shapes=[pltpu.VMEM((2, 2, tm, D_MODEL), f32), pltpu.SemaphoreType.DMA((2,))],
        ),
        out_shape=jax.ShapeDtypeStruct((m_rows, D_MODEL), f32),
        compiler_params=_params(("arbitrary",)),
        name="combine",
    )(pos, x, mod_l, wts, ln_g, ln_b, y)


def kernel(x, c, ctx, c_ctx, w_ada, b_ada, w_in, b_igate, b_fgate, mh_norm_g, conv_w, conv_b, w_out,
           ln1_g, ln1_b, w_router, b_router, w1, w3, w2, ln2_g, ln2_b):
    cond_raw = jnp.zeros((8, D_MODEL), f32).at[:BATCH].set(c).at[BATCH].set(c_ctx)
    mod = _ada(cond_raw, w_ada, b_ada).reshape(DEPTH, 8, 6, D_MODEL)

    xa = jnp.concatenate([x.reshape(N_LAT, D_MODEL), ctx.reshape(N_CTX, D_MODEL)], axis=0)
    w_router_p = jnp.zeros((D_MODEL, 128), f32).at[:, :N_EXPERTS].set(w_router)
    w_router_hi = w_router_p.astype(bf16)
    w_router_lo = (w_router_p - w_router_hi.astype(f32)).astype(bf16)
    w_router_p = jnp.concatenate([w_router_hi, w_router_lo], axis=1)
    b_router_c = b_router.reshape(N_EXPERTS, 1)
    w_tail, w_gate = _wprep(w_in)
    w1b, w3b, w2b, w_outb = w1.astype(bf16), w3.astype(bf16), w2.astype(bf16), w_out.astype(bf16)

    for l in range(DEPTH):
        last = l == DEPTH - 1
        proj, gates = _inproj(xa, mod[l], w_in, w_tail, w_gate, l)

        bias = jnp.concatenate([b_igate[l], b_fgate[l]]).astype(f32)
        m_lat, m_ctx = _mlstm3(proj, gates, bias, mh_norm_g[l].reshape(1, D_MLSTM), not last)

        xn, hp, idx8, wts = _mix(xa, mod[l], m_lat, m_ctx, proj, conv_w[l], conv_b[l].reshape(1, D_CONV),
                                 w_outb, ln1_g[l].reshape(1, D_MODEL),
                                 ln1_b[l].reshape(1, D_MODEL), w_router_p, b_router_c, not last, l)

        pos, te, n_used = _route(idx8, xn.shape[0])
        y = _experts(te, n_used, pos, hp, w1b, w3b, w2b, l)
        xa = _combine(pos, xn, mod[l], wts, ln2_g[l].reshape(1, D_MODEL), ln2_b[l].reshape(1, D_MODEL), y)

    return xa.reshape(BATCH, SEQ, D_MODEL)
```

```python
import functools

import jax
import jax.numpy as jnp
from jax import lax
from jax.experimental import pallas as pl
from jax.experimental.pallas import tpu as pltpu

f32 = jnp.float32
bf16 = jnp.bfloat16
i32 = jnp.int32
u32 = jnp.uint32

D_MODEL = 2048
BATCH = 4
SEQ = 2048
DEPTH = 4
GRID_W = 64
CTX_LEN = 256
D_MLSTM = 1024
HEADS = 4
DV = 256
DQK = 128
CHUNK = 64
D_CONV = 1024
D_CONV_H = 512
N_EXPERTS = 16
N_GROUPS = 4
EPG = 4
D_FF = 1024
ALPHA = (2 * DEPTH) ** 0.25
LN_EPS = 1e-6
QK_SCALE = DQK ** -0.5

N_LAT = BATCH * SEQ
N_CTX = BATCH * CTX_LEN
N_ALL = N_LAT + N_CTX
D_PROJ = 6144
N_GATE = 16
HALF = D_MODEL // 2

TM_IN = 1024
TN_IN = 512
TM_MIX = 256
TM_EXP = 256
TM_CMB = 256
VMEM_LIMIT = 56 * 1024 * 1024

HIGHEST = lax.Precision.HIGHEST


def _sigmoid(x):
    return 1.0 / (1.0 + jnp.exp(-x))


def _log_sigmoid(x):
    return jnp.minimum(x, 0.0) - jnp.log1p(jnp.exp(-jnp.abs(x)))


def _ln(x):
    mu = jnp.mean(x, axis=-1, keepdims=True)
    xc = x - mu
    var = jnp.mean(xc * xc, axis=-1, keepdims=True)
    return xc * lax.rsqrt(var + LN_EPS)


def _mod_row(i, tm):
    return jnp.minimum((i * tm) // SEQ, BATCH)


def _params(sem, vmem=VMEM_LIMIT):
    return pltpu.CompilerParams(dimension_semantics=sem, vmem_limit_bytes=vmem)


def _ada_kernel(c_ref, w_ref, b_ref, o_ref):
    c = c_ref[...]
    cond = c * _sigmoid(c)
    o_ref[...] = jnp.dot(cond, w_ref[...], preferred_element_type=f32) + b_ref[...]


def _ada(cond_raw, w_ada, b_ada):
    tn = 1024
    n = 6 * D_MODEL
    return pl.pallas_call(
        _ada_kernel,
        grid=(DEPTH, n // tn),
        in_specs=[
            pl.BlockSpec((8, D_MODEL), lambda l, j: (0, 0)),
            pl.BlockSpec((None, D_MODEL, tn), lambda l, j: (l, 0, j)),
            pl.BlockSpec((None, 1, tn), lambda l, j: (l, 0, j)),
        ],
        out_specs=pl.BlockSpec((None, 8, tn), lambda l, j: (l, 0, j)),
        out_shape=jax.ShapeDtypeStruct((DEPTH, 8, n), f32),
        compiler_params=_params(("parallel", "parallel")),
        name="ada",
    )(cond_raw, w_ada, b_ada.reshape(DEPTH, 1, n))


N_STATE = 2 * HEADS * DQK + D_MLSTM
N_STATE_BLOCKS = N_STATE // TN_IN
NT_DIMS = (((1,), (1,)), ((), ()))


def _inproj_kernel(x_ref, mod_ref, w_ref, wg_ref, o_ref, g_ref, h_scr, *, slab):
    j = pl.program_id(1)

    @pl.when(j == 0)
    def _():
        shift = mod_ref[0:1, :]
        scale = mod_ref[1:2, :]
        wg = wg_ref[...].astype(bf16)

        def body(s, c):
            r0 = pl.multiple_of(s * slab, slab)
            h = _ln(x_ref[pl.ds(r0, slab), :]) * (1.0 + scale) + shift
            hb = h.astype(bf16)
            h_scr[pl.ds(r0, slab), :] = hb
            g_ref[pl.ds(r0, slab), :] = lax.dot_general(hb, wg, NT_DIMS, preferred_element_type=f32)
            return c

        lax.fori_loop(0, TM_IN // slab, body, 0)

    o_ref[...] = lax.dot_general(h_scr[...], w_ref[0].astype(bf16), NT_DIMS,
                                 preferred_element_type=f32).astype(o_ref.dtype)


def _inproj(x, mod_l, w_in_t, l):
    m = x.shape[0]
    row0 = lambda j: pl.multiple_of(jnp.where(j < N_STATE_BLOCKS, j * TN_IN, j * TN_IN + N_GATE), 8)
    return pl.pallas_call(
        functools.partial(_inproj_kernel, slab=128),
        grid=(m // TM_IN, D_PROJ // TN_IN),
        in_specs=[
            pl.BlockSpec((TM_IN, D_MODEL), lambda i, j: (i, 0)),
            pl.BlockSpec((None, 6, D_MODEL), lambda i, j: (_mod_row(i, TM_IN), 0, 0)),
            pl.BlockSpec((pl.Element(1), pl.Element(TN_IN), pl.Element(D_MODEL)),
                         lambda i, j: (l, row0(j), 0)),
            pl.BlockSpec((None, N_GATE, D_MODEL), lambda i, j: (l, N_STATE // N_GATE, 0)),
        ],
        out_specs=[
            pl.BlockSpec((TM_IN, TN_IN), lambda i, j: (i, j)),
            pl.BlockSpec((TM_IN, N_GATE), lambda i, j: (i, 0)),
        ],
        out_shape=[
            jax.ShapeDtypeStruct((m, D_PROJ), bf16),
            jax.ShapeDtypeStruct((m, N_GATE), f32),
        ],
        scratch_shapes=[pltpu.VMEM((TM_IN, D_MODEL), bf16)],
        compiler_params=_params(("parallel", "arbitrary")),
        name="inproj",
    )(x, mod_l, w_in_t, w_in_t)


def _mlstm_chunk(q, k, v, fc_raw, ic_raw, fr_raw, ir_raw, b_i, b_f, ct_ref, n_ref, d, m, fwd):
    L = CHUNK
    lf_c = _log_sigmoid(fc_raw + b_f)
    i_c = ic_raw + b_i
    lf_r = _log_sigmoid(fr_raw + b_f)
    i_r = ir_raw + b_i
    rr = lax.broadcasted_iota(i32, (L, L), 0)
    cc = lax.broadcasted_iota(i32, (L, L), 1)
    lo = rr >= cc
    up = rr <= cc
    mask = lo if fwd else up
    a_mat = mask.astype(f32)
    a_t = (up if fwd else lo).astype(f32)
    bcol = jnp.dot(a_mat, jnp.broadcast_to(lf_c, (L, L)), precision=HIGHEST,
                   preferred_element_type=f32)
    brow = jnp.dot(jnp.broadcast_to(lf_r, (L, L)), a_t, precision=HIGHEST,
                   preferred_element_type=f32)
    dm = jnp.where(mask, bcol - brow + i_r, -jnp.inf)
    mloc = jnp.max(dm, axis=1, keepdims=True)
    wloc = jnp.exp(dm - mloc)
    b1 = bcol[:, 0:1]
    m_t = jnp.maximum(b1 + m, mloc)
    inter = jnp.exp(b1 + m - m_t)
    a = jnp.exp(mloc - m_t)
    qk = lax.dot_general(q, k, (((1,), (1,)), ((), ())), preferred_element_type=f32) * QK_SCALE
    sloc = qk * wloc
    ct = ct_ref[d]
    n = n_ref[d]
    num = inter * jnp.dot(q, ct.astype(bf16), preferred_element_type=f32) \
        + a * jnp.dot(sloc.astype(bf16), v, preferred_element_type=f32)
    den = inter * jnp.sum(q.astype(f32) * n, axis=1, keepdims=True) \
        + a * jnp.sum(sloc, axis=1, keepdims=True)
    hout = num / jnp.maximum(jnp.abs(den), jnp.exp(-m_t))
    last = L - 1 if fwd else 0
    b_last = b1[last:last + 1, :]
    wl = jnp.exp(b_last - b1 + i_c - mloc[last:last + 1, :])
    kw = k.astype(f32) * (wl * QK_SCALE)
    u = lax.dot_general(kw.astype(bf16), v, (((0,), (0,)), ((), ())), preferred_element_type=f32)
    decay = inter[last:last + 1, :]
    a_l = a[last:last + 1, :]
    ct_ref[d] = decay * ct + a_l * u
    n_ref[d] = decay * n + a_l * jnp.sum(kw, axis=0, keepdims=True)
    return hout, m_t[last:last + 1, :]


def _mlstm_kernel(bias_ref, ql_ref, kl_ref, vl_ref, ol_ref, qc_ref, kc_ref, vc_ref, oc_ref,
                  gcl_ref, grl_ref, gcc_ref, grc_ref, gain_ref, *rest, has_ctx_out):
    if has_ctx_out:
        ml_ref, mc_ref, ct_ref, n_ref, hfl, hbl, hfc, hbc = rest
    else:
        ml_ref, ct_ref, n_ref, hfl, hbl = rest
        mc_ref = hfc = hbc = None
    b = pl.program_id(0)
    h = pl.program_id(1)
    L = CHUNK
    bi = (bias_ref[h], bias_ref[HEADS + h])
    bf = (bias_ref[2 * HEADS + h], bias_ref[3 * HEADS + h])
    ct_ref[...] = jnp.zeros_like(ct_ref)
    n_ref[...] = jnp.zeros_like(n_ref)

    def run(q_ref, k_ref, v_ref, gc_ref, gr_ref, c_base, nchunks, hf, hb, carry):
        def body(j, carry):
            m_f, m_b = carry
            outs = []
            for d, fwd in ((0, True), (1, False)):
                c = j if fwd else nchunks - 1 - j
                r0 = pl.multiple_of(c * L, L)
                q = q_ref[pl.ds(r0, L), :]
                k = k_ref[pl.ds(r0, L), :]
                v = v_ref[pl.ds(r0, L), :]
                gcol = gc_ref[c_base + c]
                ic = gcol[:, d:d + 1]
                fc = gcol[:, 2 + d:3 + d]
                ir = gr_ref[d, pl.ds(c_base + c, 1), :]
                fr = gr_ref[2 + d, pl.ds(c_base + c, 1), :]
                hout, m_new = _mlstm_chunk(q, k, v, fc, ic, fr, ir, bi[d], bf[d], ct_ref, n_ref, d,
                                           m_f if fwd else m_b, fwd)
                if hf is not None:
                    (hf if fwd else hb)[pl.ds(r0, L), :] = hout
                outs.append(m_new)
            return tuple(outs)

        return lax.fori_loop(0, nchunks, body, carry)

    zero = jnp.zeros((1, 1), f32)
    carry = run(qc_ref, kc_ref, vc_ref, gcc_ref, grc_ref, b * (CTX_LEN // L), CTX_LEN // L, hfc, hbc,
                (zero, zero))
    run(ql_ref, kl_ref, vl_ref, gcl_ref, grl_ref, 0, SEQ // L, hfl, hbl, carry)

    gain = gain_ref[...]

    def finish(hf, hb, o_ref, out_ref, rows):
        slab = 256

        def ep(s, c):
            r0 = pl.multiple_of(s * slab, slab)
            hs = hf[pl.ds(r0, slab), :] + hb[pl.ds(r0, slab), :]
            r = lax.rsqrt(jnp.mean(hs * hs, axis=-1, keepdims=True) + LN_EPS)
            o = o_ref[pl.ds(r0, slab), :].astype(f32)
            out_ref[pl.ds(r0, slab), :] = (hs * r * gain * _sigmoid(o)).astype(out_ref.dtype)
            return c

        lax.fori_loop(0, rows // slab, ep, 0)

    finish(hfl, hbl, ol_ref, ml_ref, SEQ)
    if has_ctx_out:
        finish(hfc, hbc, oc_ref, mc_ref, CTX_LEN)


def _mlstm(proj, gates, bias, gain, has_ctx_out):
    L = CHUNK
    g = gates[:, :N_GATE]
    g4 = jnp.stack([g[:, 0:4], g[:, 4:8], g[:, 8:12], g[:, 12:16]], axis=-1)
    gh = jnp.transpose(g4, (1, 0, 2))
    gcol_l = gh[:, :N_LAT].reshape(HEADS, N_LAT // L, L, 4)
    gcol_c = gh[:, N_LAT:].reshape(HEADS, N_CTX // L, L, 4)
    gr = jnp.transpose(g4, (1, 2, 0))
    grow_l = gr[:, :, :N_LAT].reshape(HEADS, 4, N_LAT // L, L)
    grow_c = gr[:, :, N_LAT:].reshape(HEADS, 4, N_CTX // L, L)

    lat_rb = lambda b, h: b
    ctx_rb = lambda b, h: N_LAT // CTX_LEN + b
    in_specs = [
        pl.BlockSpec(memory_space=pltpu.SMEM),
        pl.BlockSpec((SEQ, DQK), lambda b, h: (lat_rb(b, h), h)),
        pl.BlockSpec((SEQ, DQK), lambda b, h: (lat_rb(b, h), HEADS + h)),
        pl.BlockSpec((SEQ, DV), lambda b, h: (lat_rb(b, h), HEADS + h)),
        pl.BlockSpec((SEQ, DV), lambda b, h: (lat_rb(b, h), 2 * HEADS + h)),
        pl.BlockSpec((CTX_LEN, DQK), lambda b, h: (ctx_rb(b, h), h)),
        pl.BlockSpec((CTX_LEN, DQK), lambda b, h: (ctx_rb(b, h), HEADS + h)),
        pl.BlockSpec((CTX_LEN, DV), lambda b, h: (ctx_rb(b, h), HEADS + h)),
        pl.BlockSpec((CTX_LEN, DV), lambda b, h: (ctx_rb(b, h), 2 * HEADS + h)),
        pl.BlockSpec((None, SEQ // L, L, 4), lambda b, h: (h, b, 0, 0)),
        pl.BlockSpec((None, 4, SEQ // L, L), lambda b, h: (h, 0, b, 0)),
        pl.BlockSpec((None, N_CTX // L, L, 4), lambda b, h: (h, 0, 0, 0)),
        pl.BlockSpec((None, 4, N_CTX // L, L), lambda b, h: (h, 0, 0, 0)),
        pl.BlockSpec((1, DV), lambda b, h: (0, h)),
    ]
    out_specs = [pl.BlockSpec((SEQ, DV), lambda b, h: (b, h))]
    out_shape = [jax.ShapeDtypeStruct((N_LAT, D_MLSTM), bf16)]
    scratch = [pltpu.VMEM((2, DQK, DV), f32), pltpu.VMEM((2, 1, DQK), f32),
               pltpu.VMEM((SEQ, DV), f32), pltpu.VMEM((SEQ, DV), f32)]
    if has_ctx_out:
        out_specs.append(pl.BlockSpec((CTX_LEN, DV), lambda b, h: (b, h)))
        out_shape.append(jax.ShapeDtypeStruct((N_CTX, D_MLSTM), bf16))
        scratch += [pltpu.VMEM((CTX_LEN, DV), f32), pltpu.VMEM((CTX_LEN, DV), f32)]
    res = pl.pallas_call(
        functools.partial(_mlstm_kernel, has_ctx_out=has_ctx_out),
        grid=(BATCH, HEADS),
        in_specs=in_specs,
        out_specs=out_specs,
        out_shape=out_shape,
        scratch_shapes=scratch,
        compiler_params=_params(("parallel", "parallel")),
        name="mlstm",
    )(bias, proj, proj, proj, proj, proj, proj, proj, proj, gcol_l, grow_l, gcol_c, grow_c, gain)
    return res if has_ctx_out else (res[0], None)


GROUP = 4
N_CHUNK_CTX = CTX_LEN // CHUNK
N_CHUNK_LAT = SEQ // CHUNK
N_CHUNK = N_CHUNK_CTX + N_CHUNK_LAT
N_GROUP_LAT = N_CHUNK_LAT // GROUP
LAT_ROW0 = 8
D_AUG = DV + 128


CVT_ROWS_IN = D_MODEL // (SEQ // CHUNK // 4)
CVT_ROWS_OUT = D_FF // (SEQ // CHUNK // 4)


def _mlstm3_kernel(bias_ref, ql_ref, kl_ref, vl_ref, ol_ref, qc_ref, kc_ref, vc_ref, oc_ref,
                   gcl_ref, grl_ref, gcc_ref, grc_ref, gain_ref, w1_hbm, w3_hbm, w2_hbm, *rest,
                   has_ctx_out, layer):
    if has_ctx_out:
        ml_ref, mc_ref = rest[:2]
        rest = rest[2:]
    else:
        ml_ref, mc_ref = rest[0], None
        rest = rest[1:]
    w1b_hbm, w3b_hbm, w2b_hbm, s_ref, st_ref, msc, rows_s, cols_s = rest[:8]
    cin = rest[8:11]
    cout = rest[11:14]
    csem_in, csem_out = rest[14:16]
    L = CHUNK
    b = pl.program_id(0)
    h = pl.program_id(1)
    expert = b * HEADS + h
    cvt_src = (w1_hbm, w3_hbm, w2_hbm)
    cvt_dst = (w1b_hbm, w3b_hbm, w2b_hbm)
    cvt_rows = (CVT_ROWS_IN, CVT_ROWS_IN, CVT_ROWS_OUT)

    def cvt_in(k, g):
        r0 = pl.multiple_of(g * cvt_rows[k], cvt_rows[k])
        return pltpu.make_async_copy(cvt_src[k].at[layer, expert, pl.ds(r0, cvt_rows[k]), :], cin[k],
                                     csem_in.at[k])

    def cvt_out(k, g):
        r0 = pl.multiple_of(g * cvt_rows[k], cvt_rows[k])
        return pltpu.make_async_copy(cout[k], cvt_dst[k].at[expert, pl.ds(r0, cvt_rows[k]), :],
                                     csem_out.at[k])

    def cvt_finish(k, g):
        cvt_in(k, g).wait()

        @pl.when(g > 0)
        def _():
            cvt_out(k, g - 1).wait()

        cout[k][...] = cin[k][...].astype(bf16)
        cvt_out(k, g).start()
    bi = (bias_ref[h], bias_ref[HEADS + h])
    bf = (bias_ref[2 * HEADS + h], bias_ref[3 * HEADS + h])

    rr = lax.broadcasted_iota(i32, (L, L), 0)
    cc = lax.broadcasted_iota(i32, (L, L), 1)
    lo_mask = rr >= cc
    up_mask = rr <= cc
    masks = (lo_mask, up_mask)

    def row_forms(gr_ref, sl, dst0, n):
        for d in range(2):
            tri = (up_mask if d == 0 else lo_mask).astype(f32)
            i_r = gr_ref[d, sl, :] + bi[d]
            lf_r = _log_sigmoid(gr_ref[2 + d, sl, :] + bf[d])
            b_r = jnp.dot(lf_r, tri, precision=HIGHEST, preferred_element_type=f32)
            rows_s[d, dst0:dst0 + n, :] = i_r - b_r

    row_forms(grc_ref, pl.ds(b * N_CHUNK_CTX, N_CHUNK_CTX), 0, N_CHUNK_CTX)
    row_forms(grl_ref, slice(None), LAT_ROW0, N_CHUNK_LAT)

    kind = lax.broadcasted_iota(i32, (1, 4 * GROUP), 1) % 4
    bias_v = jnp.where(kind == 0, bi[0], jnp.where(kind == 1, bi[1], jnp.where(kind == 2, bf[0], bf[1])))
    tpos = lax.broadcasted_iota(i32, (L, 4 * GROUP), 0)

    def col_forms(x):
        y = x + bias_v
        y = jnp.where(kind >= 2, _log_sigmoid(y), y)
        pre = y
        suf = y
        s = 1
        while s < L:
            pre = pre + jnp.where(tpos >= s, pltpu.roll(pre, s, 0), 0.0)
            suf = suf + jnp.where(tpos < L - s, pltpu.roll(suf, L - s, 0), 0.0)
            s *= 2
        return jnp.where(kind == 2, pre, jnp.where(kind == 3, suf, y))

    cols_s[0] = col_forms(gcc_ref[...])

    def col_body(g, c):
        cols_s[g + 1] = col_forms(gcl_ref[g])
        return c

    lax.fori_loop(0, N_GROUP_LAT, col_body, 0)

    ones_col = (lax.broadcasted_iota(i32, (L, 128), 1) == 0).astype(bf16)

    s_ref[...] = jnp.zeros_like(s_ref)

    def state_step(d, c, row, col, j, k, v, m):
        i_c = col[:, 4 * j + d:4 * j + d + 1]
        b1 = col[:, 4 * j + 2 + d:4 * j + 3 + d]
        last = L - 1 if d == 0 else 0
        b_last = b1[last:last + 1, :]
        ct = i_c - b1
        mx = jnp.max(ct, axis=0, keepdims=True)
        wl = jnp.exp(ct - mx)
        mloc_last = b_last + mx
        m_new = jnp.maximum(b_last + m, mloc_last)
        decay = jnp.exp(b_last + m - m_new)
        a_l = jnp.exp(mloc_last - m_new)
        s_old = s_ref[d]
        st_ref[d, c] = s_old.astype(bf16)
        msc[d, pl.ds(row, 1), :] = jnp.broadcast_to(m, (1, 128))
        kw = (k.astype(f32) * (wl * QK_SCALE)).astype(bf16)
        vaug = jnp.concatenate([v, ones_col], axis=1)
        u = lax.dot_general(kw, vaug, (((0,), (0,)), ((), ())), preferred_element_type=f32)
        s_ref[d] = decay * s_old + a_l * u
        return m_new

    zero = jnp.zeros((1, 1), f32)
    m_f = m_b = zero
    col0 = cols_s[0]
    for step in range(GROUP):
        jf, jb = step, GROUP - 1 - step
        m_f = state_step(0, jf, jf, col0, jf, kc_ref[jf * L:(jf + 1) * L, :], vc_ref[jf * L:(jf + 1) * L, :], m_f)
        m_b = state_step(1, jb, jb, col0, jb, kc_ref[jb * L:(jb + 1) * L, :], vc_ref[jb * L:(jb + 1) * L, :], m_b)

    def state_body(it, carry):
        m_f, m_b = carry
        cvt_in(0, it - 1).start()
        gf = it
        gb = N_GROUP_LAT + 1 - it
        colf = cols_s[gf]
        colb = cols_s[gb]
        for step in range(GROUP):
            jf, jb = step, GROUP - 1 - step
            clf = (gf - 1) * GROUP + jf
            clb = (gb - 1) * GROUP + jb
            rf = pl.multiple_of(clf * L, L)
            rb = pl.multiple_of(clb * L, L)
            m_f = state_step(0, N_CHUNK_CTX + clf, LAT_ROW0 + clf, colf, jf,
                             kl_ref[pl.ds(rf, L), :], vl_ref[pl.ds(rf, L), :], m_f)
            m_b = state_step(1, N_CHUNK_CTX + clb, LAT_ROW0 + clb, colb, jb,
                             kl_ref[pl.ds(rb, L), :], vl_ref[pl.ds(rb, L), :], m_b)
        cvt_finish(0, it - 1)
        return m_f, m_b

    lax.fori_loop(1, N_GROUP_LAT + 1, state_body, (m_f, m_b))
    cvt_out(0, N_GROUP_LAT - 1).wait()

    gain = gain_ref[...]

    def out_chunk(q, k, v, o, col, j, c, row, out_ref, r0):
        qk = lax.dot_general(q, k, (((1,), (1,)), ((), ())), preferred_element_type=f32) * QK_SCALE
        vaug = jnp.concatenate([v, ones_col], axis=1)
        sl = []
        per = []
        for d in range(2):
            b1 = col[:, 4 * j + 2 + d:4 * j + 3 + d]
            dm = jnp.where(masks[d], b1 + rows_s[d, pl.ds(row, 1), :], -jnp.inf)
            mloc = jnp.max(dm, axis=1, keepdims=True)
            wloc = jnp.exp(dm - mloc)
            m_prev = msc[d, pl.ds(row, 1), :][:, 0:1]
            m_t = jnp.maximum(b1 + m_prev, mloc)
            inter = jnp.exp(b1 + m_prev - m_t)
            a = jnp.exp(mloc - m_t)
            sl.append((qk * wloc).astype(bf16))
            per.append((m_t, inter, a))
        x = jnp.dot(jnp.concatenate(sl, axis=0), vaug, preferred_element_type=f32)
        hs = None
        for d in range(2):
            y = jnp.dot(q, st_ref[d, c], preferred_element_type=f32)
            m_t, inter, a = per[d]
            xd = x[d * L:(d + 1) * L, :]
            den = inter * y[:, DV:DV + 1] + a * xd[:, DV:DV + 1]
            rinv = 1.0 / jnp.maximum(jnp.abs(den), jnp.exp(-m_t))
            hd = (inter * rinv) * y[:, :DV] + (a * rinv) * xd[:, :DV]
            hs = hd if hs is None else hs + hd
        r = lax.rsqrt(jnp.mean(hs * hs, axis=-1, keepdims=True) + LN_EPS)
        out_ref[pl.ds(r0, L), :] = (hs * r * gain * _sigmoid(o.astype(f32))).astype(out_ref.dtype)

    if has_ctx_out:
        for j in range(GROUP):
            sl_ = slice(j * L, (j + 1) * L)
            out_chunk(qc_ref[sl_, :], kc_ref[sl_, :], vc_ref[sl_, :], oc_ref[sl_, :], col0, j, j, j,
                      mc_ref, j * L)

    def out_body(g, carry):
        cvt_in(1, g).start()
        cvt_in(2, g).start()
        col = cols_s[g + 1]
        for j in range(GROUP):
            cl = g * GROUP + j
            r0 = pl.multiple_of(cl * L, L)
            out_chunk(ql_ref[pl.ds(r0, L), :], kl_ref[pl.ds(r0, L), :], vl_ref[pl.ds(r0, L), :],
                      ol_ref[pl.ds(r0, L), :], col, j, N_CHUNK_CTX + cl, LAT_ROW0 + cl, ml_ref, r0)
        cvt_finish(1, g)
        cvt_finish(2, g)
        return carry

    lax.fori_loop(0, N_GROUP_LAT, out_body, 0)
    cvt_out(1, N_GROUP_LAT - 1).wait()
    cvt_out(2, N_GROUP_LAT - 1).wait()


def _mlstm3(proj, gates, bias, gain, w1, w3, w2, has_ctx_out, layer):
    L = CHUNK
    g = gates[:, :N_GATE]
    g4 = jnp.stack([g[:, 0:4], g[:, 4:8], g[:, 8:12], g[:, 12:16]], axis=-1)
    gh = jnp.transpose(g4, (1, 0, 2))
    gcol_l = gh[:, :N_LAT].reshape(HEADS, BATCH * N_GROUP_LAT, GROUP, L, 4)
    gcol_l = jnp.transpose(gcol_l, (0, 1, 3, 2, 4)).reshape(HEADS, BATCH * N_GROUP_LAT, L, 4 * GROUP)
    gcol_c = gh[:, N_LAT:].reshape(HEADS, BATCH, GROUP, L, 4)
    gcol_c = jnp.transpose(gcol_c, (0, 1, 3, 2, 4)).reshape(HEADS, BATCH, L, 4 * GROUP)
    gr = jnp.transpose(g4, (1, 2, 0))
    grow_l = gr[:, :, :N_LAT].reshape(HEADS, 4, N_LAT // L, L)
    grow_c = gr[:, :, N_LAT:].reshape(HEADS, 4, N_CTX // L, L)

    ctx_rb = lambda b, h: N_LAT // CTX_LEN + b
    in_specs = [
        pl.BlockSpec(memory_space=pltpu.SMEM),
        pl.BlockSpec((SEQ, DQK), lambda b, h: (b, h)),
        pl.BlockSpec((SEQ, DQK), lambda b, h: (b, HEADS + h)),
        pl.BlockSpec((SEQ, DV), lambda b, h: (b, HEADS + h)),
        pl.BlockSpec((SEQ, DV), lambda b, h: (b, 2 * HEADS + h)),
        pl.BlockSpec((CTX_LEN, DQK), lambda b, h: (ctx_rb(b, h), h)),
        pl.BlockSpec((CTX_LEN, DQK), lambda b, h: (ctx_rb(b, h), HEADS + h)),
        pl.BlockSpec((CTX_LEN, DV), lambda b, h: (ctx_rb(b, h), HEADS + h)),
        pl.BlockSpec((CTX_LEN, DV), lambda b, h: (ctx_rb(b, h), 2 * HEADS + h)),
        pl.BlockSpec((None, N_GROUP_LAT, L, 4 * GROUP), lambda b, h: (h, b, 0, 0)),
        pl.BlockSpec((None, 4, N_CHUNK_LAT, L), lambda b, h: (h, 0, b, 0)),
        pl.BlockSpec((None, None, L, 4 * GROUP), lambda b, h: (h, b, 0, 0)),
        pl.BlockSpec((None, 4, N_CTX // L, L), lambda b, h: (h, 0, 0, 0)),
        pl.BlockSpec((1, DV), lambda b, h: (0, h)),
        pl.BlockSpec(memory_space=pl.ANY),
        pl.BlockSpec(memory_space=pl.ANY),
        pl.BlockSpec(memory_space=pl.ANY),
    ]
    out_specs = [pl.BlockSpec((SEQ, DV), lambda b, h: (b, h))]
    out_shape = [jax.ShapeDtypeStruct((N_LAT, D_MLSTM), bf16)]
    if has_ctx_out:
        out_specs.append(pl.BlockSpec((CTX_LEN, DV), lambda b, h: (b, h)))
        out_shape.append(jax.ShapeDtypeStruct((N_CTX, D_MLSTM), bf16))
    out_specs += [pl.BlockSpec(memory_space=pl.ANY)] * 3
    out_shape += [jax.ShapeDtypeStruct((N_EXPERTS, D_MODEL, D_FF), bf16),
                  jax.ShapeDtypeStruct((N_EXPERTS, D_MODEL, D_FF), bf16),
                  jax.ShapeDtypeStruct((N_EXPERTS, D_FF, D_MODEL), bf16)]
    scratch = [
        pltpu.VMEM((2, DQK, D_AUG), f32),
        pltpu.VMEM((2, N_CHUNK, DQK, D_AUG), bf16),
        pltpu.VMEM((2, LAT_ROW0 + N_CHUNK_LAT, 128), f32),
        pltpu.VMEM((2, LAT_ROW0 + N_CHUNK_LAT, L), f32),
        pltpu.VMEM((N_GROUP_LAT + 1, L, 4 * GROUP), f32),
        pltpu.VMEM((CVT_ROWS_IN, D_FF), f32),
        pltpu.VMEM((CVT_ROWS_IN, D_FF), f32),
        pltpu.VMEM((CVT_ROWS_OUT, D_MODEL), f32),
        pltpu.VMEM((CVT_ROWS_IN, D_FF), bf16),
        pltpu.VMEM((CVT_ROWS_IN, D_FF), bf16),
        pltpu.VMEM((CVT_ROWS_OUT, D_MODEL), bf16),
        pltpu.SemaphoreType.DMA((3,)),
        pltpu.SemaphoreType.DMA((3,)),
    ]
    assert N_EXPERTS == BATCH * HEADS
    res = pl.pallas_call(
        functools.partial(_mlstm3_kernel, has_ctx_out=has_ctx_out, layer=layer),
        grid=(BATCH, HEADS),
        in_specs=in_specs,
        out_specs=out_specs,
        out_shape=out_shape,
        scratch_shapes=scratch,
        compiler_params=_params(("arbitrary", "arbitrary")),
        name="mlstm",
    )(bias, proj, proj, proj, proj, proj, proj, proj, proj, gcol_l, grow_l, gcol_c, grow_c, gain,
      w1, w3, w2)
    if has_ctx_out:
        return res[0], res[1], res[2:]
    return res[0], None, res[1:]


def _top2_rows(vals):
    best = vals[0]
    bi = jnp.zeros(best.shape, i32)
    for j in range(1, len(vals)):
        take = vals[j] > best
        best = jnp.where(take, vals[j], best)
        bi = jnp.where(take, j, bi)
    sec = None
    si = None
    for j in range(len(vals)):
        cand = jnp.where(bi == j, -jnp.inf, vals[j])
        if sec is None:
            sec, si = cand, jnp.zeros(best.shape, i32)
        else:
            take = cand > sec
            sec = jnp.where(take, cand, sec)
            si = jnp.where(take, j, si)
    return bi, si


def _mix_kernel(x_ref, mod_ref, ml_ref, mc_ref, u_ref, bg_ref, cg_ref, ut_ref, ct_ref, ub_ref, cb_ref,
                cw_ref, cbias_ref, wo_ref, g1_ref, b1_ref, wr_ref, br_ref,
                xo_ref, hp_ref, idx_ref, wt_ref, m_scr, y_scr, *, n_lat_blocks, has_ctx):
    tm = TM_MIX
    i = pl.program_id(0)
    gate = mod_ref[2:3, :]
    shift2 = mod_ref[3:4, :]
    scale2 = mod_ref[4:5, :]
    cw = cw_ref[...]
    cbias = cbias_ref[...]
    row = lax.broadcasted_iota(i32, (tm, 1), 0)

    def shifted(z, first, last):
        prev = jnp.where(first, 0.0, pltpu.roll(z, 1, 0))
        nxt = jnp.where(last, 0.0, pltpu.roll(z, tm - 1, 0))
        return prev, nxt

    def lat_branch():
        z = cg_ref[...].astype(f32) * u_ref[...].astype(f32)
        col = row % GRID_W
        zh = z[:, :D_CONV_H]
        prev, nxt = shifted(zh, col == 0, col == GRID_W - 1)
        yh = cw[0:1, :D_CONV_H] * prev + cw[1:2, :D_CONV_H] * zh + cw[2:3, :D_CONV_H] * nxt
        bpb = SEQ // tm
        top_ok = (i % bpb != 0).astype(f32)
        bot_ok = (i % bpb != bpb - 1).astype(f32)
        zt = ct_ref[...].astype(f32) * ut_ref[...].astype(f32) * top_ok
        zb = cb_ref[...].astype(f32) * ub_ref[...].astype(f32) * bot_ok
        zv = z[:, D_CONV_H:]
        zext = jnp.concatenate([zt, zv, zb], axis=0)
        yv = cw[0:1, D_CONV_H:] * zext[0:tm] + cw[1:2, D_CONV_H:] * zv \
            + cw[2:3, D_CONV_H:] * zext[2 * GRID_W:2 * GRID_W + tm]
        y = jnp.concatenate([yh, yv], axis=1) + cbias
        y_scr[...] = (bg_ref[...].astype(f32) * y).astype(bf16)
        m_scr[...] = ml_ref[...]

    def ctx_branch():
        z = cg_ref[...].astype(f32) * u_ref[...].astype(f32)
        pos = row % CTX_LEN
        prev, nxt = shifted(z, pos == 0, pos == CTX_LEN - 1)
        y = cw[0:1, :] * prev + cw[1:2, :] * z + cw[2:3, :] * nxt + cbias
        y_scr[...] = (bg_ref[...].astype(f32) * y).astype(bf16)
        m_scr[...] = mc_ref[...]

    if has_ctx:
        pl.when(i < n_lat_blocks)(lat_branch)
        pl.when(i >= n_lat_blocks)(ctx_branch)
    else:
        lat_branch()

    out = jnp.dot(m_scr[...], wo_ref[0:D_MLSTM, :], preferred_element_type=f32) \
        + jnp.dot(y_scr[...], wo_ref[D_MLSTM:, :], preferred_element_type=f32)
    xn = _ln(ALPHA * x_ref[...] + gate * out) * g1_ref[...] + b1_ref[...]
    xo_ref[...] = xn
    h2 = _ln(xn) * (1.0 + scale2) + shift2

    hp_ref[...] = h2

    h_hi = h2.astype(bf16)
    hs = jnp.concatenate([h_hi, (h2 - h_hi.astype(f32)).astype(bf16)], axis=0)
    pr = jnp.dot(hs, wr_ref[...], preferred_element_type=f32)
    logits = pr[:tm, :128] + (pr[:tm, 128:] + pr[tm:, :128])
    lt = logits.T
    s = _sigmoid(lt[0:N_EXPERTS, :])
    sb = s + br_ref[...]
    sb_rows = [sb[e:e + 1, :] for e in range(N_EXPERTS)]
    s_rows = [s[e:e + 1, :] for e in range(N_EXPERTS)]
    gscores = []
    for g in range(N_GROUPS):
        a_, b_, c_, d_ = sb_rows[EPG * g:EPG * g + EPG]
        hi1, lo1 = jnp.maximum(a_, b_), jnp.minimum(a_, b_)
        hi2, lo2 = jnp.maximum(c_, d_), jnp.minimum(c_, d_)
        top = jnp.maximum(hi1, hi2)
        second = jnp.maximum(jnp.minimum(hi1, hi2), jnp.maximum(lo1, lo2))
        gscores.append(top + second)
    gbest = gscores[0]
    gsel = jnp.zeros(gbest.shape, i32)
    for g in range(1, N_GROUPS):
        take = gscores[g] > gbest
        gbest = jnp.where(take, gscores[g], gbest)
        gsel = jnp.where(take, g, gsel)

    def pick_group(rows, j):
        v = rows[j]
        for g in range(1, N_GROUPS):
            v = jnp.where(gsel == g, rows[EPG * g + j], v)
        return v

    cand_b = [pick_group(sb_rows, j) for j in range(EPG)]
    cand_s = [pick_group(s_rows, j) for j in range(EPG)]
    i1, i2 = _top2_rows(cand_b)

    def pick_idx(rows, idx):
        v = rows[0]
        for j in range(1, EPG):
            v = jnp.where(idx == j, rows[j], v)
        return v

    s1 = pick_idx(cand_s, i1)
    s2 = pick_idx(cand_s, i2)
    tot = s1 + s2
    w1 = s1 / tot
    w2 = s2 / tot
    e1 = gsel * EPG + i1
    e2 = gsel * EPG + i2
    r8 = lax.broadcasted_iota(i32, (8, tm), 0)
    idx_ref[...] = jnp.where(r8 == 0, e1, jnp.where(r8 == 1, e2, 0))
    r128 = lax.broadcasted_iota(i32, (128, tm), 0)
    wmat = jnp.where(r128 == 0, w1, jnp.where(r128 == 1, w2, 0.0))
    wt_ref[...] = wmat.T


def _mix(x, mod_l, m_lat, m_ctx, proj, conv_w, conv_b, w_out, ln_g, ln_b, w_router, b_router, has_ctx, l):
    tm = TM_MIX
    m_rows = N_ALL if has_ctx else N_LAT
    nlb = N_LAT // tm
    nblocks = m_rows // tm
    hb = tm // GRID_W
    n_hblocks = N_ALL // GRID_W
    if m_ctx is None:
        m_ctx = m_lat
    ncb = m_ctx.shape[0] // tm
    in_specs = [
        pl.BlockSpec((tm, D_MODEL), lambda i: (i, 0)),
        pl.BlockSpec((None, 6, D_MODEL), lambda i: (_mod_row(i, tm), 0, 0)),
        pl.BlockSpec((tm, D_MLSTM), lambda i: (jnp.minimum(i, nlb - 1), 0)),
        pl.BlockSpec((tm, D_MLSTM), lambda i: (jnp.clip(i - nlb, 0, ncb - 1), 0)),
        pl.BlockSpec((tm, D_CONV), lambda i: (i, 3)),
        pl.BlockSpec((tm, D_CONV), lambda i: (i, 4)),
        pl.BlockSpec((tm, D_CONV), lambda i: (i, 5)),
        pl.BlockSpec((GRID_W, D_CONV_H), lambda i: (jnp.maximum(i * hb - 1, 0), 7)),
        pl.BlockSpec((GRID_W, D_CONV_H), lambda i: (jnp.maximum(i * hb - 1, 0), 11)),
        pl.BlockSpec((GRID_W, D_CONV_H), lambda i: (jnp.minimum((i + 1) * hb, n_hblocks - 1), 7)),
        pl.BlockSpec((GRID_W, D_CONV_H), lambda i: (jnp.minimum((i + 1) * hb, n_hblocks - 1), 11)),
        pl.BlockSpec((3, D_CONV), lambda i: (0, 0)),
        pl.BlockSpec((1, D_CONV), lambda i: (0, 0)),
        pl.BlockSpec((None, D_MODEL, D_MODEL), lambda i: (l, 0, 0)),
        pl.BlockSpec((1, D_MODEL), lambda i: (0, 0)),
        pl.BlockSpec((1, D_MODEL), lambda i: (0, 0)),
        pl.BlockSpec((D_MODEL, 256), lambda i: (0, 0)),
        pl.BlockSpec((N_EXPERTS, 1), lambda i: (0, 0)),
    ]
    out_specs = [
        pl.BlockSpec((tm, D_MODEL), lambda i: (i, 0)),
        pl.BlockSpec((tm, D_MODEL), lambda i: (i, 0)),
        pl.BlockSpec((8, tm), lambda i: (0, i)),
        pl.BlockSpec((tm, 128), lambda i: (i, 0)),
    ]
    out_shape = [
        jax.ShapeDtypeStruct((m_rows, D_MODEL), f32),
        jax.ShapeDtypeStruct((m_rows, D_MODEL), f32),
        jax.ShapeDtypeStruct((8, m_rows), i32),
        jax.ShapeDtypeStruct((m_rows, 128), f32),
    ]
    return pl.pallas_call(
        functools.partial(_mix_kernel, n_lat_blocks=nlb, has_ctx=has_ctx),
        grid=(nblocks,),
        in_specs=in_specs,
        out_specs=out_specs,
        out_shape=out_shape,
        scratch_shapes=[pltpu.VMEM((tm, D_MLSTM), bf16), pltpu.VMEM((tm, D_CONV), bf16)],
        compiler_params=_params(("parallel",)),
        name="mix",
    )(x, mod_l, m_lat, m_ctx, proj, proj, proj, proj, proj, proj, proj,
      conv_w, conv_b, w_out, ln_g, ln_b, w_router, b_router)


def _route_kernel(ef_ref, pos_ref, meta_ref, *, rows):
    R = rows
    ef = ef_ref[...]
    li = lax.broadcasted_iota(i32, (128, 128), 0)
    lj = lax.broadcasted_iota(i32, (128, 128), 1)
    strict_up = (li < lj).astype(bf16)
    ri = lax.broadcasted_iota(i32, (R, R), 0)
    rj = lax.broadcasted_iota(i32, (R, R), 1)
    strict_lo = (rj < ri).astype(bf16)
    lane16 = lax.broadcasted_iota(i32, (R, N_EXPERTS), 1)

    ohs = [(ef == e) for e in range(N_EXPERTS)]
    within = [jnp.dot(oh.astype(bf16), strict_up, preferred_element_type=f32) for oh in ohs]
    rt = jnp.zeros((R, N_EXPERTS), f32)
    for e in range(N_EXPERTS):
        rt = jnp.where(lane16 == e, jnp.sum(ohs[e].astype(f32), axis=1, keepdims=True), rt)
    rp = jnp.dot(strict_lo, rt.astype(bf16), preferred_element_type=f32)
    counts = rp[R - 1:R, :] + rt[R - 1:R, :]
    ntile = jnp.floor((counts + (TM_EXP - 1)) * (1.0 / TM_EXP))
    ei = lax.broadcasted_iota(i32, (N_EXPERTS, N_EXPERTS), 0)
    ej = lax.broadcasted_iota(i32, (N_EXPERTS, N_EXPERTS), 1)
    tend = jnp.dot(ntile.astype(bf16), (ei <= ej).astype(bf16), preferred_element_type=f32)
    off = rp + (tend - ntile) * float(TM_EXP)
    pos = jnp.zeros((R, 128), f32)
    for e in range(N_EXPERTS):
        pos = jnp.where(ohs[e], within[e] + off[:, e:e + 1], pos)
    pos_ref[...] = pos.astype(i32)

    n_used = tend[:, N_EXPERTS - 1:N_EXPERTS]
    tile = jnp.minimum(lax.broadcasted_iota(i32, (1, 128), 1).astype(f32), n_used - 1.0)
    te = jnp.zeros((1, 128), f32)
    for e in range(N_EXPERTS):
        te = te + (tend[:, e:e + 1] <= tile).astype(f32)
    te = jnp.minimum(te, float(N_EXPERTS - 1))
    r8 = lax.broadcasted_iota(i32, (8, 128), 0)
    meta_ref[...] = jnp.where(r8 == 0, te, jnp.where(r8 == 1, n_used, 0.0)).astype(i32)


def _route(idx8, m_rows):
    rows = 2 * m_rows // 128
    n_tiles = 2 * m_rows // TM_EXP + N_EXPERTS
    assert n_tiles <= 128
    pos, meta = pl.pallas_call(
        functools.partial(_route_kernel, rows=rows),
        out_shape=[jax.ShapeDtypeStruct((rows, 128), i32), jax.ShapeDtypeStruct((8, 128), i32)],
        name="route",
    )(idx8[:2, :].reshape(rows, 128))
    return pos.reshape(-1), meta[0, :n_tiles], meta[1, 0:1]


def _expert_kernel(te_ref, nu_ref, pos_ref, hp_ref, w1_ref, w3_ref, w2_ref, y_ref, xbuf, sem, src_s, *,
                   m_rows):
    tm = TM_EXP
    i = pl.program_id(0)
    n_used = nu_ref[0]
    n_rows = src_s.shape[0]

    def issue(tile, slot):
        base = tile * tm
        for j in range(tm):
            t = src_s[base + j]
            pltpu.make_async_copy(hp_ref.at[pl.ds(t, 1), :], xbuf.at[slot, pl.ds(j, 1), :],
                                  sem.at[slot]).start()

    def wait(slot):
        pltpu.make_async_copy(hp_ref.at[pl.ds(0, tm), :], xbuf.at[slot], sem.at[slot]).wait()

    @pl.when(i == 0)
    def _():
        def clear(p, c):
            src_s[p] = 0
            return c

        lax.fori_loop(0, n_rows, clear, 0, unroll=16)

        def scatter(t, c):
            src_s[pos_ref[t]] = t
            src_s[pos_ref[m_rows + t]] = t
            return c

        lax.fori_loop(0, m_rows, scatter, 0, unroll=8)
        issue(0, 0)

    @pl.when(i < n_used)
    def _():
        slot = i % 2
        wait(slot)
        xb = xbuf[slot].astype(bf16)
        issue(i + 1, 1 - slot)
        a1 = jnp.dot(xb, w1_ref[...], preferred_element_type=f32)
        a3 = jnp.dot(xb, w3_ref[...], preferred_element_type=f32)
        act = (a1 * _sigmoid(a1) * a3).astype(bf16)
        y_ref[...] = jnp.dot(act, w2_ref[...], preferred_element_type=f32)

        @pl.when(i + 1 == n_used)
        def _():
            wait(1 - slot)

    @pl.when(i >= n_used)
    def _():
        y_ref[...] = jnp.zeros_like(y_ref)


def _experts(te, n_used, pos, hp, w1, w3, w2):
    tm = TM_EXP
    m_rows = hp.shape[0]
    n_tiles = te.shape[0]
    wspec = lambda r, c: pl.BlockSpec((None, r, c), lambda i, te, nu, pos: (te[i], 0, 0))
    return pl.pallas_call(
        functools.partial(_expert_kernel, m_rows=m_rows),
        grid_spec=pltpu.PrefetchScalarGridSpec(
            num_scalar_prefetch=3,
            grid=(n_tiles,),
            in_specs=[
                pl.BlockSpec(memory_space=pl.ANY),
                wspec(D_MODEL, D_FF),
                wspec(D_MODEL, D_FF),
                wspec(D_FF, D_MODEL),
            ],
            out_specs=pl.BlockSpec((tm, D_MODEL), lambda i, te, nu, pos: (i, 0)),
            scratch_shapes=[pltpu.VMEM((2, tm, D_MODEL), f32), pltpu.SemaphoreType.DMA((2,)),
                            pltpu.SMEM((n_tiles * tm,), i32)],
        ),
        out_shape=jax.ShapeDtypeStruct((n_tiles * tm, D_MODEL), f32),
        compiler_params=_params(("arbitrary",)),
        name="experts",
    )(te, n_used, pos, hp, w1, w3, w2)


def _combine_kernel(pos_ref, x_ref, mod_ref, wt_ref, g_ref, b_ref, y_ref, o_ref, buf, sem, *,
                    m_rows, nblocks):
    tm = TM_CMB
    i = pl.program_id(0)

    def issue(blk, slot):
        base = blk * tm
        for j in range(tm):
            for k in range(2):
                p = pos_ref[k * m_rows + base + j]
                pltpu.make_async_copy(y_ref.at[pl.ds(p, 1), :], buf.at[slot, k, pl.ds(j, 1), :],
                                      sem.at[slot]).start()

    @pl.when(i == 0)
    def _():
        issue(0, 0)

    slot = i % 2
    for k in range(2):
        pltpu.make_async_copy(y_ref.at[pl.ds(0, tm), :], buf.at[slot, k], sem.at[slot]).wait()

    wt = wt_ref[...]
    moe = wt[:, 0:1] * buf[slot, 0] + wt[:, 1:2] * buf[slot, 1]
    gate = mod_ref[5:6, :]
    v = ALPHA * x_ref[...] + gate * moe

    issue(jnp.minimum(i + 1, nblocks - 1), 1 - slot)
    o_ref[...] = _ln(v) * g_ref[...] + b_ref[...]

    @pl.when(i + 1 == nblocks)
    def _():
        for k in range(2):
            pltpu.make_async_copy(y_ref.at[pl.ds(0, tm), :], buf.at[1 - slot, k], sem.at[1 - slot]).wait()


def _combine(pos, x, mod_l, wts, ln_g, ln_b, y):
    tm = TM_CMB
    m_rows = x.shape[0]
    nblocks = m_rows // tm
    return pl.pallas_call(
        functools.partial(_combine_kernel, m_rows=m_rows, nblocks=nblocks),
        grid_spec=pltpu.PrefetchScalarGridSpec(
            num_scalar_prefetch=1,
            grid=(nblocks,),
            in_specs=[
                pl.BlockSpec((tm, D_MODEL), lambda i, p: (i, 0)),
                pl.BlockSpec((None, 6, D_MODEL), lambda i, p: (_mod_row(i, tm), 0, 0)),
                pl.BlockSpec((tm, 128), lambda i, p: (i, 0)),
                pl.BlockSpec((1, D_MODEL), lambda i, p: (0, 0)),
                pl.BlockSpec((1, D_MODEL), lambda i, p: (0, 0)),
                pl.BlockSpec(memory_space=pl.ANY),
            ],
            out_specs=pl.BlockSpec((tm, D_MODEL), lambda i, p: (i, 0)),
            scratch_shapes=[pltpu.VMEM((2, 2, tm, D_MODEL), f32), pltpu.SemaphoreType.DMA((2,))],
        ),
        out_shape=jax.ShapeDtypeStruct((m_rows, D_MODEL), f32),
        compiler_params=_params(("arbitrary",)),
        name="combine",
    )(pos, x, mod_l, wts, ln_g, ln_b, y)


def kernel(x, c, ctx, c_ctx, w_ada, b_ada, w_in, b_igate, b_fgate, mh_norm_g, conv_w, conv_b, w_out,
           ln1_g, ln1_b, w_router, b_router, w1, w3, w2, ln2_g, ln2_b):
    cond_raw = jnp.zeros((8, D_MODEL), f32).at[:BATCH].set(c).at[BATCH].set(c_ctx)
    mod = _ada(cond_raw, w_ada, b_ada).reshape(DEPTH, 8, 6, D_MODEL)

    xa = jnp.concatenate([x.reshape(N_LAT, D_MODEL), ctx.reshape(N_CTX, D_MODEL)], axis=0)
    w_router_p = jnp.zeros((D_MODEL, 128), f32).at[:, :N_EXPERTS].set(w_router)
    w_router_hi = w_router_p.astype(bf16)
    w_router_lo = (w_router_p - w_router_hi.astype(f32)).astype(bf16)
    w_router_p = jnp.concatenate([w_router_hi, w_router_lo], axis=1)
    b_router_c = b_router.reshape(N_EXPERTS, 1)
    w_in_t = jnp.swapaxes(w_in, 1, 2)
    w_outb = w_out.astype(bf16)

    for l in range(DEPTH):
        last = l == DEPTH - 1
        proj, gates = _inproj(xa, mod[l], w_in_t, l)

        bias = jnp.concatenate([b_igate[l], b_fgate[l]]).astype(f32)
        m_lat, m_ctx, (w1b, w3b, w2b) = _mlstm3(proj, gates, bias, mh_norm_g[l].reshape(1, D_MLSTM),
                                                w1, w3, w2, not last, l)

        xn, hp, idx8, wts = _mix(xa, mod[l], m_lat, m_ctx, proj, conv_w[l], conv_b[l].reshape(1, D_CONV),
                                 w_outb, ln1_g[l].reshape(1, D_MODEL),
                                 ln1_b[l].reshape(1, D_MODEL), w_router_p, b_router_c, not last, l)

        pos, te, n_used = _route(idx8, xn.shape[0])
        y = _experts(te, n_used, pos, hp, w1b, w3b, w2b)
        xa = _combine(pos, xn, mod[l], wts, ln2_g[l].reshape(1, D_MODEL), ln2_b[l].reshape(1, D_MODEL), y)

    return xa.reshape(BATCH, SEQ, D_MODEL)
```

```python
import functools

import jax
import jax.numpy as jnp
from jax import lax
from jax.experimental import pallas as pl
from jax.experimental.pallas import tpu as pltpu

f32 = jnp.float32
bf16 = jnp.bfloat16
i32 = jnp.int32
u32 = jnp.uint32

D_MODEL = 2048
BATCH = 4
SEQ = 2048
DEPTH = 4
GRID_W = 64
CTX_LEN = 256
D_MLSTM = 1024
HEADS = 4
DV = 256
DQK = 128
CHUNK = 64
D_CONV = 1024
D_CONV_H = 512
N_EXPERTS = 16
N_GROUPS = 4
EPG = 4
D_FF = 1024
ALPHA = (2 * DEPTH) ** 0.25
LN_EPS = 1e-6
QK_SCALE = DQK ** -0.5

N_LAT = BATCH * SEQ
N_CTX = BATCH * CTX_LEN
N_ALL = N_LAT + N_CTX
D_PROJ = 6144
N_GATE = 16
HALF = D_MODEL // 2

TM_IN = 1024
TN_IN = 512
TM_MIX = 256
TM_EXP = 256
TM_CMB = 256
VMEM_LIMIT = 56 * 1024 * 1024

HIGHEST = lax.Precision.HIGHEST


def _sigmoid(x):
    return 1.0 / (1.0 + jnp.exp(-x))


def _log_sigmoid(x):
    return jnp.minimum(x, 0.0) - jnp.log1p(jnp.exp(-jnp.abs(x)))


def _ln(x):
    mu = jnp.mean(x, axis=-1, keepdims=True)
    xc = x - mu
    var = jnp.mean(xc * xc, axis=-1, keepdims=True)
    return xc * lax.rsqrt(var + LN_EPS)


def _mod_row(i, tm):
    return jnp.minimum((i * tm) // SEQ, BATCH)


def _params(sem, vmem=VMEM_LIMIT):
    return pltpu.CompilerParams(dimension_semantics=sem, vmem_limit_bytes=vmem)


def _ada_kernel(c_ref, w_ref, b_ref, o_ref):
    c = c_ref[...]
    cond = c * _sigmoid(c)
    o_ref[...] = jnp.dot(cond, w_ref[...], preferred_element_type=f32) + b_ref[...]


def _ada(cond_raw, w_ada, b_ada):
    tn = 1024
    n = 6 * D_MODEL
    return pl.pallas_call(
        _ada_kernel,
        grid=(DEPTH, n // tn),
        in_specs=[
            pl.BlockSpec((8, D_MODEL), lambda l, j: (0, 0)),
            pl.BlockSpec((None, D_MODEL, tn), lambda l, j: (l, 0, j)),
            pl.BlockSpec((None, 1, tn), lambda l, j: (l, 0, j)),
        ],
        out_specs=pl.BlockSpec((None, 8, tn), lambda l, j: (l, 0, j)),
        out_shape=jax.ShapeDtypeStruct((DEPTH, 8, n), f32),
        compiler_params=_params(("parallel", "parallel")),
        name="ada",
    )(cond_raw, w_ada, b_ada.reshape(DEPTH, 1, n))


N_STATE = 2 * HEADS * DQK + D_MLSTM
N_STATE_BLOCKS = N_STATE // TN_IN
NT_DIMS = (((1,), (1,)), ((), ()))


def _inproj_kernel(x_ref, mod_ref, w_ref, wg_ref, o_ref, g_ref, h_scr, *, slab):
    j = pl.program_id(1)

    @pl.when(j == 0)
    def _():
        shift = mod_ref[0:1, :]
        scale = mod_ref[1:2, :]
        wg = wg_ref[...].astype(bf16)

        def body(s, c):
            r0 = pl.multiple_of(s * slab, slab)
            h = _ln(x_ref[pl.ds(r0, slab), :]) * (1.0 + scale) + shift
            hb = h.astype(bf16)
            h_scr[pl.ds(r0, slab), :] = hb
            g_ref[pl.ds(r0, slab), :] = lax.dot_general(hb, wg, NT_DIMS, preferred_element_type=f32)
            return c

        lax.fori_loop(0, TM_IN // slab, body, 0)

    o_ref[...] = lax.dot_general(h_scr[...], w_ref[0].astype(bf16), NT_DIMS,
                                 preferred_element_type=f32).astype(o_ref.dtype)


def _inproj(x, mod_l, w_in_t, l):
    m = x.shape[0]
    row0 = lambda j: pl.multiple_of(jnp.where(j < N_STATE_BLOCKS, j * TN_IN, j * TN_IN + N_GATE), 8)
    return pl.pallas_call(
        functools.partial(_inproj_kernel, slab=128),
        grid=(m // TM_IN, D_PROJ // TN_IN),
        in_specs=[
            pl.BlockSpec((TM_IN, D_MODEL), lambda i, j: (i, 0)),
            pl.BlockSpec((None, 6, D_MODEL), lambda i, j: (_mod_row(i, TM_IN), 0, 0)),
            pl.BlockSpec((pl.Element(1), pl.Element(TN_IN), pl.Element(D_MODEL)),
                         lambda i, j: (l, row0(j), 0)),
            pl.BlockSpec((None, N_GATE, D_MODEL), lambda i, j: (l, N_STATE // N_GATE, 0)),
        ],
        out_specs=[
            pl.BlockSpec((TM_IN, TN_IN), lambda i, j: (i, j)),
            pl.BlockSpec((TM_IN, N_GATE), lambda i, j: (i, 0)),
        ],
        out_shape=[
            jax.ShapeDtypeStruct((m, D_PROJ), bf16),
            jax.ShapeDtypeStruct((m, N_GATE), f32),
        ],
        scratch_shapes=[pltpu.VMEM((TM_IN, D_MODEL), bf16)],
        compiler_params=_params(("parallel", "arbitrary")),
        name="inproj",
    )(x, mod_l, w_in_t, w_in_t)


def _mlstm_chunk(q, k, v, fc_raw, ic_raw, fr_raw, ir_raw, b_i, b_f, ct_ref, n_ref, d, m, fwd):
    L = CHUNK
    lf_c = _log_sigmoid(fc_raw + b_f)
    i_c = ic_raw + b_i
    lf_r = _log_sigmoid(fr_raw + b_f)
    i_r = ir_raw + b_i
    rr = lax.broadcasted_iota(i32, (L, L), 0)
    cc = lax.broadcasted_iota(i32, (L, L), 1)
    lo = rr >= cc
    up = rr <= cc
    mask = lo if fwd else up
    a_mat = mask.astype(f32)
    a_t = (up if fwd else lo).astype(f32)
    bcol = jnp.dot(a_mat, jnp.broadcast_to(lf_c, (L, L)), precision=HIGHEST,
                   preferred_element_type=f32)
    brow = jnp.dot(jnp.broadcast_to(lf_r, (L, L)), a_t, precision=HIGHEST,
                   preferred_element_type=f32)
    dm = jnp.where(mask, bcol - brow + i_r, -jnp.inf)
    mloc = jnp.max(dm, axis=1, keepdims=True)
    wloc = jnp.exp(dm - mloc)
    b1 = bcol[:, 0:1]
    m_t = jnp.maximum(b1 + m, mloc)
    inter = jnp.exp(b1 + m - m_t)
    a = jnp.exp(mloc - m_t)
    qk = lax.dot_general(q, k, (((1,), (1,)), ((), ())), preferred_element_type=f32) * QK_SCALE
    sloc = qk * wloc
    ct = ct_ref[d]
    n = n_ref[d]
    num = inter * jnp.dot(q, ct.astype(bf16), preferred_element_type=f32) \
        + a * jnp.dot(sloc.astype(bf16), v, preferred_element_type=f32)
    den = inter * jnp.sum(q.astype(f32) * n, axis=1, keepdims=True) \
        + a * jnp.sum(sloc, axis=1, keepdims=True)
    hout = num / jnp.maximum(jnp.abs(den), jnp.exp(-m_t))
    last = L - 1 if fwd else 0
    b_last = b1[last:last + 1, :]
    wl = jnp.exp(b_last - b1 + i_c - mloc[last:last + 1, :])
    kw = k.astype(f32) * (wl * QK_SCALE)
    u = lax.dot_general(kw.astype(bf16), v, (((0,), (0,)), ((), ())), preferred_element_type=f32)
    decay = inter[last:last + 1, :]
    a_l = a[last:last + 1, :]
    ct_ref[d] = decay * ct + a_l * u
    n_ref[d] = decay * n + a_l * jnp.sum(kw, axis=0, keepdims=True)
    return hout, m_t[last:last + 1, :]


def _mlstm_kernel(bias_ref, ql_ref, kl_ref, vl_ref, ol_ref, qc_ref, kc_ref, vc_ref, oc_ref,
                  gcl_ref, grl_ref, gcc_ref, grc_ref, gain_ref, *rest, has_ctx_out):
    if has_ctx_out:
        ml_ref, mc_ref, ct_ref, n_ref, hfl, hbl, hfc, hbc = rest
    else:
        ml_ref, ct_ref, n_ref, hfl, hbl = rest
        mc_ref = hfc = hbc = None
    b = pl.program_id(0)
    h = pl.program_id(1)
    L = CHUNK
    bi = (bias_ref[h], bias_ref[HEADS + h])
    bf = (bias_ref[2 * HEADS + h], bias_ref[3 * HEADS + h])
    ct_ref[...] = jnp.zeros_like(ct_ref)
    n_ref[...] = jnp.zeros_like(n_ref)

    def run(q_ref, k_ref, v_ref, gc_ref, gr_ref, c_base, nchunks, hf, hb, carry):
        def body(j, carry):
            m_f, m_b = carry
            outs = []
            for d, fwd in ((0, True), (1, False)):
                c = j if fwd else nchunks - 1 - j
                r0 = pl.multiple_of(c * L, L)
                q = q_ref[pl.ds(r0, L), :]
                k = k_ref[pl.ds(r0, L), :]
                v = v_ref[pl.ds(r0, L), :]
                gcol = gc_ref[c_base + c]
                ic = gcol[:, d:d + 1]
                fc = gcol[:, 2 + d:3 + d]
                ir = gr_ref[d, pl.ds(c_base + c, 1), :]
                fr = gr_ref[2 + d, pl.ds(c_base + c, 1), :]
                hout, m_new = _mlstm_chunk(q, k, v, fc, ic, fr, ir, bi[d], bf[d], ct_ref, n_ref, d,
                                           m_f if fwd else m_b, fwd)
                if hf is not None:
                    (hf if fwd else hb)[pl.ds(r0, L), :] = hout
                outs.append(m_new)
            return tuple(outs)

        return lax.fori_loop(0, nchunks, body, carry)

    zero = jnp.zeros((1, 1), f32)
    carry = run(qc_ref, kc_ref, vc_ref, gcc_ref, grc_ref, b * (CTX_LEN // L), CTX_LEN // L, hfc, hbc,
                (zero, zero))
    run(ql_ref, kl_ref, vl_ref, gcl_ref, grl_ref, 0, SEQ // L, hfl, hbl, carry)

    gain = gain_ref[...]

    def finish(hf, hb, o_ref, out_ref, rows):
        slab = 256

        def ep(s, c):
            r0 = pl.multiple_of(s * slab, slab)
            hs = hf[pl.ds(r0, slab), :] + hb[pl.ds(r0, slab), :]
            r = lax.rsqrt(jnp.mean(hs * hs, axis=-1, keepdims=True) + LN_EPS)
            o = o_ref[pl.ds(r0, slab), :].astype(f32)
            out_ref[pl.ds(r0, slab), :] = (hs * r * gain * _sigmoid(o)).astype(out_ref.dtype)
            return c

        lax.fori_loop(0, rows // slab, ep, 0)

    finish(hfl, hbl, ol_ref, ml_ref, SEQ)
    if has_ctx_out:
        finish(hfc, hbc, oc_ref, mc_ref, CTX_LEN)


def _mlstm(proj, gates, bias, gain, has_ctx_out):
    L = CHUNK
    g = gates[:, :N_GATE]
    g4 = jnp.stack([g[:, 0:4], g[:, 4:8], g[:, 8:12], g[:, 12:16]], axis=-1)
    gh = jnp.transpose(g4, (1, 0, 2))
    gcol_l = gh[:, :N_LAT].reshape(HEADS, N_LAT // L, L, 4)
    gcol_c = gh[:, N_LAT:].reshape(HEADS, N_CTX // L, L, 4)
    gr = jnp.transpose(g4, (1, 2, 0))
    grow_l = gr[:, :, :N_LAT].reshape(HEADS, 4, N_LAT // L, L)
    grow_c = gr[:, :, N_LAT:].reshape(HEADS, 4, N_CTX // L, L)

    lat_rb = lambda b, h: b
    ctx_rb = lambda b, h: N_LAT // CTX_LEN + b
    in_specs = [
        pl.BlockSpec(memory_space=pltpu.SMEM),
        pl.BlockSpec((SEQ, DQK), lambda b, h: (lat_rb(b, h), h)),
        pl.BlockSpec((SEQ, DQK), lambda b, h: (lat_rb(b, h), HEADS + h)),
        pl.BlockSpec((SEQ, DV), lambda b, h: (lat_rb(b, h), HEADS + h)),
        pl.BlockSpec((SEQ, DV), lambda b, h: (lat_rb(b, h), 2 * HEADS + h)),
        pl.BlockSpec((CTX_LEN, DQK), lambda b, h: (ctx_rb(b, h), h)),
        pl.BlockSpec((CTX_LEN, DQK), lambda b, h: (ctx_rb(b, h), HEADS + h)),
        pl.BlockSpec((CTX_LEN, DV), lambda b, h: (ctx_rb(b, h), HEADS + h)),
        pl.BlockSpec((CTX_LEN, DV), lambda b, h: (ctx_rb(b, h), 2 * HEADS + h)),
        pl.BlockSpec((None, SEQ // L, L, 4), lambda b, h: (h, b, 0, 0)),
        pl.BlockSpec((None, 4, SEQ // L, L), lambda b, h: (h, 0, b, 0)),
        pl.BlockSpec((None, N_CTX // L, L, 4), lambda b, h: (h, 0, 0, 0)),
        pl.BlockSpec((None, 4, N_CTX // L, L), lambda b, h: (h, 0, 0, 0)),
        pl.BlockSpec((1, DV), lambda b, h: (0, h)),
    ]
    out_specs = [pl.BlockSpec((SEQ, DV), lambda b, h: (b, h))]
    out_shape = [jax.ShapeDtypeStruct((N_LAT, D_MLSTM), bf16)]
    scratch = [pltpu.VMEM((2, DQK, DV), f32), pltpu.VMEM((2, 1, DQK), f32),
               pltpu.VMEM((SEQ, DV), f32), pltpu.VMEM((SEQ, DV), f32)]
    if has_ctx_out:
        out_specs.append(pl.BlockSpec((CTX_LEN, DV), lambda b, h: (b, h)))
        out_shape.append(jax.ShapeDtypeStruct((N_CTX, D_MLSTM), bf16))
        scratch += [pltpu.VMEM((CTX_LEN, DV), f32), pltpu.VMEM((CTX_LEN, DV), f32)]
    res = pl.pallas_call(
        functools.partial(_mlstm_kernel, has_ctx_out=has_ctx_out),
        grid=(BATCH, HEADS),
        in_specs=in_specs,
        out_specs=out_specs,
        out_shape=out_shape,
        scratch_shapes=scratch,
        compiler_params=_params(("parallel", "parallel")),
        name="mlstm",
    )(bias, proj, proj, proj, proj, proj, proj, proj, proj, gcol_l, grow_l, gcol_c, grow_c, gain)
    return res if has_ctx_out else (res[0], None)


GROUP = 4
N_CHUNK_CTX = CTX_LEN // CHUNK
N_CHUNK_LAT = SEQ // CHUNK
N_CHUNK = N_CHUNK_CTX + N_CHUNK_LAT
N_GROUP_LAT = N_CHUNK_LAT // GROUP
LAT_ROW0 = 8
D_AUG = DV + 128


CVT_STEPS = 2 * (SEQ // CHUNK // 4)
CVT_ROWS_IN = D_MODEL // CVT_STEPS
CVT_ROWS_OUT = D_FF // CVT_STEPS


def _mlstm3_kernel(bias_ref, ql_ref, kl_ref, vl_ref, ol_ref, qc_ref, kc_ref, vc_ref, oc_ref,
                   gcl_ref, grl_ref, gcc_ref, grc_ref, gain_ref, w1_hbm, w3_hbm, w2_hbm, *rest,
                   has_ctx_out, layer):
    if has_ctx_out:
        ml_ref, mc_ref = rest[:2]
        rest = rest[2:]
    else:
        ml_ref, mc_ref = rest[0], None
        rest = rest[1:]
    w1b_hbm, w3b_hbm, w2b_hbm, s_ref, st_ref, msc, rows_s, cols_s = rest[:8]
    cin = rest[8:11]
    cout = rest[11:14]
    csem_in, csem_out = rest[14:16]
    L = CHUNK
    b = pl.program_id(0)
    h = pl.program_id(1)
    expert = b * HEADS + h
    cvt_src = (w1_hbm, w3_hbm, w2_hbm)
    cvt_dst = (w1b_hbm, w3b_hbm, w2b_hbm)
    cvt_rows = (CVT_ROWS_IN, CVT_ROWS_IN, CVT_ROWS_OUT)

    def cvt_in(k, t):
        r0 = pl.multiple_of(t * cvt_rows[k], cvt_rows[k])
        return pltpu.make_async_copy(cvt_src[k].at[layer, expert, pl.ds(r0, cvt_rows[k]), :],
                                     cin[k].at[t % 2], csem_in.at[k, t % 2])

    def cvt_out(k, t):
        r0 = pl.multiple_of(t * cvt_rows[k], cvt_rows[k])
        return pltpu.make_async_copy(cout[k].at[t % 2], cvt_dst[k].at[expert, pl.ds(r0, cvt_rows[k]), :],
                                     csem_out.at[k, t % 2])

    def cvt_step(t):
        for k in range(3):
            cvt_in(k, t).wait()

            @pl.when(t >= 2)
            def _():
                cvt_out(k, t - 2).wait()

            cout[k][t % 2] = cin[k][t % 2].astype(bf16)
            cvt_out(k, t).start()

            @pl.when(t + 2 < CVT_STEPS)
            def _():
                cvt_in(k, t + 2).start()

    for k in range(3):
        cvt_in(k, 0).start()
        cvt_in(k, 1).start()
    bi = (bias_ref[h], bias_ref[HEADS + h])
    bf = (bias_ref[2 * HEADS + h], bias_ref[3 * HEADS + h])

    rr = lax.broadcasted_iota(i32, (L, L), 0)
    cc = lax.broadcasted_iota(i32, (L, L), 1)
    lo_mask = rr >= cc
    up_mask = rr <= cc
    masks = (lo_mask, up_mask)

    def row_forms(gr_ref, sl, dst0, n):
        for d in range(2):
            tri = (up_mask if d == 0 else lo_mask).astype(f32)
            i_r = gr_ref[d, sl, :] + bi[d]
            lf_r = _log_sigmoid(gr_ref[2 + d, sl, :] + bf[d])
            b_r = jnp.dot(lf_r, tri, precision=HIGHEST, preferred_element_type=f32)
            rows_s[d, dst0:dst0 + n, :] = i_r - b_r

    row_forms(grc_ref, pl.ds(b * N_CHUNK_CTX, N_CHUNK_CTX), 0, N_CHUNK_CTX)
    row_forms(grl_ref, slice(None), LAT_ROW0, N_CHUNK_LAT)

    kind = lax.broadcasted_iota(i32, (1, 4 * GROUP), 1) % 4
    bias_v = jnp.where(kind == 0, bi[0], jnp.where(kind == 1, bi[1], jnp.where(kind == 2, bf[0], bf[1])))
    tpos = lax.broadcasted_iota(i32, (L, 4 * GROUP), 0)

    def col_forms(x):
        y = x + bias_v
        y = jnp.where(kind >= 2, _log_sigmoid(y), y)
        pre = y
        suf = y
        s = 1
        while s < L:
            pre = pre + jnp.where(tpos >= s, pltpu.roll(pre, s, 0), 0.0)
            suf = suf + jnp.where(tpos < L - s, pltpu.roll(suf, L - s, 0), 0.0)
            s *= 2
        return jnp.where(kind == 2, pre, jnp.where(kind == 3, suf, y))

    cols_s[0] = col_forms(gcc_ref[...])

    def col_body(g, c):
        cols_s[g + 1] = col_forms(gcl_ref[g])
        return c

    lax.fori_loop(0, N_GROUP_LAT, col_body, 0)

    ones_col = (lax.broadcasted_iota(i32, (L, 128), 1) == 0).astype(bf16)

    s_ref[...] = jnp.zeros_like(s_ref)

    def state_step(d, c, row, col, j, k, v, m):
        i_c = col[:, 4 * j + d:4 * j + d + 1]
        b1 = col[:, 4 * j + 2 + d:4 * j + 3 + d]
        last = L - 1 if d == 0 else 0
        b_last = b1[last:last + 1, :]
        ct = i_c - b1
        mx = jnp.max(ct, axis=0, keepdims=True)
        wl = jnp.exp(ct - mx)
        mloc_last = b_last + mx
        m_new = jnp.maximum(b_last + m, mloc_last)
        decay = jnp.exp(b_last + m - m_new)
        a_l = jnp.exp(mloc_last - m_new)
        s_old = s_ref[d]
        st_ref[d, c] = s_old.astype(bf16)
        msc[d, pl.ds(row, 1), :] = jnp.broadcast_to(m, (1, 128))
        kw = (k.astype(f32) * (wl * QK_SCALE)).astype(bf16)
        vaug = jnp.concatenate([v, ones_col], axis=1)
        u = lax.dot_general(kw, vaug, (((0,), (0,)), ((), ())), preferred_element_type=f32)
        s_ref[d] = decay * s_old + a_l * u
        return m_new

    zero = jnp.zeros((1, 1), f32)
    m_f = m_b = zero
    col0 = cols_s[0]
    for step in range(GROUP):
        jf, jb = step, GROUP - 1 - step
        m_f = state_step(0, jf, jf, col0, jf, kc_ref[jf * L:(jf + 1) * L, :], vc_ref[jf * L:(jf + 1) * L, :], m_f)
        m_b = state_step(1, jb, jb, col0, jb, kc_ref[jb * L:(jb + 1) * L, :], vc_ref[jb * L:(jb + 1) * L, :], m_b)

    def state_body(it, carry):
        m_f, m_b = carry
        gf = it
        gb = N_GROUP_LAT + 1 - it
        colf = cols_s[gf]
        colb = cols_s[gb]
        for step in range(GROUP):
            jf, jb = step, GROUP - 1 - step
            clf = (gf - 1) * GROUP + jf
            clb = (gb - 1) * GROUP + jb
            rf = pl.multiple_of(clf * L, L)
            rb = pl.multiple_of(clb * L, L)
            m_f = state_step(0, N_CHUNK_CTX + clf, LAT_ROW0 + clf, colf, jf,
                             kl_ref[pl.ds(rf, L), :], vl_ref[pl.ds(rf, L), :], m_f)
            m_b = state_step(1, N_CHUNK_CTX + clb, LAT_ROW0 + clb, colb, jb,
                             kl_ref[pl.ds(rb, L), :], vl_ref[pl.ds(rb, L), :], m_b)
        cvt_step(it - 1)
        return m_f, m_b

    lax.fori_loop(1, N_GROUP_LAT + 1, state_body, (m_f, m_b))

    gain = gain_ref[...]

    def out_chunk(q, k, v, o, col, j, c, row, out_ref, r0):
        qk = lax.dot_general(q, k, (((1,), (1,)), ((), ())), preferred_element_type=f32) * QK_SCALE
        vaug = jnp.concatenate([v, ones_col], axis=1)
        sl = []
        per = []
        for d in range(2):
            b1 = col[:, 4 * j + 2 + d:4 * j + 3 + d]
            dm = jnp.where(masks[d], b1 + rows_s[d, pl.ds(row, 1), :], -jnp.inf)
            mloc = jnp.max(dm, axis=1, keepdims=True)
            wloc = jnp.exp(dm - mloc)
            m_prev = msc[d, pl.ds(row, 1), :][:, 0:1]
            m_t = jnp.maximum(b1 + m_prev, mloc)
            inter = jnp.exp(b1 + m_prev - m_t)
            a = jnp.exp(mloc - m_t)
            sl.append((qk * wloc).astype(bf16))
            per.append((m_t, inter, a))
        x = jnp.dot(jnp.concatenate(sl, axis=0), vaug, preferred_element_type=f32)
        hs = None
        for d in range(2):
            y = jnp.dot(q, st_ref[d, c], preferred_element_type=f32)
            m_t, inter, a = per[d]
            xd = x[d * L:(d + 1) * L, :]
            den = inter * y[:, DV:DV + 1] + a * xd[:, DV:DV + 1]
            rinv = 1.0 / jnp.maximum(jnp.abs(den), jnp.exp(-m_t))
            hd = (inter * rinv) * y[:, :DV] + (a * rinv) * xd[:, :DV]
            hs = hd if hs is None else hs + hd
        r = lax.rsqrt(jnp.mean(hs * hs, axis=-1, keepdims=True) + LN_EPS)
        out_ref[pl.ds(r0, L), :] = (hs * r * gain * _sigmoid(o.astype(f32))).astype(out_ref.dtype)

    if has_ctx_out:
        for j in range(GROUP):
            sl_ = slice(j * L, (j + 1) * L)
            out_chunk(qc_ref[sl_, :], kc_ref[sl_, :], vc_ref[sl_, :], oc_ref[sl_, :], col0, j, j, j,
                      mc_ref, j * L)

    def out_body(g, carry):
        col = cols_s[g + 1]
        for j in range(GROUP):
            cl = g * GROUP + j
            r0 = pl.multiple_of(cl * L, L)
            out_chunk(ql_ref[pl.ds(r0, L), :], kl_ref[pl.ds(r0, L), :], vl_ref[pl.ds(r0, L), :],
                      ol_ref[pl.ds(r0, L), :], col, j, N_CHUNK_CTX + cl, LAT_ROW0 + cl, ml_ref, r0)
        cvt_step(N_GROUP_LAT + g)
        return carry

    lax.fori_loop(0, N_GROUP_LAT, out_body, 0)
    for k in range(3):
        cvt_out(k, CVT_STEPS - 2).wait()
        cvt_out(k, CVT_STEPS - 1).wait()


def _mlstm3(proj, gates, bias, gain, w1, w3, w2, has_ctx_out, layer):
    L = CHUNK
    g = gates[:, :N_GATE]
    g4 = jnp.stack([g[:, 0:4], g[:, 4:8], g[:, 8:12], g[:, 12:16]], axis=-1)
    gh = jnp.transpose(g4, (1, 0, 2))
    gcol_l = gh[:, :N_LAT].reshape(HEADS, BATCH * N_GROUP_LAT, GROUP, L, 4)
    gcol_l = jnp.transpose(gcol_l, (0, 1, 3, 2, 4)).reshape(HEADS, BATCH * N_GROUP_LAT, L, 4 * GROUP)
    gcol_c = gh[:, N_LAT:].reshape(HEADS, BATCH, GROUP, L, 4)
    gcol_c = jnp.transpose(gcol_c, (0, 1, 3, 2, 4)).reshape(HEADS, BATCH, L, 4 * GROUP)
    gr = jnp.transpose(g4, (1, 2, 0))
    grow_l = gr[:, :, :N_LAT].reshape(HEADS, 4, N_LAT // L, L)
    grow_c = gr[:, :, N_LAT:].reshape(HEADS, 4, N_CTX // L, L)

    ctx_rb = lambda b, h: N_LAT // CTX_LEN + b
    in_specs = [
        pl.BlockSpec(memory_space=pltpu.SMEM),
        pl.BlockSpec((SEQ, DQK), lambda b, h: (b, h)),
        pl.BlockSpec((SEQ, DQK), lambda b, h: (b, HEADS + h)),
        pl.BlockSpec((SEQ, DV), lambda b, h: (b, HEADS + h)),
        pl.BlockSpec((SEQ, DV), lambda b, h: (b, 2 * HEADS + h)),
        pl.BlockSpec((CTX_LEN, DQK), lambda b, h: (ctx_rb(b, h), h)),
        pl.BlockSpec((CTX_LEN, DQK), lambda b, h: (ctx_rb(b, h), HEADS + h)),
        pl.BlockSpec((CTX_LEN, DV), lambda b, h: (ctx_rb(b, h), HEADS + h)),
        pl.BlockSpec((CTX_LEN, DV), lambda b, h: (ctx_rb(b, h), 2 * HEADS + h)),
        pl.BlockSpec((None, N_GROUP_LAT, L, 4 * GROUP), lambda b, h: (h, b, 0, 0)),
        pl.BlockSpec((None, 4, N_CHUNK_LAT, L), lambda b, h: (h, 0, b, 0)),
        pl.BlockSpec((None, None, L, 4 * GROUP), lambda b, h: (h, b, 0, 0)),
        pl.BlockSpec((None, 4, N_CTX // L, L), lambda b, h: (h, 0, 0, 0)),
        pl.BlockSpec((1, DV), lambda b, h: (0, h)),
        pl.BlockSpec(memory_space=pl.ANY),
        pl.BlockSpec(memory_space=pl.ANY),
        pl.BlockSpec(memory_space=pl.ANY),
    ]
    out_specs = [pl.BlockSpec((SEQ, DV), lambda b, h: (b, h))]
    out_shape = [jax.ShapeDtypeStruct((N_LAT, D_MLSTM), bf16)]
    if has_ctx_out:
        out_specs.append(pl.BlockSpec((CTX_LEN, DV), lambda b, h: (b, h)))
        out_shape.append(jax.ShapeDtypeStruct((N_CTX, D_MLSTM), bf16))
    out_specs += [pl.BlockSpec(memory_space=pl.ANY)] * 3
    out_shape += [jax.ShapeDtypeStruct((N_EXPERTS, D_MODEL, D_FF), bf16),
                  jax.ShapeDtypeStruct((N_EXPERTS, D_MODEL, D_FF), bf16),
                  jax.ShapeDtypeStruct((N_EXPERTS, D_FF, D_MODEL), bf16)]
    scratch = [
        pltpu.VMEM((2, DQK, D_AUG), f32),
        pltpu.VMEM((2, N_CHUNK, DQK, D_AUG), bf16),
        pltpu.VMEM((2, LAT_ROW0 + N_CHUNK_LAT, 128), f32),
        pltpu.VMEM((2, LAT_ROW0 + N_CHUNK_LAT, L), f32),
        pltpu.VMEM((N_GROUP_LAT + 1, L, 4 * GROUP), f32),
        pltpu.VMEM((2, CVT_ROWS_IN, D_FF), f32),
        pltpu.VMEM((2, CVT_ROWS_IN, D_FF), f32),
        pltpu.VMEM((2, CVT_ROWS_OUT, D_MODEL), f32),
        pltpu.VMEM((2, CVT_ROWS_IN, D_FF), bf16),
        pltpu.VMEM((2, CVT_ROWS_IN, D_FF), bf16),
        pltpu.VMEM((2, CVT_ROWS_OUT, D_MODEL), bf16),
        pltpu.SemaphoreType.DMA((3, 2)),
        pltpu.SemaphoreType.DMA((3, 2)),
    ]
    assert N_EXPERTS == BATCH * HEADS
    res = pl.pallas_call(
        functools.partial(_mlstm3_kernel, has_ctx_out=has_ctx_out, layer=layer),
        grid=(BATCH, HEADS),
        in_specs=in_specs,
        out_specs=out_specs,
        out_shape=out_shape,
        scratch_shapes=scratch,
        compiler_params=_params(("arbitrary", "arbitrary")),
        name="mlstm",
    )(bias, proj, proj, proj, proj, proj, proj, proj, proj, gcol_l, grow_l, gcol_c, grow_c, gain,
      w1, w3, w2)
    if has_ctx_out:
        return res[0], res[1], res[2:]
    return res[0], None, res[1:]


def _top2_rows(vals):
    best = vals[0]
    bi = jnp.zeros(best.shape, i32)
    for j in range(1, len(vals)):
        take = vals[j] > best
        best = jnp.where(take, vals[j], best)
        bi = jnp.where(take, j, bi)
    sec = None
    si = None
    for j in range(len(vals)):
        cand = jnp.where(bi == j, -jnp.inf, vals[j])
        if sec is None:
            sec, si = cand, jnp.zeros(best.shape, i32)
        else:
            take = cand > sec
            sec = jnp.where(take, cand, sec)
            si = jnp.where(take, j, si)
    return bi, si


def _mix_kernel(x_ref, mod_ref, ml_ref, mc_ref, u_ref, bg_ref, cg_ref, ut_ref, ct_ref, ub_ref, cb_ref,
                cw_ref, cbias_ref, wo_ref, g1_ref, b1_ref, wr_ref, br_ref,
                xo_ref, hp_ref, idx_ref, wt_ref, m_scr, y_scr, *, n_lat_blocks, has_ctx):
    tm = TM_MIX
    i = pl.program_id(0)
    gate = mod_ref[2:3, :]
    shift2 = mod_ref[3:4, :]
    scale2 = mod_ref[4:5, :]
    cw = cw_ref[...]
    cbias = cbias_ref[...]
    row = lax.broadcasted_iota(i32, (tm, 1), 0)

    def shifted(z, first, last):
        prev = jnp.where(first, 0.0, pltpu.roll(z, 1, 0))
        nxt = jnp.where(last, 0.0, pltpu.roll(z, tm - 1, 0))
        return prev, nxt

    def lat_branch():
        z = cg_ref[...].astype(f32) * u_ref[...].astype(f32)
        col = row % GRID_W
        zh = z[:, :D_CONV_H]
        prev, nxt = shifted(zh, col == 0, col == GRID_W - 1)
        yh = cw[0:1, :D_CONV_H] * prev + cw[1:2, :D_CONV_H] * zh + cw[2:3, :D_CONV_H] * nxt
        bpb = SEQ // tm
        top_ok = (i % bpb != 0).astype(f32)
        bot_ok = (i % bpb != bpb - 1).astype(f32)
        zt = ct_ref[...].astype(f32) * ut_ref[...].astype(f32) * top_ok
        zb = cb_ref[...].astype(f32) * ub_ref[...].astype(f32) * bot_ok
        zv = z[:, D_CONV_H:]
        zext = jnp.concatenate([zt, zv, zb], axis=0)
        yv = cw[0:1, D_CONV_H:] * zext[0:tm] + cw[1:2, D_CONV_H:] * zv \
            + cw[2:3, D_CONV_H:] * zext[2 * GRID_W:2 * GRID_W + tm]
        y = jnp.concatenate([yh, yv], axis=1) + cbias
        y_scr[...] = (bg_ref[...].astype(f32) * y).astype(bf16)
        m_scr[...] = ml_ref[...]

    def ctx_branch():
        z = cg_ref[...].astype(f32) * u_ref[...].astype(f32)
        pos = row % CTX_LEN
        prev, nxt = shifted(z, pos == 0, pos == CTX_LEN - 1)
        y = cw[0:1, :] * prev + cw[1:2, :] * z + cw[2:3, :] * nxt + cbias
        y_scr[...] = (bg_ref[...].astype(f32) * y).astype(bf16)
        m_scr[...] = mc_ref[...]

    if has_ctx:
        pl.when(i < n_lat_blocks)(lat_branch)
        pl.when(i >= n_lat_blocks)(ctx_branch)
    else:
        lat_branch()

    out = jnp.dot(m_scr[...], wo_ref[0:D_MLSTM, :], preferred_element_type=f32) \
        + jnp.dot(y_scr[...], wo_ref[D_MLSTM:, :], preferred_element_type=f32)
    xn = _ln(ALPHA * x_ref[...] + gate * out) * g1_ref[...] + b1_ref[...]
    xo_ref[...] = xn
    h2 = _ln(xn) * (1.0 + scale2) + shift2

    hp_ref[...] = h2

    h_hi = h2.astype(bf16)
    hs = jnp.concatenate([h_hi, (h2 - h_hi.astype(f32)).astype(bf16)], axis=0)
    pr = jnp.dot(hs, wr_ref[...], preferred_element_type=f32)
    logits = pr[:tm, :128] + (pr[:tm, 128:] + pr[tm:, :128])
    lt = logits.T
    s = _sigmoid(lt[0:N_EXPERTS, :])
    sb = s + br_ref[...]
    sb_rows = [sb[e:e + 1, :] for e in range(N_EXPERTS)]
    s_rows = [s[e:e + 1, :] for e in range(N_EXPERTS)]
    gscores = []
    for g in range(N_GROUPS):
        a_, b_, c_, d_ = sb_rows[EPG * g:EPG * g + EPG]
        hi1, lo1 = jnp.maximum(a_, b_), jnp.minimum(a_, b_)
        hi2, lo2 = jnp.maximum(c_, d_), jnp.minimum(c_, d_)
        top = jnp.maximum(hi1, hi2)
        second = jnp.maximum(jnp.minimum(hi1, hi2), jnp.maximum(lo1, lo2))
        gscores.append(top + second)
    gbest = gscores[0]
    gsel = jnp.zeros(gbest.shape, i32)
    for g in range(1, N_GROUPS):
        take = gscores[g] > gbest
        gbest = jnp.where(take, gscores[g], gbest)
        gsel = jnp.where(take, g, gsel)

    def pick_group(rows, j):
        v = rows[j]
        for g in range(1, N_GROUPS):
            v = jnp.where(gsel == g, rows[EPG * g + j], v)
        return v

    cand_b = [pick_group(sb_rows, j) for j in range(EPG)]
    cand_s = [pick_group(s_rows, j) for j in range(EPG)]
    i1, i2 = _top2_rows(cand_b)

    def pick_idx(rows, idx):
        v = rows[0]
        for j in range(1, EPG):
            v = jnp.where(idx == j, rows[j], v)
        return v

    s1 = pick_idx(cand_s, i1)
    s2 = pick_idx(cand_s, i2)
    tot = s1 + s2
    w1 = s1 / tot
    w2 = s2 / tot
    e1 = gsel * EPG + i1
    e2 = gsel * EPG + i2
    r8 = lax.broadcasted_iota(i32, (8, tm), 0)
    idx_ref[...] = jnp.where(r8 == 0, e1, jnp.where(r8 == 1, e2, 0))
    r128 = lax.broadcasted_iota(i32, (128, tm), 0)
    wmat = jnp.where(r128 == 0, w1, jnp.where(r128 == 1, w2, 0.0))
    wt_ref[...] = wmat.T


def _mix(x, mod_l, m_lat, m_ctx, proj, conv_w, conv_b, w_out, ln_g, ln_b, w_router, b_router, has_ctx, l):
    tm = TM_MIX
    m_rows = N_ALL if has_ctx else N_LAT
    nlb = N_LAT // tm
    nblocks = m_rows // tm
    hb = tm // GRID_W
    n_hblocks = N_ALL // GRID_W
    if m_ctx is None:
        m_ctx = m_lat
    ncb = m_ctx.shape[0] // tm
    in_specs = [
        pl.BlockSpec((tm, D_MODEL), lambda i: (i, 0)),
        pl.BlockSpec((None, 6, D_MODEL), lambda i: (_mod_row(i, tm), 0, 0)),
        pl.BlockSpec((tm, D_MLSTM), lambda i: (jnp.minimum(i, nlb - 1), 0)),
        pl.BlockSpec((tm, D_MLSTM), lambda i: (jnp.clip(i - nlb, 0, ncb - 1), 0)),
        pl.BlockSpec((tm, D_CONV), lambda i: (i, 3)),
        pl.BlockSpec((tm, D_CONV), lambda i: (i, 4)),
        pl.BlockSpec((tm, D_CONV), lambda i: (i, 5)),
        pl.BlockSpec((GRID_W, D_CONV_H), lambda i: (jnp.maximum(i * hb - 1, 0), 7)),
        pl.BlockSpec((GRID_W, D_CONV_H), lambda i: (jnp.maximum(i * hb - 1, 0), 11)),
        pl.BlockSpec((GRID_W, D_CONV_H), lambda i: (jnp.minimum((i + 1) * hb, n_hblocks - 1), 7)),
        pl.BlockSpec((GRID_W, D_CONV_H), lambda i: (jnp.minimum((i + 1) * hb, n_hblocks - 1), 11)),
        pl.BlockSpec((3, D_CONV), lambda i: (0, 0)),
        pl.BlockSpec((1, D_CONV), lambda i: (0, 0)),
        pl.BlockSpec((None, D_MODEL, D_MODEL), lambda i: (l, 0, 0)),
        pl.BlockSpec((1, D_MODEL), lambda i: (0, 0)),
        pl.BlockSpec((1, D_MODEL), lambda i: (0, 0)),
        pl.BlockSpec((D_MODEL, 256), lambda i: (0, 0)),
        pl.BlockSpec((N_EXPERTS, 1), lambda i: (0, 0)),
    ]
    out_specs = [
        pl.BlockSpec((tm, D_MODEL), lambda i: (i, 0)),
        pl.BlockSpec((tm, D_MODEL), lambda i: (i, 0)),
        pl.BlockSpec((8, tm), lambda i: (0, i)),
        pl.BlockSpec((tm, 128), lambda i: (i, 0)),
    ]
    out_shape = [
        jax.ShapeDtypeStruct((m_rows, D_MODEL), f32),
        jax.ShapeDtypeStruct((m_rows, D_MODEL), f32),
        jax.ShapeDtypeStruct((8, m_rows), i32),
        jax.ShapeDtypeStruct((m_rows, 128), f32),
    ]
    return pl.pallas_call(
        functools.partial(_mix_kernel, n_lat_blocks=nlb, has_ctx=has_ctx),
        grid=(nblocks,),
        in_specs=in_specs,
        out_specs=out_specs,
        out_shape=out_shape,
        scratch_shapes=[pltpu.VMEM((tm, D_MLSTM), bf16), pltpu.VMEM((tm, D_CONV), bf16)],
        compiler_params=_params(("parallel",)),
        name="mix",
    )(x, mod_l, m_lat, m_ctx, proj, proj, proj, proj, proj, proj, proj,
      conv_w, conv_b, w_out, ln_g, ln_b, w_router, b_router)


def _route_kernel(ef_ref, pos_ref, meta_ref, *, rows):
    R = rows
    ef = ef_ref[...]
    li = lax.broadcasted_iota(i32, (128, 128), 0)
    lj = lax.broadcasted_iota(i32, (128, 128), 1)
    strict_up = (li < lj).astype(bf16)
    ri = lax.broadcasted_iota(i32, (R, R), 0)
    rj = lax.broadcasted_iota(i32, (R, R), 1)
    strict_lo = (rj < ri).astype(bf16)
    lane16 = lax.broadcasted_iota(i32, (R, N_EXPERTS), 1)

    ohs = [(ef == e) for e in range(N_EXPERTS)]
    within = [jnp.dot(oh.astype(bf16), strict_up, preferred_element_type=f32) for oh in ohs]
    rt = jnp.zeros((R, N_EXPERTS), f32)
    for e in range(N_EXPERTS):
        rt = jnp.where(lane16 == e, jnp.sum(ohs[e].astype(f32), axis=1, keepdims=True), rt)
    rp = jnp.dot(strict_lo, rt.astype(bf16), preferred_element_type=f32)
    counts = rp[R - 1:R, :] + rt[R - 1:R, :]
    ntile = jnp.floor((counts + (TM_EXP - 1)) * (1.0 / TM_EXP))
    ei = lax.broadcasted_iota(i32, (N_EXPERTS, N_EXPERTS), 0)
    ej = lax.broadcasted_iota(i32, (N_EXPERTS, N_EXPERTS), 1)
    tend = jnp.dot(ntile.astype(bf16), (ei <= ej).astype(bf16), preferred_element_type=f32)
    off = rp + (tend - ntile) * float(TM_EXP)
    pos = jnp.zeros((R, 128), f32)
    for e in range(N_EXPERTS):
        pos = jnp.where(ohs[e], within[e] + off[:, e:e + 1], pos)
    pos_ref[...] = pos.astype(i32)

    n_used = tend[:, N_EXPERTS - 1:N_EXPERTS]
    tile = jnp.minimum(lax.broadcasted_iota(i32, (1, 128), 1).astype(f32), n_used - 1.0)
    te = jnp.zeros((1, 128), f32)
    for e in range(N_EXPERTS):
        te = te + (tend[:, e:e + 1] <= tile).astype(f32)
    te = jnp.minimum(te, float(N_EXPERTS - 1))
    r8 = lax.broadcasted_iota(i32, (8, 128), 0)
    meta_ref[...] = jnp.where(r8 == 0, te, jnp.where(r8 == 1, n_used, 0.0)).astype(i32)


def _route(idx8, m_rows):
    rows = 2 * m_rows // 128
    n_tiles = 2 * m_rows // TM_EXP + N_EXPERTS
    assert n_tiles <= 128
    pos, meta = pl.pallas_call(
        functools.partial(_route_kernel, rows=rows),
        out_shape=[jax.ShapeDtypeStruct((rows, 128), i32), jax.ShapeDtypeStruct((8, 128), i32)],
        name="route",
    )(idx8[:2, :].reshape(rows, 128))
    return pos.reshape(-1), meta[0, :n_tiles], meta[1, 0:1]


def _expert_kernel(te_ref, nu_ref, pos_ref, hp_ref, w1_ref, w3_ref, w2_ref, y_ref, xbuf, sem, src_s, *,
                   m_rows):
    tm = TM_EXP
    i = pl.program_id(0)
    n_used = nu_ref[0]
    n_rows = src_s.shape[0]

    def issue(tile, slot):
        base = tile * tm
        for j in range(tm):
            t = src_s[base + j]
            pltpu.make_async_copy(hp_ref.at[pl.ds(t, 1), :], xbuf.at[slot, pl.ds(j, 1), :],
                                  sem.at[slot]).start()

    def wait(slot):
        pltpu.make_async_copy(hp_ref.at[pl.ds(0, tm), :], xbuf.at[slot], sem.at[slot]).wait()

    @pl.when(i == 0)
    def _():
        def clear(p, c):
            src_s[p] = 0
            return c

        lax.fori_loop(0, n_rows, clear, 0, unroll=16)

        def scatter(t, c):
            src_s[pos_ref[t]] = t
            src_s[pos_ref[m_rows + t]] = t
            return c

        lax.fori_loop(0, m_rows, scatter, 0, unroll=8)
        issue(0, 0)

    @pl.when(i < n_used)
    def _():
        slot = i % 2
        issue(i + 1, 1 - slot)
        wait(slot)
        xb = xbuf[slot].astype(bf16)
        a1 = jnp.dot(xb, w1_ref[...], preferred_element_type=f32)
        a3 = jnp.dot(xb, w3_ref[...], preferred_element_type=f32)
        act = (a1 * _sigmoid(a1) * a3).astype(bf16)
        y_ref[...] = jnp.dot(act, w2_ref[...], preferred_element_type=f32)

        @pl.when(i + 1 == n_used)
        def _():
            wait(1 - slot)

    @pl.when(i >= n_used)
    def _():
        y_ref[...] = jnp.zeros_like(y_ref)


def _experts(te, n_used, pos, hp, w1, w3, w2):
    tm = TM_EXP
    m_rows = hp.shape[0]
    n_tiles = te.shape[0]
    wspec = lambda r, c: pl.BlockSpec((None, r, c), lambda i, te, nu, pos: (te[i], 0, 0))
    return pl.pallas_call(
        functools.partial(_expert_kernel, m_rows=m_rows),
        grid_spec=pltpu.PrefetchScalarGridSpec(
            num_scalar_prefetch=3,
            grid=(n_tiles,),
            in_specs=[
                pl.BlockSpec(memory_space=pl.ANY),
                wspec(D_MODEL, D_FF),
                wspec(D_MODEL, D_FF),
                wspec(D_FF, D_MODEL),
            ],
            out_specs=pl.BlockSpec((tm, D_MODEL), lambda i, te, nu, pos: (i, 0)),
            scratch_shapes=[pltpu.VMEM((2, tm, D_MODEL), f32), pltpu.SemaphoreType.DMA((2,)),
                            pltpu.SMEM((n_tiles * tm,), i32)],
        ),
        out_shape=jax.ShapeDtypeStruct((n_tiles * tm, D_MODEL), f32),
        compiler_params=_params(("arbitrary",)),
        name="experts",
    )(te, n_used, pos, hp, w1, w3, w2)


def _combine_kernel(pos_ref, x_ref, mod_ref, wt_ref, g_ref, b_ref, y_ref, o_ref, buf, sem, *,
                    m_rows, nblocks):
    tm = TM_CMB
    i = pl.program_id(0)

    def issue(blk, slot):
        base = blk * tm
        for j in range(tm):
            for k in range(2):
                p = pos_ref[k * m_rows + base + j]
                pltpu.make_async_copy(y_ref.at[pl.ds(p, 1), :], buf.at[slot, k, pl.ds(j, 1), :],
                                      sem.at[slot]).start()

    @pl.when(i == 0)
    def _():
        issue(0, 0)

    slot = i % 2
    for k in range(2):
        pltpu.make_async_copy(y_ref.at[pl.ds(0, tm), :], buf.at[slot, k], sem.at[slot]).wait()

    wt = wt_ref[...]
    moe = wt[:, 0:1] * buf[slot, 0] + wt[:, 1:2] * buf[slot, 1]
    gate = mod_ref[5:6, :]
    v = ALPHA * x_ref[...] + gate * moe

    issue(jnp.minimum(i + 1, nblocks - 1), 1 - slot)
    o_ref[...] = _ln(v) * g_ref[...] + b_ref[...]

    @pl.when(i + 1 == nblocks)
    def _():
        for k in range(2):
            pltpu.make_async_copy(y_ref.at[pl.ds(0, tm), :], buf.at[1 - slot, k], sem.at[1 - slot]).wait()


def _combine(pos, x, mod_l, wts, ln_g, ln_b, y):
    tm = TM_CMB
    m_rows = x.shape[0]
    nblocks = m_rows // tm
    return pl.pallas_call(
        functools.partial(_combine_kernel, m_rows=m_rows, nblocks=nblocks),
        grid_spec=pltpu.PrefetchScalarGridSpec(
            num_scalar_prefetch=1,
            grid=(nblocks,),
            in_specs=[
                pl.BlockSpec((tm, D_MODEL), lambda i, p: (i, 0)),
                pl.BlockSpec((None, 6, D_MODEL), lambda i, p: (_mod_row(i, tm), 0, 0)),
                pl.BlockSpec((tm, 128), lambda i, p: (i, 0)),
                pl.BlockSpec((1, D_MODEL), lambda i, p: (0, 0)),
                pl.BlockSpec((1, D_MODEL), lambda i, p: (0, 0)),
                pl.BlockSpec(memory_space=pl.ANY),
            ],
            out_specs=pl.BlockSpec((tm, D_MODEL), lambda i, p: (i, 0)),
            scratch_shapes=[pltpu.VMEM((2, 2, tm, D_MODEL), f32), pltpu.SemaphoreType.DMA((2,))],
        ),
        out_shape=jax.ShapeDtypeStruct((m_rows, D_MODEL), f32),
        compiler_params=_params(("arbitrary",)),
        name="combine",
    )(pos, x, mod_l, wts, ln_g, ln_b, y)


def kernel(x, c, ctx, c_ctx, w_ada, b_ada, w_in, b_igate, b_fgate, mh_norm_g, conv_w, conv_b, w_out,
           ln1_g, ln1_b, w_router, b_router, w1, w3, w2, ln2_g, ln2_b):
    cond_raw = jnp.zeros((8, D_MODEL), f32).at[:BATCH].set(c).at[BATCH].set(c_ctx)
    mod = _ada(cond_raw, w_ada, b_ada).reshape(DEPTH, 8, 6, D_MODEL)

    xa = jnp.concatenate([x.reshape(N_LAT, D_MODEL), ctx.reshape(N_CTX, D_MODEL)], axis=0)
    w_router_p = jnp.zeros((D_MODEL, 128), f32).at[:, :N_EXPERTS].set(w_router)
    w_router_hi = w_router_p.astype(bf16)
    w_router_lo = (w_router_p - w_router_hi.astype(f32)).astype(bf16)
    w_router_p = jnp.concatenate([w_router_hi, w_router_lo], axis=1)
    b_router_c = b_router.reshape(N_EXPERTS, 1)
    w_in_t = jnp.swapaxes(w_in, 1, 2)
    w_outb = w_out.astype(bf16)

    for l in range(DEPTH):
        last = l == DEPTH - 1
        proj, gates = _inproj(xa, mod[l], w_in_t, l)

        bias = jnp.concatenate([b_igate[l], b_fgate[l]]).astype(f32)
        m_lat, m_ctx, (w1b, w3b, w2b) = _mlstm3(proj, gates, bias, mh_norm_g[l].reshape(1, D_MLSTM),
                                                w1, w3, w2, not last, l)

        xn, hp, idx8, wts = _mix(xa, mod[l], m_lat, m_ctx, proj, conv_w[l], conv_b[l].reshape(1, D_CONV),
                                 w_outb, ln1_g[l].reshape(1, D_MODEL),
                                 ln1_b[l].reshape(1, D_MODEL), w_router_p, b_router_c, not last, l)

        pos, te, n_used = _route(idx8, xn.shape[0])
        y = _experts(te, n_used, pos, hp, w1b, w3b, w2b)
        xa = _combine(pos, xn, mod[l], wts, ln2_g[l].reshape(1, D_MODEL), ln2_b[l].reshape(1, D_MODEL), y)

    return xa.reshape(BATCH, SEQ, D_MODEL)
```

```python
import functools

import jax
import jax.numpy as jnp
from jax import lax
from jax.experimental import pallas as pl
from jax.experimental.pallas import tpu as pltpu

f32 = jnp.float32
bf16 = jnp.bfloat16
i32 = jnp.int32
u32 = jnp.uint32

D_MODEL = 2048
BATCH = 4
SEQ = 2048
DEPTH = 4
GRID_W = 64
CTX_LEN = 256
D_MLSTM = 1024
HEADS = 4
DV = 256
DQK = 128
CHUNK = 64
D_CONV = 1024
D_CONV_H = 512
N_EXPERTS = 16
N_GROUPS = 4
EPG = 4
D_FF = 1024
ALPHA = (2 * DEPTH) ** 0.25
LN_EPS = 1e-6
QK_SCALE = DQK ** -0.5

N_LAT = BATCH * SEQ
N_CTX = BATCH * CTX_LEN
N_ALL = N_LAT + N_CTX
D_PROJ = 6144
N_GATE = 16
HALF = D_MODEL // 2

TM_IN = 3072
TN_IN = 512
TM_MIX = 256
TM_EXP = 256
TM_CMB = 256
VMEM_LIMIT = 56 * 1024 * 1024

HIGHEST = lax.Precision.HIGHEST


def _sigmoid(x):
    return 1.0 / (1.0 + jnp.exp(-x))


def _log_sigmoid(x):
    return jnp.minimum(x, 0.0) - jnp.log1p(jnp.exp(-jnp.abs(x)))


def _ln(x):
    mu = jnp.mean(x, axis=-1, keepdims=True)
    xc = x - mu
    var = jnp.mean(xc * xc, axis=-1, keepdims=True)
    return xc * lax.rsqrt(var + LN_EPS)


def _mod_row(i, tm):
    return jnp.minimum((i * tm) // SEQ, BATCH)


def _params(sem, vmem=VMEM_LIMIT):
    return pltpu.CompilerParams(dimension_semantics=sem, vmem_limit_bytes=vmem)


def _ada_kernel(c_ref, w_ref, b_ref, o_ref):
    c = c_ref[...]
    cond = c * _sigmoid(c)
    o_ref[...] = jnp.dot(cond, w_ref[...], preferred_element_type=f32) + b_ref[...]


def _ada(cond_raw, w_ada, b_ada):
    tn = 1024
    n = 6 * D_MODEL
    return pl.pallas_call(
        _ada_kernel,
        grid=(DEPTH, n // tn),
        in_specs=[
            pl.BlockSpec((8, D_MODEL), lambda l, j: (0, 0)),
            pl.BlockSpec((None, D_MODEL, tn), lambda l, j: (l, 0, j)),
            pl.BlockSpec((None, 1, tn), lambda l, j: (l, 0, j)),
        ],
        out_specs=pl.BlockSpec((None, 8, tn), lambda l, j: (l, 0, j)),
        out_shape=jax.ShapeDtypeStruct((DEPTH, 8, n), f32),
        compiler_params=_params(("parallel", "parallel")),
        name="ada",
    )(cond_raw, w_ada, b_ada.reshape(DEPTH, 1, n))


N_STATE = 2 * HEADS * DQK + D_MLSTM
N_STATE_BLOCKS = N_STATE // TN_IN
NT_DIMS = (((1,), (1,)), ((), ()))


def _modulated(x, mod_ref):
    return (_ln(x) * (1.0 + mod_ref[1:2, :]) + mod_ref[0:1, :]).astype(bf16)


def _inproj_kernel(h_ref, w_ref, wg_ref, o_ref, g_ref):
    @pl.when(pl.program_id(1) == 0)
    def _():
        g_ref[...] = lax.dot_general(h_ref[...], wg_ref[...].astype(bf16), NT_DIMS,
                                     preferred_element_type=f32)

    o_ref[...] = lax.dot_general(h_ref[...], w_ref[0].astype(bf16), NT_DIMS,
                                 preferred_element_type=f32).astype(o_ref.dtype)


def _inproj(h, w_in_t, l):
    m = h.shape[0]
    row0 = lambda j: pl.multiple_of(jnp.where(j < N_STATE_BLOCKS, j * TN_IN, j * TN_IN + N_GATE), 8)
    return pl.pallas_call(
        _inproj_kernel,
        grid=(m // TM_IN, D_PROJ // TN_IN),
        in_specs=[
            pl.BlockSpec((TM_IN, D_MODEL), lambda i, j: (i, 0)),
            pl.BlockSpec((pl.Element(1), pl.Element(TN_IN), pl.Element(D_MODEL)),
                         lambda i, j: (l, row0(j), 0)),
            pl.BlockSpec((None, N_GATE, D_MODEL), lambda i, j: (l, N_STATE // N_GATE, 0)),
        ],
        out_specs=[
            pl.BlockSpec((TM_IN, TN_IN), lambda i, j: (i, j)),
            pl.BlockSpec((TM_IN, N_GATE), lambda i, j: (i, 0)),
        ],
        out_shape=[
            jax.ShapeDtypeStruct((m, D_PROJ), bf16),
            jax.ShapeDtypeStruct((m, N_GATE), f32),
        ],
        compiler_params=_params(("parallel", "arbitrary")),
        name="inproj",
    )(h, w_in_t, w_in_t)


def _prep_kernel(x_ref, c_ref, mod_ref, xa_ref, h_ref, *, n_lat_blocks):
    i = pl.program_id(0)

    @pl.when(i < n_lat_blocks)
    def _():
        xa_ref[...] = x_ref[...]

    @pl.when(i >= n_lat_blocks)
    def _():
        xa_ref[...] = c_ref[...]

    h_ref[...] = _modulated(xa_ref[...], mod_ref)


def _prep(x2, c2, mod_l):
    tm = TM_CMB
    nlb = N_LAT // tm
    return pl.pallas_call(
        functools.partial(_prep_kernel, n_lat_blocks=nlb),
        grid=(N_ALL // tm,),
        in_specs=[
            pl.BlockSpec((tm, D_MODEL), lambda i: (jnp.minimum(i, nlb - 1), 0)),
            pl.BlockSpec((tm, D_MODEL), lambda i: (jnp.maximum(i - nlb, 0), 0)),
            pl.BlockSpec((None, 6, D_MODEL), lambda i: (_mod_row(i, tm), 0, 0)),
        ],
        out_specs=[pl.BlockSpec((tm, D_MODEL), lambda i: (i, 0)),
                   pl.BlockSpec((tm, D_MODEL), lambda i: (i, 0))],
        out_shape=[jax.ShapeDtypeStruct((N_ALL, D_MODEL), f32),
                   jax.ShapeDtypeStruct((N_ALL, D_MODEL), bf16)],
        compiler_params=_params(("parallel",)),
        name="prep",
    )(x2, c2, mod_l)


def _mlstm_chunk(q, k, v, fc_raw, ic_raw, fr_raw, ir_raw, b_i, b_f, ct_ref, n_ref, d, m, fwd):
    L = CHUNK
    lf_c = _log_sigmoid(fc_raw + b_f)
    i_c = ic_raw + b_i
    lf_r = _log_sigmoid(fr_raw + b_f)
    i_r = ir_raw + b_i
    rr = lax.broadcasted_iota(i32, (L, L), 0)
    cc = lax.broadcasted_iota(i32, (L, L), 1)
    lo = rr >= cc
    up = rr <= cc
    mask = lo if fwd else up
    a_mat = mask.astype(f32)
    a_t = (up if fwd else lo).astype(f32)
    bcol = jnp.dot(a_mat, jnp.broadcast_to(lf_c, (L, L)), precision=HIGHEST,
                   preferred_element_type=f32)
    brow = jnp.dot(jnp.broadcast_to(lf_r, (L, L)), a_t, precision=HIGHEST,
                   preferred_element_type=f32)
    dm = jnp.where(mask, bcol - brow + i_r, -jnp.inf)
    mloc = jnp.max(dm, axis=1, keepdims=True)
    wloc = jnp.exp(dm - mloc)
    b1 = bcol[:, 0:1]
    m_t = jnp.maximum(b1 + m, mloc)
    inter = jnp.exp(b1 + m - m_t)
    a = jnp.exp(mloc - m_t)
    qk = lax.dot_general(q, k, (((1,), (1,)), ((), ())), preferred_element_type=f32) * QK_SCALE
    sloc = qk * wloc
    ct = ct_ref[d]
    n = n_ref[d]
    num = inter * jnp.dot(q, ct.astype(bf16), preferred_element_type=f32) \
        + a * jnp.dot(sloc.astype(bf16), v, preferred_element_type=f32)
    den = inter * jnp.sum(q.astype(f32) * n, axis=1, keepdims=True) \
        + a * jnp.sum(sloc, axis=1, keepdims=True)
    hout = num / jnp.maximum(jnp.abs(den), jnp.exp(-m_t))
    last = L - 1 if fwd else 0
    b_last = b1[last:last + 1, :]
    wl = jnp.exp(b_last - b1 + i_c - mloc[last:last + 1, :])
    kw = k.astype(f32) * (wl * QK_SCALE)
    u = lax.dot_general(kw.astype(bf16), v, (((0,), (0,)), ((), ())), preferred_element_type=f32)
    decay = inter[last:last + 1, :]
    a_l = a[last:last + 1, :]
    ct_ref[d] = decay * ct + a_l * u
    n_ref[d] = decay * n + a_l * jnp.sum(kw, axis=0, keepdims=True)
    return hout, m_t[last:last + 1, :]


def _mlstm_kernel(bias_ref, ql_ref, kl_ref, vl_ref, ol_ref, qc_ref, kc_ref, vc_ref, oc_ref,
                  gcl_ref, grl_ref, gcc_ref, grc_ref, gain_ref, *rest, has_ctx_out):
    if has_ctx_out:
        ml_ref, mc_ref, ct_ref, n_ref, hfl, hbl, hfc, hbc = rest
    else:
        ml_ref, ct_ref, n_ref, hfl, hbl = rest
        mc_ref = hfc = hbc = None
    b = pl.program_id(0)
    h = pl.program_id(1)
    L = CHUNK
    bi = (bias_ref[h], bias_ref[HEADS + h])
    bf = (bias_ref[2 * HEADS + h], bias_ref[3 * HEADS + h])
    ct_ref[...] = jnp.zeros_like(ct_ref)
    n_ref[...] = jnp.zeros_like(n_ref)

    def run(q_ref, k_ref, v_ref, gc_ref, gr_ref, c_base, nchunks, hf, hb, carry):
        def body(j, carry):
            m_f, m_b = carry
            outs = []
            for d, fwd in ((0, True), (1, False)):
                c = j if fwd else nchunks - 1 - j
                r0 = pl.multiple_of(c * L, L)
                q = q_ref[pl.ds(r0, L), :]
                k = k_ref[pl.ds(r0, L), :]
                v = v_ref[pl.ds(r0, L), :]
                gcol = gc_ref[c_base + c]
                ic = gcol[:, d:d + 1]
                fc = gcol[:, 2 + d:3 + d]
                ir = gr_ref[d, pl.ds(c_base + c, 1), :]
                fr = gr_ref[2 + d, pl.ds(c_base + c, 1), :]
                hout, m_new = _mlstm_chunk(q, k, v, fc, ic, fr, ir, bi[d], bf[d], ct_ref, n_ref, d,
                                           m_f if fwd else m_b, fwd)
                if hf is not None:
                    (hf if fwd else hb)[pl.ds(r0, L), :] = hout
                outs.append(m_new)
            return tuple(outs)

        return lax.fori_loop(0, nchunks, body, carry)

    zero = jnp.zeros((1, 1), f32)
    carry = run(qc_ref, kc_ref, vc_ref, gcc_ref, grc_ref, b * (CTX_LEN // L), CTX_LEN // L, hfc, hbc,
                (zero, zero))
    run(ql_ref, kl_ref, vl_ref, gcl_ref, grl_ref, 0, SEQ // L, hfl, hbl, carry)

    gain = gain_ref[...]

    def finish(hf, hb, o_ref, out_ref, rows):
        slab = 256

        def ep(s, c):
            r0 = pl.multiple_of(s * slab, slab)
            hs = hf[pl.ds(r0, slab), :] + hb[pl.ds(r0, slab), :]
            r = lax.rsqrt(jnp.mean(hs * hs, axis=-1, keepdims=True) + LN_EPS)
            o = o_ref[pl.ds(r0, slab), :].astype(f32)
            out_ref[pl.ds(r0, slab), :] = (hs * r * gain * _sigmoid(o)).astype(out_ref.dtype)
            return c

        lax.fori_loop(0, rows // slab, ep, 0)

    finish(hfl, hbl, ol_ref, ml_ref, SEQ)
    if has_ctx_out:
        finish(hfc, hbc, oc_ref, mc_ref, CTX_LEN)


def _mlstm(proj, gates, bias, gain, has_ctx_out):
    L = CHUNK
    g = gates[:, :N_GATE]
    g4 = jnp.stack([g[:, 0:4], g[:, 4:8], g[:, 8:12], g[:, 12:16]], axis=-1)
    gh = jnp.transpose(g4, (1, 0, 2))
    gcol_l = gh[:, :N_LAT].reshape(HEADS, N_LAT // L, L, 4)
    gcol_c = gh[:, N_LAT:].reshape(HEADS, N_CTX // L, L, 4)
    gr = jnp.transpose(g4, (1, 2, 0))
    grow_l = gr[:, :, :N_LAT].reshape(HEADS, 4, N_LAT // L, L)
    grow_c = gr[:, :, N_LAT:].reshape(HEADS, 4, N_CTX // L, L)

    lat_rb = lambda b, h: b
    ctx_rb = lambda b, h: N_LAT // CTX_LEN + b
    in_specs = [
        pl.BlockSpec(memory_space=pltpu.SMEM),
        pl.BlockSpec((SEQ, DQK), lambda b, h: (lat_rb(b, h), h)),
        pl.BlockSpec((SEQ, DQK), lambda b, h: (lat_rb(b, h), HEADS + h)),
        pl.BlockSpec((SEQ, DV), lambda b, h: (lat_rb(b, h), HEADS + h)),
        pl.BlockSpec((SEQ, DV), lambda b, h: (lat_rb(b, h), 2 * HEADS + h)),
        pl.BlockSpec((CTX_LEN, DQK), lambda b, h: (ctx_rb(b, h), h)),
        pl.BlockSpec((CTX_LEN, DQK), lambda b, h: (ctx_rb(b, h), HEADS + h)),
        pl.BlockSpec((CTX_LEN, DV), lambda b, h: (ctx_rb(b, h), HEADS + h)),
        pl.BlockSpec((CTX_LEN, DV), lambda b, h: (ctx_rb(b, h), 2 * HEADS + h)),
        pl.BlockSpec((None, SEQ // L, L, 4), lambda b, h: (h, b, 0, 0)),
        pl.BlockSpec((None, 4, SEQ // L, L), lambda b, h: (h, 0, b, 0)),
        pl.BlockSpec((None, N_CTX // L, L, 4), lambda b, h: (h, 0, 0, 0)),
        pl.BlockSpec((None, 4, N_CTX // L, L), lambda b, h: (h, 0, 0, 0)),
        pl.BlockSpec((1, DV), lambda b, h: (0, h)),
    ]
    out_specs = [pl.BlockSpec((SEQ, DV), lambda b, h: (b, h))]
    out_shape = [jax.ShapeDtypeStruct((N_LAT, D_MLSTM), bf16)]
    scratch = [pltpu.VMEM((2, DQK, DV), f32), pltpu.VMEM((2, 1, DQK), f32),
               pltpu.VMEM((SEQ, DV), f32), pltpu.VMEM((SEQ, DV), f32)]
    if has_ctx_out:
        out_specs.append(pl.BlockSpec((CTX_LEN, DV), lambda b, h: (b, h)))
        out_shape.append(jax.ShapeDtypeStruct((N_CTX, D_MLSTM), bf16))
        scratch += [pltpu.VMEM((CTX_LEN, DV), f32), pltpu.VMEM((CTX_LEN, DV), f32)]
    res = pl.pallas_call(
        functools.partial(_mlstm_kernel, has_ctx_out=has_ctx_out),
        grid=(BATCH, HEADS),
        in_specs=in_specs,
        out_specs=out_specs,
        out_shape=out_shape,
        scratch_shapes=scratch,
        compiler_params=_params(("parallel", "parallel")),
        name="mlstm",
    )(bias, proj, proj, proj, proj, proj, proj, proj, proj, gcol_l, grow_l, gcol_c, grow_c, gain)
    return res if has_ctx_out else (res[0], None)


GROUP = 4
N_CHUNK_CTX = CTX_LEN // CHUNK
N_CHUNK_LAT = SEQ // CHUNK
N_CHUNK = N_CHUNK_CTX + N_CHUNK_LAT
N_GROUP_LAT = N_CHUNK_LAT // GROUP
LAT_ROW0 = 8
D_AUG = DV + 128


CVT_STEPS = 2 * (SEQ // CHUNK // 4)
CVT_ROWS_IN = D_MODEL // CVT_STEPS
CVT_ROWS_OUT = D_FF // CVT_STEPS


def _mlstm3_kernel(bias_ref, ql_ref, kl_ref, vl_ref, ol_ref, qc_ref, kc_ref, vc_ref, oc_ref,
                   gcl_ref, grl_ref, gcc_ref, grc_ref, gain_ref, w1_hbm, w3_hbm, w2_hbm, *rest,
                   has_ctx_out, layer):
    if has_ctx_out:
        ml_ref, mc_ref = rest[:2]
        rest = rest[2:]
    else:
        ml_ref, mc_ref = rest[0], None
        rest = rest[1:]
    w1b_hbm, w3b_hbm, w2b_hbm, s_ref, st_ref, msc, rows_s, cols_s = rest[:8]
    cin = rest[8:11]
    cout = rest[11:14]
    csem_in, csem_out = rest[14:16]
    L = CHUNK
    b = pl.program_id(0)
    h = pl.program_id(1)
    expert = b * HEADS + h
    cvt_src = (w1_hbm, w3_hbm, w2_hbm)
    cvt_dst = (w1b_hbm, w3b_hbm, w2b_hbm)
    cvt_rows = (CVT_ROWS_IN, CVT_ROWS_IN, CVT_ROWS_OUT)

    def cvt_in(k, t):
        r0 = pl.multiple_of(t * cvt_rows[k], cvt_rows[k])
        return pltpu.make_async_copy(cvt_src[k].at[layer, expert, pl.ds(r0, cvt_rows[k]), :],
                                     cin[k].at[t % 2], csem_in.at[k, t % 2])

    def cvt_out(k, t):
        r0 = pl.multiple_of(t * cvt_rows[k], cvt_rows[k])
        return pltpu.make_async_copy(cout[k].at[t % 2], cvt_dst[k].at[expert, pl.ds(r0, cvt_rows[k]), :],
                                     csem_out.at[k, t % 2])

    def cvt_step(t):
        for k in range(3):
            cvt_in(k, t).wait()

            @pl.when(t >= 2)
            def _():
                cvt_out(k, t - 2).wait()

            cout[k][t % 2] = cin[k][t % 2].astype(bf16)
            cvt_out(k, t).start()

            @pl.when(t + 2 < CVT_STEPS)
            def _():
                cvt_in(k, t + 2).start()

    for k in range(3):
        cvt_in(k, 0).start()
        cvt_in(k, 1).start()
    bi = (bias_ref[h], bias_ref[HEADS + h])
    bf = (bias_ref[2 * HEADS + h], bias_ref[3 * HEADS + h])

    rr = lax.broadcasted_iota(i32, (L, L), 0)
    cc = lax.broadcasted_iota(i32, (L, L), 1)
    lo_mask = rr >= cc
    up_mask = rr <= cc
    masks = (lo_mask, up_mask)

    def row_forms(gr_ref, sl, dst0, n):
        for d in range(2):
            tri = (up_mask if d == 0 else lo_mask).astype(f32)
            i_r = gr_ref[d, sl, :] + bi[d]
            lf_r = _log_sigmoid(gr_ref[2 + d, sl, :] + bf[d])
            b_r = jnp.dot(lf_r, tri, precision=HIGHEST, preferred_element_type=f32)
            rows_s[d, dst0:dst0 + n, :] = i_r - b_r

    row_forms(grc_ref, pl.ds(b * N_CHUNK_CTX, N_CHUNK_CTX), 0, N_CHUNK_CTX)
    row_forms(grl_ref, slice(None), LAT_ROW0, N_CHUNK_LAT)

    kind = lax.broadcasted_iota(i32, (1, 4 * GROUP), 1) % 4
    bias_v = jnp.where(kind == 0, bi[0], jnp.where(kind == 1, bi[1], jnp.where(kind == 2, bf[0], bf[1])))
    tpos = lax.broadcasted_iota(i32, (L, 4 * GROUP), 0)

    def col_forms(x):
        y = x + bias_v
        y = jnp.where(kind >= 2, _log_sigmoid(y), y)
        pre = y
        suf = y
        s = 1
        while s < L:
            pre = pre + jnp.where(tpos >= s, pltpu.roll(pre, s, 0), 0.0)
            suf = suf + jnp.where(tpos < L - s, pltpu.roll(suf, L - s, 0), 0.0)
            s *= 2
        return jnp.where(kind == 2, pre, jnp.where(kind == 3, suf, y))

    cols_s[0] = col_forms(gcc_ref[...])

    def col_body(g, c):
        cols_s[g + 1] = col_forms(gcl_ref[g])
        return c

    lax.fori_loop(0, N_GROUP_LAT, col_body, 0)

    ones_col = (lax.broadcasted_iota(i32, (L, 128), 1) == 0).astype(bf16)

    s_ref[...] = jnp.zeros_like(s_ref)

    def state_step(d, c, row, col, j, k, v, m):
        i_c = col[:, 4 * j + d:4 * j + d + 1]
        b1 = col[:, 4 * j + 2 + d:4 * j + 3 + d]
        last = L - 1 if d == 0 else 0
        b_last = b1[last:last + 1, :]
        ct = i_c - b1
        mx = jnp.max(ct, axis=0, keepdims=True)
        wl = jnp.exp(ct - mx)
        mloc_last = b_last + mx
        m_new = jnp.maximum(b_last + m, mloc_last)
        decay = jnp.exp(b_last + m - m_new)
        a_l = jnp.exp(mloc_last - m_new)
        s_old = s_ref[d]
        st_ref[d, c] = s_old.astype(bf16)
        msc[d, pl.ds(row, 1), :] = jnp.broadcast_to(m, (1, 128))
        kw = (k.astype(f32) * (wl * QK_SCALE)).astype(bf16)
        vaug = jnp.concatenate([v, ones_col], axis=1)
        u = lax.dot_general(kw, vaug, (((0,), (0,)), ((), ())), preferred_element_type=f32)
        s_ref[d] = decay * s_old + a_l * u
        return m_new

    zero = jnp.zeros((1, 1), f32)
    m_f = m_b = zero
    col0 = cols_s[0]
    for step in range(GROUP):
        jf, jb = step, GROUP - 1 - step
        m_f = state_step(0, jf, jf, col0, jf, kc_ref[jf * L:(jf + 1) * L, :], vc_ref[jf * L:(jf + 1) * L, :], m_f)
        m_b = state_step(1, jb, jb, col0, jb, kc_ref[jb * L:(jb + 1) * L, :], vc_ref[jb * L:(jb + 1) * L, :], m_b)

    def state_body(it, carry):
        m_f, m_b = carry
        gf = it
        gb = N_GROUP_LAT + 1 - it
        colf = cols_s[gf]
        colb = cols_s[gb]
        for step in range(GROUP):
            jf, jb = step, GROUP - 1 - step
            clf = (gf - 1) * GROUP + jf
            clb = (gb - 1) * GROUP + jb
            rf = pl.multiple_of(clf * L, L)
            rb = pl.multiple_of(clb * L, L)
            m_f = state_step(0, N_CHUNK_CTX + clf, LAT_ROW0 + clf, colf, jf,
                             kl_ref[pl.ds(rf, L), :], vl_ref[pl.ds(rf, L), :], m_f)
            m_b = state_step(1, N_CHUNK_CTX + clb, LAT_ROW0 + clb, colb, jb,
                             kl_ref[pl.ds(rb, L), :], vl_ref[pl.ds(rb, L), :], m_b)
        cvt_step(it - 1)
        return m_f, m_b

    lax.fori_loop(1, N_GROUP_LAT + 1, state_body, (m_f, m_b))

    gain = gain_ref[...]

    def out_chunk(q, k, v, o, col, j, c, row, out_ref, r0):
        qk = lax.dot_general(q, k, (((1,), (1,)), ((), ())), preferred_element_type=f32) * QK_SCALE
        vaug = jnp.concatenate([v, ones_col], axis=1)
        sl = []
        per = []
        for d in range(2):
            b1 = col[:, 4 * j + 2 + d:4 * j + 3 + d]
            dm = jnp.where(masks[d], b1 + rows_s[d, pl.ds(row, 1), :], -jnp.inf)
            mloc = jnp.max(dm, axis=1, keepdims=True)
            wloc = jnp.exp(dm - mloc)
            m_prev = msc[d, pl.ds(row, 1), :][:, 0:1]
            m_t = jnp.maximum(b1 + m_prev, mloc)
            inter = jnp.exp(b1 + m_prev - m_t)
            a = jnp.exp(mloc - m_t)
            sl.append((qk * wloc).astype(bf16))
            per.append((m_t, inter, a))
        x = jnp.dot(jnp.concatenate(sl, axis=0), vaug, preferred_element_type=f32)
        hs = None
        for d in range(2):
            y = jnp.dot(q, st_ref[d, c], preferred_element_type=f32)
            m_t, inter, a = per[d]
            xd = x[d * L:(d + 1) * L, :]
            den = inter * y[:, DV:DV + 1] + a * xd[:, DV:DV + 1]
            rinv = 1.0 / jnp.maximum(jnp.abs(den), jnp.exp(-m_t))
            hd = (inter * rinv) * y[:, :DV] + (a * rinv) * xd[:, :DV]
            hs = hd if hs is None else hs + hd
        r = lax.rsqrt(jnp.mean(hs * hs, axis=-1, keepdims=True) + LN_EPS)
        out_ref[pl.ds(r0, L), :] = (hs * r * gain * _sigmoid(o.astype(f32))).astype(out_ref.dtype)

    if has_ctx_out:
        for j in range(GROUP):
            sl_ = slice(j * L, (j + 1) * L)
            out_chunk(qc_ref[sl_, :], kc_ref[sl_, :], vc_ref[sl_, :], oc_ref[sl_, :], col0, j, j, j,
                      mc_ref, j * L)

    def out_body(g, carry):
        col = cols_s[g + 1]
        for j in range(GROUP):
            cl = g * GROUP + j
            r0 = pl.multiple_of(cl * L, L)
            out_chunk(ql_ref[pl.ds(r0, L), :], kl_ref[pl.ds(r0, L), :], vl_ref[pl.ds(r0, L), :],
                      ol_ref[pl.ds(r0, L), :], col, j, N_CHUNK_CTX + cl, LAT_ROW0 + cl, ml_ref, r0)
        cvt_step(N_GROUP_LAT + g)
        return carry

    lax.fori_loop(0, N_GROUP_LAT, out_body, 0)
    for k in range(3):
        cvt_out(k, CVT_STEPS - 2).wait()
        cvt_out(k, CVT_STEPS - 1).wait()


def _mlstm3(proj, gates, bias, gain, w1, w3, w2, has_ctx_out, layer):
    L = CHUNK
    g = gates[:, :N_GATE]
    g4 = jnp.stack([g[:, 0:4], g[:, 4:8], g[:, 8:12], g[:, 12:16]], axis=-1)
    gh = jnp.transpose(g4, (1, 0, 2))
    gcol_l = gh[:, :N_LAT].reshape(HEADS, BATCH * N_GROUP_LAT, GROUP, L, 4)
    gcol_l = jnp.transpose(gcol_l, (0, 1, 3, 2, 4)).reshape(HEADS, BATCH * N_GROUP_LAT, L, 4 * GROUP)
    gcol_c = gh[:, N_LAT:].reshape(HEADS, BATCH, GROUP, L, 4)
    gcol_c = jnp.transpose(gcol_c, (0, 1, 3, 2, 4)).reshape(HEADS, BATCH, L, 4 * GROUP)
    gr = jnp.transpose(g4, (1, 2, 0))
    grow_l = gr[:, :, :N_LAT].reshape(HEADS, 4, N_LAT // L, L)
    grow_c = gr[:, :, N_LAT:].reshape(HEADS, 4, N_CTX // L, L)

    ctx_rb = lambda b, h: N_LAT // CTX_LEN + b
    in_specs = [
        pl.BlockSpec(memory_space=pltpu.SMEM),
        pl.BlockSpec((SEQ, DQK), lambda b, h: (b, h)),
        pl.BlockSpec((SEQ, DQK), lambda b, h: (b, HEADS + h)),
        pl.BlockSpec((SEQ, DV), lambda b, h: (b, HEADS + h)),
        pl.BlockSpec((SEQ, DV), lambda b, h: (b, 2 * HEADS + h)),
        pl.BlockSpec((CTX_LEN, DQK), lambda b, h: (ctx_rb(b, h), h)),
        pl.BlockSpec((CTX_LEN, DQK), lambda b, h: (ctx_rb(b, h), HEADS + h)),
        pl.BlockSpec((CTX_LEN, DV), lambda b, h: (ctx_rb(b, h), HEADS + h)),
        pl.BlockSpec((CTX_LEN, DV), lambda b, h: (ctx_rb(b, h), 2 * HEADS + h)),
        pl.BlockSpec((None, N_GROUP_LAT, L, 4 * GROUP), lambda b, h: (h, b, 0, 0)),
        pl.BlockSpec((None, 4, N_CHUNK_LAT, L), lambda b, h: (h, 0, b, 0)),
        pl.BlockSpec((None, None, L, 4 * GROUP), lambda b, h: (h, b, 0, 0)),
        pl.BlockSpec((None, 4, N_CTX // L, L), lambda b, h: (h, 0, 0, 0)),
        pl.BlockSpec((1, DV), lambda b, h: (0, h)),
        pl.BlockSpec(memory_space=pl.ANY),
        pl.BlockSpec(memory_space=pl.ANY),
        pl.BlockSpec(memory_space=pl.ANY),
    ]
    out_specs = [pl.BlockSpec((SEQ, DV), lambda b, h: (b, h))]
    out_shape = [jax.ShapeDtypeStruct((N_LAT, D_MLSTM), bf16)]
    if has_ctx_out:
        out_specs.append(pl.BlockSpec((CTX_LEN, DV), lambda b, h: (b, h)))
        out_shape.append(jax.ShapeDtypeStruct((N_CTX, D_MLSTM), bf16))
    out_specs += [pl.BlockSpec(memory_space=pl.ANY)] * 3
    out_shape += [jax.ShapeDtypeStruct((N_EXPERTS, D_MODEL, D_FF), bf16),
                  jax.ShapeDtypeStruct((N_EXPERTS, D_MODEL, D_FF), bf16),
                  jax.ShapeDtypeStruct((N_EXPERTS, D_FF, D_MODEL), bf16)]
    scratch = [
        pltpu.VMEM((2, DQK, D_AUG), f32),
        pltpu.VMEM((2, N_CHUNK, DQK, D_AUG), bf16),
        pltpu.VMEM((2, LAT_ROW0 + N_CHUNK_LAT, 128), f32),
        pltpu.VMEM((2, LAT_ROW0 + N_CHUNK_LAT, L), f32),
        pltpu.VMEM((N_GROUP_LAT + 1, L, 4 * GROUP), f32),
        pltpu.VMEM((2, CVT_ROWS_IN, D_FF), f32),
        pltpu.VMEM((2, CVT_ROWS_IN, D_FF), f32),
        pltpu.VMEM((2, CVT_ROWS_OUT, D_MODEL), f32),
        pltpu.VMEM((2, CVT_ROWS_IN, D_FF), bf16),
        pltpu.VMEM((2, CVT_ROWS_IN, D_FF), bf16),
        pltpu.VMEM((2, CVT_ROWS_OUT, D_MODEL), bf16),
        pltpu.SemaphoreType.DMA((3, 2)),
        pltpu.SemaphoreType.DMA((3, 2)),
    ]
    assert N_EXPERTS == BATCH * HEADS
    res = pl.pallas_call(
        functools.partial(_mlstm3_kernel, has_ctx_out=has_ctx_out, layer=layer),
        grid=(BATCH, HEADS),
        in_specs=in_specs,
        out_specs=out_specs,
        out_shape=out_shape,
        scratch_shapes=scratch,
        compiler_params=_params(("arbitrary", "arbitrary")),
        name="mlstm",
    )(bias, proj, proj, proj, proj, proj, proj, proj, proj, gcol_l, grow_l, gcol_c, grow_c, gain,
      w1, w3, w2)
    if has_ctx_out:
        return res[0], res[1], res[2:]
    return res[0], None, res[1:]


def _top2_rows(vals):
    best = vals[0]
    bi = jnp.zeros(best.shape, i32)
    for j in range(1, len(vals)):
        take = vals[j] > best
        best = jnp.where(take, vals[j], best)
        bi = jnp.where(take, j, bi)
    sec = None
    si = None
    for j in range(len(vals)):
        cand = jnp.where(bi == j, -jnp.inf, vals[j])
        if sec is None:
            sec, si = cand, jnp.zeros(best.shape, i32)
        else:
            take = cand > sec
            sec = jnp.where(take, cand, sec)
            si = jnp.where(take, j, si)
    return bi, si


def _mix_kernel(x_ref, mod_ref, ml_ref, mc_ref, u_ref, bg_ref, cg_ref, ut_ref, ct_ref, ub_ref, cb_ref,
                cw_ref, cbias_ref, wo_ref, g1_ref, b1_ref, wr_ref, br_ref,
                xo_ref, hp_ref, idx_ref, wt_ref, m_scr, y_scr, *, n_lat_blocks, has_ctx):
    tm = TM_MIX
    i = pl.program_id(0)
    gate = mod_ref[2:3, :]
    shift2 = mod_ref[3:4, :]
    scale2 = mod_ref[4:5, :]
    cw = cw_ref[...]
    cbias = cbias_ref[...]
    row = lax.broadcasted_iota(i32, (tm, 1), 0)

    def shifted(z, first, last):
        prev = jnp.where(first, 0.0, pltpu.roll(z, 1, 0))
        nxt = jnp.where(last, 0.0, pltpu.roll(z, tm - 1, 0))
        return prev, nxt

    def lat_branch():
        z = cg_ref[...].astype(f32) * u_ref[...].astype(f32)
        col = row % GRID_W
        zh = z[:, :D_CONV_H]
        prev, nxt = shifted(zh, col == 0, col == GRID_W - 1)
        yh = cw[0:1, :D_CONV_H] * prev + cw[1:2, :D_CONV_H] * zh + cw[2:3, :D_CONV_H] * nxt
        bpb = SEQ // tm
        top_ok = (i % bpb != 0).astype(f32)
        bot_ok = (i % bpb != bpb - 1).astype(f32)
        zt = ct_ref[...].astype(f32) * ut_ref[...].astype(f32) * top_ok
        zb = cb_ref[...].astype(f32) * ub_ref[...].astype(f32) * bot_ok
        zv = z[:, D_CONV_H:]
        zext = jnp.concatenate([zt, zv, zb], axis=0)
        yv = cw[0:1, D_CONV_H:] * zext[0:tm] + cw[1:2, D_CONV_H:] * zv \
            + cw[2:3, D_CONV_H:] * zext[2 * GRID_W:2 * GRID_W + tm]
        y = jnp.concatenate([yh, yv], axis=1) + cbias
        y_scr[...] = (bg_ref[...].astype(f32) * y).astype(bf16)
        m_scr[...] = ml_ref[...]

    def ctx_branch():
        z = cg_ref[...].astype(f32) * u_ref[...].astype(f32)
        pos = row % CTX_LEN
        prev, nxt = shifted(z, pos == 0, pos == CTX_LEN - 1)
        y = cw[0:1, :] * prev + cw[1:2, :] * z + cw[2:3, :] * nxt + cbias
        y_scr[...] = (bg_ref[...].astype(f32) * y).astype(bf16)
        m_scr[...] = mc_ref[...]

    if has_ctx:
        pl.when(i < n_lat_blocks)(lat_branch)
        pl.when(i >= n_lat_blocks)(ctx_branch)
    else:
        lat_branch()

    out = jnp.dot(m_scr[...], wo_ref[0:D_MLSTM, :], preferred_element_type=f32) \
        + jnp.dot(y_scr[...], wo_ref[D_MLSTM:, :], preferred_element_type=f32)
    xn = _ln(ALPHA * x_ref[...] + gate * out) * g1_ref[...] + b1_ref[...]
    xo_ref[...] = xn
    h2 = _ln(xn) * (1.0 + scale2) + shift2

    hp_ref[...] = h2

    h_hi = h2.astype(bf16)
    hs = jnp.concatenate([h_hi, (h2 - h_hi.astype(f32)).astype(bf16)], axis=0)
    pr = jnp.dot(hs, wr_ref[...], preferred_element_type=f32)
    logits = pr[:tm, :128] + (pr[:tm, 128:] + pr[tm:, :128])
    lt = logits.T
    s = _sigmoid(lt[0:N_EXPERTS, :])
    sb = s + br_ref[...]
    sb_rows = [sb[e:e + 1, :] for e in range(N_EXPERTS)]
    s_rows = [s[e:e + 1, :] for e in range(N_EXPERTS)]
    gscores = []
    for g in range(N_GROUPS):
        a_, b_, c_, d_ = sb_rows[EPG * g:EPG * g + EPG]
        hi1, lo1 = jnp.maximum(a_, b_), jnp.minimum(a_, b_)
        hi2, lo2 = jnp.maximum(c_, d_), jnp.minimum(c_, d_)
        top = jnp.maximum(hi1, hi2)
        second = jnp.maximum(jnp.minimum(hi1, hi2), jnp.maximum(lo1, lo2))
        gscores.append(top + second)
    gbest = gscores[0]
    gsel = jnp.zeros(gbest.shape, i32)
    for g in range(1, N_GROUPS):
        take = gscores[g] > gbest
        gbest = jnp.where(take, gscores[g], gbest)
        gsel = jnp.where(take, g, gsel)

    def pick_group(rows, j):
        v = rows[j]
        for g in range(1, N_GROUPS):
            v = jnp.where(gsel == g, rows[EPG * g + j], v)
        return v

    cand_b = [pick_group(sb_rows, j) for j in range(EPG)]
    cand_s = [pick_group(s_rows, j) for j in range(EPG)]
    i1, i2 = _top2_rows(cand_b)

    def pick_idx(rows, idx):
        v = rows[0]
        for j in range(1, EPG):
            v = jnp.where(idx == j, rows[j], v)
        return v

    s1 = pick_idx(cand_s, i1)
    s2 = pick_idx(cand_s, i2)
    tot = s1 + s2
    w1 = s1 / tot
    w2 = s2 / tot
    e1 = gsel * EPG + i1
    e2 = gsel * EPG + i2
    r8 = lax.broadcasted_iota(i32, (8, tm), 0)
    idx_ref[...] = jnp.where(r8 == 0, e1, jnp.where(r8 == 1, e2, 0))
    r128 = lax.broadcasted_iota(i32, (128, tm), 0)
    wmat = jnp.where(r128 == 0, w1, jnp.where(r128 == 1, w2, 0.0))
    wt_ref[...] = wmat.T


def _mix(x, mod_l, m_lat, m_ctx, proj, conv_w, conv_b, w_out, ln_g, ln_b, w_router, b_router, has_ctx, l):
    tm = TM_MIX
    m_rows = N_ALL if has_ctx else N_LAT
    nlb = N_LAT // tm
    nblocks = m_rows // tm
    hb = tm // GRID_W
    n_hblocks = N_ALL // GRID_W
    if m_ctx is None:
        m_ctx = m_lat
    ncb = m_ctx.shape[0] // tm
    in_specs = [
        pl.BlockSpec((tm, D_MODEL), lambda i: (i, 0)),
        pl.BlockSpec((None, 6, D_MODEL), lambda i: (_mod_row(i, tm), 0, 0)),
        pl.BlockSpec((tm, D_MLSTM), lambda i: (jnp.minimum(i, nlb - 1), 0)),
        pl.BlockSpec((tm, D_MLSTM), lambda i: (jnp.clip(i - nlb, 0, ncb - 1), 0)),
        pl.BlockSpec((tm, D_CONV), lambda i: (i, 3)),
        pl.BlockSpec((tm, D_CONV), lambda i: (i, 4)),
        pl.BlockSpec((tm, D_CONV), lambda i: (i, 5)),
        pl.BlockSpec((GRID_W, D_CONV_H), lambda i: (jnp.maximum(i * hb - 1, 0), 7)),
        pl.BlockSpec((GRID_W, D_CONV_H), lambda i: (jnp.maximum(i * hb - 1, 0), 11)),
        pl.BlockSpec((GRID_W, D_CONV_H), lambda i: (jnp.minimum((i + 1) * hb, n_hblocks - 1), 7)),
        pl.BlockSpec((GRID_W, D_CONV_H), lambda i: (jnp.minimum((i + 1) * hb, n_hblocks - 1), 11)),
        pl.BlockSpec((3, D_CONV), lambda i: (0, 0)),
        pl.BlockSpec((1, D_CONV), lambda i: (0, 0)),
        pl.BlockSpec((None, D_MODEL, D_MODEL), lambda i: (l, 0, 0)),
        pl.BlockSpec((1, D_MODEL), lambda i: (0, 0)),
        pl.BlockSpec((1, D_MODEL), lambda i: (0, 0)),
        pl.BlockSpec((D_MODEL, 256), lambda i: (0, 0)),
        pl.BlockSpec((N_EXPERTS, 1), lambda i: (0, 0)),
    ]
    out_specs = [
        pl.BlockSpec((tm, D_MODEL), lambda i: (i, 0)),
        pl.BlockSpec((tm, D_MODEL), lambda i: (i, 0)),
        pl.BlockSpec((8, tm), lambda i: (0, i)),
        pl.BlockSpec((tm, 128), lambda i: (i, 0)),
    ]
    out_shape = [
        jax.ShapeDtypeStruct((m_rows, D_MODEL), f32),
        jax.ShapeDtypeStruct((m_rows, D_MODEL), f32),
        jax.ShapeDtypeStruct((8, m_rows), i32),
        jax.ShapeDtypeStruct((m_rows, 128), f32),
    ]
    return pl.pallas_call(
        functools.partial(_mix_kernel, n_lat_blocks=nlb, has_ctx=has_ctx),
        grid=(nblocks,),
        in_specs=in_specs,
        out_specs=out_specs,
        out_shape=out_shape,
        scratch_shapes=[pltpu.VMEM((tm, D_MLSTM), bf16), pltpu.VMEM((tm, D_CONV), bf16)],
        compiler_params=_params(("parallel",)),
        name="mix",
    )(x, mod_l, m_lat, m_ctx, proj, proj, proj, proj, proj, proj, proj,
      conv_w, conv_b, w_out, ln_g, ln_b, w_router, b_router)


def _route_kernel(ef_ref, pos_ref, meta_ref, *, rows):
    R = rows
    ef = ef_ref[...]
    li = lax.broadcasted_iota(i32, (128, 128), 0)
    lj = lax.broadcasted_iota(i32, (128, 128), 1)
    strict_up = (li < lj).astype(bf16)
    ri = lax.broadcasted_iota(i32, (R, R), 0)
    rj = lax.broadcasted_iota(i32, (R, R), 1)
    strict_lo = (rj < ri).astype(bf16)
    lane16 = lax.broadcasted_iota(i32, (R, N_EXPERTS), 1)

    ohs = [(ef == e) for e in range(N_EXPERTS)]
    within = [jnp.dot(oh.astype(bf16), strict_up, preferred_element_type=f32) for oh in ohs]
    rt = jnp.zeros((R, N_EXPERTS), f32)
    for e in range(N_EXPERTS):
        rt = jnp.where(lane16 == e, jnp.sum(ohs[e].astype(f32), axis=1, keepdims=True), rt)
    rp = jnp.dot(strict_lo, rt.astype(bf16), preferred_element_type=f32)
    counts = rp[R - 1:R, :] + rt[R - 1:R, :]
    ntile = jnp.floor((counts + (TM_EXP - 1)) * (1.0 / TM_EXP))
    ei = lax.broadcasted_iota(i32, (N_EXPERTS, N_EXPERTS), 0)
    ej = lax.broadcasted_iota(i32, (N_EXPERTS, N_EXPERTS), 1)
    tend = jnp.dot(ntile.astype(bf16), (ei <= ej).astype(bf16), preferred_element_type=f32)
    off = rp + (tend - ntile) * float(TM_EXP)
    pos = jnp.zeros((R, 128), f32)
    for e in range(N_EXPERTS):
        pos = jnp.where(ohs[e], within[e] + off[:, e:e + 1], pos)
    pos_ref[...] = pos.astype(i32)

    n_used = tend[:, N_EXPERTS - 1:N_EXPERTS]
    tile = jnp.minimum(lax.broadcasted_iota(i32, (1, 128), 1).astype(f32), n_used - 1.0)
    te = jnp.zeros((1, 128), f32)
    for e in range(N_EXPERTS):
        te = te + (tend[:, e:e + 1] <= tile).astype(f32)
    te = jnp.minimum(te, float(N_EXPERTS - 1))
    r8 = lax.broadcasted_iota(i32, (8, 128), 0)
    meta_ref[...] = jnp.where(r8 == 0, te, jnp.where(r8 == 1, n_used, 0.0)).astype(i32)


def _route(idx8, m_rows):
    rows = 2 * m_rows // 128
    n_tiles = 2 * m_rows // TM_EXP + N_EXPERTS
    assert n_tiles <= 128
    pos, meta = pl.pallas_call(
        functools.partial(_route_kernel, rows=rows),
        out_shape=[jax.ShapeDtypeStruct((rows, 128), i32), jax.ShapeDtypeStruct((8, 128), i32)],
        name="route",
    )(idx8[:2, :].reshape(rows, 128))
    return pos.reshape(-1), meta[0, :n_tiles], meta[1, 0:1]


def _expert_kernel(te_ref, nu_ref, pos_ref, hp_ref, w1_ref, w3_ref, w2_ref, y_ref, xbuf, sem, src_s, *,
                   m_rows):
    tm = TM_EXP
    i = pl.program_id(0)
    n_used = nu_ref[0]
    n_rows = src_s.shape[0]

    def issue(tile, slot):
        base = tile * tm
        for j in range(tm):
            t = src_s[base + j]
            pltpu.make_async_copy(hp_ref.at[pl.ds(t, 1), :], xbuf.at[slot, pl.ds(j, 1), :],
                                  sem.at[slot]).start()

    def wait(slot):
        pltpu.make_async_copy(hp_ref.at[pl.ds(0, tm), :], xbuf.at[slot], sem.at[slot]).wait()

    @pl.when(i == 0)
    def _():
        def clear(p, c):
            src_s[p] = 0
            return c

        lax.fori_loop(0, n_rows, clear, 0, unroll=16)

        def scatter(t, c):
            src_s[pos_ref[t]] = t
            src_s[pos_ref[m_rows + t]] = t
            return c

        lax.fori_loop(0, m_rows, scatter, 0, unroll=8)
        issue(0, 0)

    @pl.when(i < n_used)
    def _():
        slot = i % 2
        issue(i + 1, 1 - slot)
        wait(slot)
        xb = xbuf[slot].astype(bf16)
        a1 = jnp.dot(xb, w1_ref[...], preferred_element_type=f32)
        a3 = jnp.dot(xb, w3_ref[...], preferred_element_type=f32)
        act = (a1 * _sigmoid(a1) * a3).astype(bf16)
        y_ref[...] = jnp.dot(act, w2_ref[...], preferred_element_type=f32)

        @pl.when(i + 1 == n_used)
        def _():
            wait(1 - slot)

    @pl.when(i >= n_used)
    def _():
        y_ref[...] = jnp.zeros_like(y_ref)


def _experts(te, n_used, pos, hp, w1, w3, w2):
    tm = TM_EXP
    m_rows = hp.shape[0]
    n_tiles = te.shape[0]
    wspec = lambda r, c: pl.BlockSpec((None, r, c), lambda i, te, nu, pos: (te[i], 0, 0))
    return pl.pallas_call(
        functools.partial(_expert_kernel, m_rows=m_rows),
        grid_spec=pltpu.PrefetchScalarGridSpec(
            num_scalar_prefetch=3,
            grid=(n_tiles,),
            in_specs=[
                pl.BlockSpec(memory_space=pl.ANY),
                wspec(D_MODEL, D_FF),
                wspec(D_MODEL, D_FF),
                wspec(D_FF, D_MODEL),
            ],
            out_specs=pl.BlockSpec((tm, D_MODEL), lambda i, te, nu, pos: (i, 0)),
            scratch_shapes=[pltpu.VMEM((2, tm, D_MODEL), f32), pltpu.SemaphoreType.DMA((2,)),
                            pltpu.SMEM((n_tiles * tm,), i32)],
        ),
        out_shape=jax.ShapeDtypeStruct((n_tiles * tm, D_MODEL), f32),
        compiler_params=_params(("arbitrary",)),
        name="experts",
    )(te, n_used, pos, hp, w1, w3, w2)


def _combine_kernel(pos_ref, x_ref, mod_ref, wt_ref, g_ref, b_ref, *rest, m_rows, nblocks, has_next):
    if has_next:
        modn_ref, y_ref, o_ref, h_ref, buf, sem = rest
    else:
        y_ref, o_ref, buf, sem = rest
    tm = TM_CMB
    i = pl.program_id(0)

    def issue(blk, slot):
        base = blk * tm
        for j in range(tm):
            for k in range(2):
                p = pos_ref[k * m_rows + base + j]
                pltpu.make_async_copy(y_ref.at[pl.ds(p, 1), :], buf.at[slot, k, pl.ds(j, 1), :],
                                      sem.at[slot]).start()

    @pl.when(i == 0)
    def _():
        issue(0, 0)

    slot = i % 2
    for k in range(2):
        pltpu.make_async_copy(y_ref.at[pl.ds(0, tm), :], buf.at[slot, k], sem.at[slot]).wait()

    wt = wt_ref[...]
    moe = wt[:, 0:1] * buf[slot, 0] + wt[:, 1:2] * buf[slot, 1]
    gate = mod_ref[5:6, :]
    v = ALPHA * x_ref[...] + gate * moe

    issue(jnp.minimum(i + 1, nblocks - 1), 1 - slot)
    xn = _ln(v) * g_ref[...] + b_ref[...]
    o_ref[...] = xn
    if has_next:
        h_ref[...] = _modulated(xn, modn_ref)

    @pl.when(i + 1 == nblocks)
    def _():
        for k in range(2):
            pltpu.make_async_copy(y_ref.at[pl.ds(0, tm), :], buf.at[1 - slot, k], sem.at[1 - slot]).wait()


def _combine(pos, x, mod_l, wts, ln_g, ln_b, y, mod_next=None):
    tm = TM_CMB
    m_rows = x.shape[0]
    nblocks = m_rows // tm
    has_next = mod_next is not None
    mod_spec = pl.BlockSpec((None, 6, D_MODEL), lambda i, p: (_mod_row(i, tm), 0, 0))
    row_spec = pl.BlockSpec((tm, D_MODEL), lambda i, p: (i, 0))
    vec_spec = pl.BlockSpec((1, D_MODEL), lambda i, p: (0, 0))
    in_specs = [row_spec, mod_spec, pl.BlockSpec((tm, 128), lambda i, p: (i, 0)), vec_spec, vec_spec]
    args = [pos, x, mod_l, wts, ln_g, ln_b]
    out_specs = [row_spec]
    out_shape = [jax.ShapeDtypeStruct((m_rows, D_MODEL), f32)]
    if has_next:
        in_specs.append(mod_spec)
        args.append(mod_next)
        out_specs.append(row_spec)
        out_shape.append(jax.ShapeDtypeStruct((m_rows, D_MODEL), bf16))
    in_specs.append(pl.BlockSpec(memory_space=pl.ANY))
    args.append(y)
    res = pl.pallas_call(
        functools.partial(_combine_kernel, m_rows=m_rows, nblocks=nblocks, has_next=has_next),
        grid_spec=pltpu.PrefetchScalarGridSpec(
            num_scalar_prefetch=1,
            grid=(nblocks,),
            in_specs=in_specs,
            out_specs=out_specs,
            scratch_shapes=[pltpu.VMEM((2, 2, tm, D_MODEL), f32), pltpu.SemaphoreType.DMA((2,))],
        ),
        out_shape=out_shape,
        compiler_params=_params(("arbitrary",)),
        name="combine",
    )(*args)
    return (res[0], res[1]) if has_next else (res[0], None)


def kernel(x, c, ctx, c_ctx, w_ada, b_ada, w_in, b_igate, b_fgate, mh_norm_g, conv_w, conv_b, w_out,
           ln1_g, ln1_b, w_router, b_router, w1, w3, w2, ln2_g, ln2_b):
    cond_raw = jnp.zeros((8, D_MODEL), f32).at[:BATCH].set(c).at[BATCH].set(c_ctx)
    mod = _ada(cond_raw, w_ada, b_ada).reshape(DEPTH, 8, 6, D_MODEL)

    xa, h = _prep(x.reshape(N_LAT, D_MODEL), ctx.reshape(N_CTX, D_MODEL), mod[0])
    w_router_p = jnp.zeros((D_MODEL, 128), f32).at[:, :N_EXPERTS].set(w_router)
    w_router_hi = w_router_p.astype(bf16)
    w_router_lo = (w_router_p - w_router_hi.astype(f32)).astype(bf16)
    w_router_p = jnp.concatenate([w_router_hi, w_router_lo], axis=1)
    b_router_c = b_router.reshape(N_EXPERTS, 1)
    w_in_t = jnp.swapaxes(w_in, 1, 2)
    w_outb = w_out.astype(bf16)

    for l in range(DEPTH):
        last = l == DEPTH - 1
        proj, gates = _inproj(h, w_in_t, l)

        bias = jnp.concatenate([b_igate[l], b_fgate[l]]).astype(f32)
        m_lat, m_ctx, (w1b, w3b, w2b) = _mlstm3(proj, gates, bias, mh_norm_g[l].reshape(1, D_MLSTM),
                                                w1, w3, w2, not last, l)

        xn, hp, idx8, wts = _mix(xa, mod[l], m_lat, m_ctx, proj, conv_w[l], conv_b[l].reshape(1, D_CONV),
                                 w_outb, ln1_g[l].reshape(1, D_MODEL),
                                 ln1_b[l].reshape(1, D_MODEL), w_router_p, b_router_c, not last, l)

        pos, te, n_used = _route(idx8, xn.shape[0])
        y = _experts(te, n_used, pos, hp, w1b, w3b, w2b)
        xa, h = _combine(pos, xn, mod[l], wts, ln2_g[l].reshape(1, D_MODEL), ln2_b[l].reshape(1, D_MODEL), y,
                         None if last else mod[l + 1])

    return xa.reshape(BATCH, SEQ, D_MODEL)
```

```python
import functools

import jax
import jax.numpy as jnp
from jax import lax
from jax.experimental import pallas as pl
from jax.experimental.pallas import tpu as pltpu

f32 = jnp.float32
bf16 = jnp.bfloat16
i32 = jnp.int32
u32 = jnp.uint32

D_MODEL = 2048
BATCH = 4
SEQ = 2048
DEPTH = 4
GRID_W = 64
CTX_LEN = 256
D_MLSTM = 1024
HEADS = 4
DV = 256
DQK = 128
CHUNK = 64
D_CONV = 1024
D_CONV_H = 512
N_EXPERTS = 16
N_GROUPS = 4
EPG = 4
D_FF = 1024
ALPHA = (2 * DEPTH) ** 0.25
LN_EPS = 1e-6
QK_SCALE = DQK ** -0.5

N_LAT = BATCH * SEQ
N_CTX = BATCH * CTX_LEN
N_ALL = N_LAT + N_CTX
D_PROJ = 6144
N_GATE = 16
HALF = D_MODEL // 2

TM_IN = 3072
TN_IN = 512
TM_MIX = 256
TM_EXP = 256
TM_CMB = 256
VMEM_LIMIT = 56 * 1024 * 1024

HIGHEST = lax.Precision.HIGHEST


def _sigmoid(x):
    return 1.0 / (1.0 + jnp.exp(-x))


def _log_sigmoid(x):
    return jnp.minimum(x, 0.0) - jnp.log1p(jnp.exp(-jnp.abs(x)))


def _ln(x):
    mu = jnp.mean(x, axis=-1, keepdims=True)
    xc = x - mu
    var = jnp.mean(xc * xc, axis=-1, keepdims=True)
    return xc * lax.rsqrt(var + LN_EPS)


def _mod_row(i, tm):
    return jnp.minimum((i * tm) // SEQ, BATCH)


def _params(sem, vmem=VMEM_LIMIT):
    return pltpu.CompilerParams(dimension_semantics=sem, vmem_limit_bytes=vmem)


def _ada_kernel(c_ref, w_ref, b_ref, o_ref):
    c = c_ref[...]
    cond = c * _sigmoid(c)
    o_ref[...] = jnp.dot(cond, w_ref[...], preferred_element_type=f32) + b_ref[...]


def _ada(cond_raw, w_ada, b_ada):
    tn = 1024
    n = 6 * D_MODEL
    return pl.pallas_call(
        _ada_kernel,
        grid=(DEPTH, n // tn),
        in_specs=[
            pl.BlockSpec((8, D_MODEL), lambda l, j: (0, 0)),
            pl.BlockSpec((None, D_MODEL, tn), lambda l, j: (l, 0, j)),
            pl.BlockSpec((None, 1, tn), lambda l, j: (l, 0, j)),
        ],
        out_specs=pl.BlockSpec((None, 8, tn), lambda l, j: (l, 0, j)),
        out_shape=jax.ShapeDtypeStruct((DEPTH, 8, n), f32),
        compiler_params=_params(("parallel", "parallel")),
        name="ada",
    )(cond_raw, w_ada, b_ada.reshape(DEPTH, 1, n))


N_STATE = 2 * HEADS * DQK + D_MLSTM
N_STATE_BLOCKS = N_STATE // TN_IN
NT_DIMS = (((1,), (1,)), ((), ()))


def _modulated(x, mod_ref):
    return (_ln(x) * (1.0 + mod_ref[1:2, :]) + mod_ref[0:1, :]).astype(bf16)


def _inproj_kernel(h_ref, w_ref, wg_ref, o_ref, g_ref):
    @pl.when(pl.program_id(1) == 0)
    def _():
        g_ref[...] = lax.dot_general(h_ref[...], wg_ref[...].astype(bf16), NT_DIMS,
                                     preferred_element_type=f32)

    o_ref[...] = lax.dot_general(h_ref[...], w_ref[0].astype(bf16), NT_DIMS,
                                 preferred_element_type=f32).astype(o_ref.dtype)


def _inproj(h, w_in_t, l):
    m = h.shape[0]
    row0 = lambda j: pl.multiple_of(jnp.where(j < N_STATE_BLOCKS, j * TN_IN, j * TN_IN + N_GATE), 8)
    return pl.pallas_call(
        _inproj_kernel,
        grid=(m // TM_IN, D_PROJ // TN_IN),
        in_specs=[
            pl.BlockSpec((TM_IN, D_MODEL), lambda i, j: (i, 0)),
            pl.BlockSpec((pl.Element(1), pl.Element(TN_IN), pl.Element(D_MODEL)),
                         lambda i, j: (l, row0(j), 0)),
            pl.BlockSpec((None, N_GATE, D_MODEL), lambda i, j: (l, N_STATE // N_GATE, 0)),
        ],
        out_specs=[
            pl.BlockSpec((TM_IN, TN_IN), lambda i, j: (i, j)),
            pl.BlockSpec((TM_IN, N_GATE), lambda i, j: (i, 0)),
        ],
        out_shape=[
            jax.ShapeDtypeStruct((m, D_PROJ), bf16),
            jax.ShapeDtypeStruct((m, N_GATE), f32),
        ],
        compiler_params=_params(("parallel", "arbitrary")),
        name="inproj",
    )(h, w_in_t, w_in_t)


def _prep_kernel(x_ref, c_ref, mod_ref, xa_ref, h_ref, *, n_lat_blocks):
    i = pl.program_id(0)

    @pl.when(i < n_lat_blocks)
    def _():
        xa_ref[...] = x_ref[...]

    @pl.when(i >= n_lat_blocks)
    def _():
        xa_ref[...] = c_ref[...]

    h_ref[...] = _modulated(xa_ref[...], mod_ref)


def _prep(x2, c2, mod_l):
    tm = TM_CMB
    nlb = N_LAT // tm
    return pl.pallas_call(
        functools.partial(_prep_kernel, n_lat_blocks=nlb),
        grid=(N_ALL // tm,),
        in_specs=[
            pl.BlockSpec((tm, D_MODEL), lambda i: (jnp.minimum(i, nlb - 1), 0)),
            pl.BlockSpec((tm, D_MODEL), lambda i: (jnp.maximum(i - nlb, 0), 0)),
            pl.BlockSpec((None, 6, D_MODEL), lambda i: (_mod_row(i, tm), 0, 0)),
        ],
        out_specs=[pl.BlockSpec((tm, D_MODEL), lambda i: (i, 0)),
                   pl.BlockSpec((tm, D_MODEL), lambda i: (i, 0))],
        out_shape=[jax.ShapeDtypeStruct((N_ALL, D_MODEL), f32),
                   jax.ShapeDtypeStruct((N_ALL, D_MODEL), bf16)],
        compiler_params=_params(("parallel",)),
        name="prep",
    )(x2, c2, mod_l)


def _mlstm_chunk(q, k, v, fc_raw, ic_raw, fr_raw, ir_raw, b_i, b_f, ct_ref, n_ref, d, m, fwd):
    L = CHUNK
    lf_c = _log_sigmoid(fc_raw + b_f)
    i_c = ic_raw + b_i
    lf_r = _log_sigmoid(fr_raw + b_f)
    i_r = ir_raw + b_i
    rr = lax.broadcasted_iota(i32, (L, L), 0)
    cc = lax.broadcasted_iota(i32, (L, L), 1)
    lo = rr >= cc
    up = rr <= cc
    mask = lo if fwd else up
    a_mat = mask.astype(f32)
    a_t = (up if fwd else lo).astype(f32)
    bcol = jnp.dot(a_mat, jnp.broadcast_to(lf_c, (L, L)), precision=HIGHEST,
                   preferred_element_type=f32)
    brow = jnp.dot(jnp.broadcast_to(lf_r, (L, L)), a_t, precision=HIGHEST,
                   preferred_element_type=f32)
    dm = jnp.where(mask, bcol - brow + i_r, -jnp.inf)
    mloc = jnp.max(dm, axis=1, keepdims=True)
    wloc = jnp.exp(dm - mloc)
    b1 = bcol[:, 0:1]
    m_t = jnp.maximum(b1 + m, mloc)
    inter = jnp.exp(b1 + m - m_t)
    a = jnp.exp(mloc - m_t)
    qk = lax.dot_general(q, k, (((1,), (1,)), ((), ())), preferred_element_type=f32) * QK_SCALE
    sloc = qk * wloc
    ct = ct_ref[d]
    n = n_ref[d]
    num = inter * jnp.dot(q, ct.astype(bf16), preferred_element_type=f32) \
        + a * jnp.dot(sloc.astype(bf16), v, preferred_element_type=f32)
    den = inter * jnp.sum(q.astype(f32) * n, axis=1, keepdims=True) \
        + a * jnp.sum(sloc, axis=1, keepdims=True)
    hout = num / jnp.maximum(jnp.abs(den), jnp.exp(-m_t))
    last = L - 1 if fwd else 0
    b_last = b1[last:last + 1, :]
    wl = jnp.exp(b_last - b1 + i_c - mloc[last:last + 1, :])
    kw = k.astype(f32) * (wl * QK_SCALE)
    u = lax.dot_general(kw.astype(bf16), v, (((0,), (0,)), ((), ())), preferred_element_type=f32)
    decay = inter[last:last + 1, :]
    a_l = a[last:last + 1, :]
    ct_ref[d] = decay * ct + a_l * u
    n_ref[d] = decay * n + a_l * jnp.sum(kw, axis=0, keepdims=True)
    return hout, m_t[last:last + 1, :]


def _mlstm_kernel(bias_ref, ql_ref, kl_ref, vl_ref, ol_ref, qc_ref, kc_ref, vc_ref, oc_ref,
                  gcl_ref, grl_ref, gcc_ref, grc_ref, gain_ref, *rest, has_ctx_out):
    if has_ctx_out:
        ml_ref, mc_ref, ct_ref, n_ref, hfl, hbl, hfc, hbc = rest
    else:
        ml_ref, ct_ref, n_ref, hfl, hbl = rest
        mc_ref = hfc = hbc = None
    b = pl.program_id(0)
    h = pl.program_id(1)
    L = CHUNK
    bi = (bias_ref[h], bias_ref[HEADS + h])
    bf = (bias_ref[2 * HEADS + h], bias_ref[3 * HEADS + h])
    ct_ref[...] = jnp.zeros_like(ct_ref)
    n_ref[...] = jnp.zeros_like(n_ref)

    def run(q_ref, k_ref, v_ref, gc_ref, gr_ref, c_base, nchunks, hf, hb, carry):
        def body(j, carry):
            m_f, m_b = carry
            outs = []
            for d, fwd in ((0, True), (1, False)):
                c = j if fwd else nchunks - 1 - j
                r0 = pl.multiple_of(c * L, L)
                q = q_ref[pl.ds(r0, L), :]
                k = k_ref[pl.ds(r0, L), :]
                v = v_ref[pl.ds(r0, L), :]
                gcol = gc_ref[c_base + c]
                ic = gcol[:, d:d + 1]
                fc = gcol[:, 2 + d:3 + d]
                ir = gr_ref[d, pl.ds(c_base + c, 1), :]
                fr = gr_ref[2 + d, pl.ds(c_base + c, 1), :]
                hout, m_new = _mlstm_chunk(q, k, v, fc, ic, fr, ir, bi[d], bf[d], ct_ref, n_ref, d,
                                           m_f if fwd else m_b, fwd)
                if hf is not None:
                    (hf if fwd else hb)[pl.ds(r0, L), :] = hout
                outs.append(m_new)
            return tuple(outs)

        return lax.fori_loop(0, nchunks, body, carry)

    zero = jnp.zeros((1, 1), f32)
    carry = run(qc_ref, kc_ref, vc_ref, gcc_ref, grc_ref, b * (CTX_LEN // L), CTX_LEN // L, hfc, hbc,
                (zero, zero))
    run(ql_ref, kl_ref, vl_ref, gcl_ref, grl_ref, 0, SEQ // L, hfl, hbl, carry)

    gain = gain_ref[...]

    def finish(hf, hb, o_ref, out_ref, rows):
        slab = 256

        def ep(s, c):
            r0 = pl.multiple_of(s * slab, slab)
            hs = hf[pl.ds(r0, slab), :] + hb[pl.ds(r0, slab), :]
            r = lax.rsqrt(jnp.mean(hs * hs, axis=-1, keepdims=True) + LN_EPS)
            o = o_ref[pl.ds(r0, slab), :].astype(f32)
            out_ref[pl.ds(r0, slab), :] = (hs * r * gain * _sigmoid(o)).astype(out_ref.dtype)
            return c

        lax.fori_loop(0, rows // slab, ep, 0)

    finish(hfl, hbl, ol_ref, ml_ref, SEQ)
    if has_ctx_out:
        finish(hfc, hbc, oc_ref, mc_ref, CTX_LEN)


def _mlstm(proj, gates, bias, gain, has_ctx_out):
    L = CHUNK
    g = gates[:, :N_GATE]
    g4 = jnp.stack([g[:, 0:4], g[:, 4:8], g[:, 8:12], g[:, 12:16]], axis=-1)
    gh = jnp.transpose(g4, (1, 0, 2))
    gcol_l = gh[:, :N_LAT].reshape(HEADS, N_LAT // L, L, 4)
    gcol_c = gh[:, N_LAT:].reshape(HEADS, N_CTX // L, L, 4)
    gr = jnp.transpose(g4, (1, 2, 0))
    grow_l = gr[:, :, :N_LAT].reshape(HEADS, 4, N_LAT // L, L)
    grow_c = gr[:, :, N_LAT:].reshape(HEADS, 4, N_CTX // L, L)

    lat_rb = lambda b, h: b
    ctx_rb = lambda b, h: N_LAT // CTX_LEN + b
    in_specs = [
        pl.BlockSpec(memory_space=pltpu.SMEM),
        pl.BlockSpec((SEQ, DQK), lambda b, h: (lat_rb(b, h), h)),
        pl.BlockSpec((SEQ, DQK), lambda b, h: (lat_rb(b, h), HEADS + h)),
        pl.BlockSpec((SEQ, DV), lambda b, h: (lat_rb(b, h), HEADS + h)),
        pl.BlockSpec((SEQ, DV), lambda b, h: (lat_rb(b, h), 2 * HEADS + h)),
        pl.BlockSpec((CTX_LEN, DQK), lambda b, h: (ctx_rb(b, h), h)),
        pl.BlockSpec((CTX_LEN, DQK), lambda b, h: (ctx_rb(b, h), HEADS + h)),
        pl.BlockSpec((CTX_LEN, DV), lambda b, h: (ctx_rb(b, h), HEADS + h)),
        pl.BlockSpec((CTX_LEN, DV), lambda b, h: (ctx_rb(b, h), 2 * HEADS + h)),
        pl.BlockSpec((None, SEQ // L, L, 4), lambda b, h: (h, b, 0, 0)),
        pl.BlockSpec((None, 4, SEQ // L, L), lambda b, h: (h, 0, b, 0)),
        pl.BlockSpec((None, N_CTX // L, L, 4), lambda b, h: (h, 0, 0, 0)),
        pl.BlockSpec((None, 4, N_CTX // L, L), lambda b, h: (h, 0, 0, 0)),
        pl.BlockSpec((1, DV), lambda b, h: (0, h)),
    ]
    out_specs = [pl.BlockSpec((SEQ, DV), lambda b, h: (b, h))]
    out_shape = [jax.ShapeDtypeStruct((N_LAT, D_MLSTM), bf16)]
    scratch = [pltpu.VMEM((2, DQK, DV), f32), pltpu.VMEM((2, 1, DQK), f32),
               pltpu.VMEM((SEQ, DV), f32), pltpu.VMEM((SEQ, DV), f32)]
    if has_ctx_out:
        out_specs.append(pl.BlockSpec((CTX_LEN, DV), lambda b, h: (b, h)))
        out_shape.append(jax.ShapeDtypeStruct((N_CTX, D_MLSTM), bf16))
        scratch += [pltpu.VMEM((CTX_LEN, DV), f32), pltpu.VMEM((CTX_LEN, DV), f32)]
    res = pl.pallas_call(
        functools.partial(_mlstm_kernel, has_ctx_out=has_ctx_out),
        grid=(BATCH, HEADS),
        in_specs=in_specs,
        out_specs=out_specs,
        out_shape=out_shape,
        scratch_shapes=scratch,
        compiler_params=_params(("parallel", "parallel")),
        name="mlstm",
    )(bias, proj, proj, proj, proj, proj, proj, proj, proj, gcol_l, grow_l, gcol_c, grow_c, gain)
    return res if has_ctx_out else (res[0], None)


MCHUNK = 128
N_CHUNK_CTX = CTX_LEN // MCHUNK
N_CHUNK_LAT = SEQ // MCHUNK
N_CHUNK = N_CHUNK_CTX + N_CHUNK_LAT
GROUP = N_CHUNK_CTX
N_GROUP_LAT = N_CHUNK_LAT // GROUP
LAT_ROW0 = 8
D_AUG = DV + 128


CVT_STEPS = 2 * N_GROUP_LAT
CVT_ROWS_IN = D_MODEL // CVT_STEPS
CVT_ROWS_OUT = D_FF // CVT_STEPS


def _mlstm3_kernel(bias_ref, ql_ref, kl_ref, vl_ref, ol_ref, qc_ref, kc_ref, vc_ref, oc_ref,
                   gcl_ref, grl_ref, gcc_ref, grc_ref, gain_ref, w1_hbm, w3_hbm, w2_hbm, *rest,
                   has_ctx_out, layer):
    if has_ctx_out:
        ml_ref, mc_ref = rest[:2]
        rest = rest[2:]
    else:
        ml_ref, mc_ref = rest[0], None
        rest = rest[1:]
    w1b_hbm, w3b_hbm, w2b_hbm, s_ref, st_ref, msc, rows_s, cols_s = rest[:8]
    cin = rest[8:11]
    cout = rest[11:14]
    csem_in, csem_out = rest[14:16]
    L = MCHUNK
    b = pl.program_id(0)
    h = pl.program_id(1)
    expert = b * HEADS + h
    cvt_src = (w1_hbm, w3_hbm, w2_hbm)
    cvt_dst = (w1b_hbm, w3b_hbm, w2b_hbm)
    cvt_rows = (CVT_ROWS_IN, CVT_ROWS_IN, CVT_ROWS_OUT)

    def cvt_in(k, t):
        r0 = pl.multiple_of(t * cvt_rows[k], cvt_rows[k])
        return pltpu.make_async_copy(cvt_src[k].at[layer, expert, pl.ds(r0, cvt_rows[k]), :],
                                     cin[k].at[t % 2], csem_in.at[k, t % 2])

    def cvt_out(k, t):
        r0 = pl.multiple_of(t * cvt_rows[k], cvt_rows[k])
        return pltpu.make_async_copy(cout[k].at[t % 2], cvt_dst[k].at[expert, pl.ds(r0, cvt_rows[k]), :],
                                     csem_out.at[k, t % 2])

    def cvt_step(t):
        for k in range(3):
            cvt_in(k, t).wait()

            @pl.when(t >= 2)
            def _():
                cvt_out(k, t - 2).wait()

            cout[k][t % 2] = cin[k][t % 2].astype(bf16)
            cvt_out(k, t).start()

            @pl.when(t + 2 < CVT_STEPS)
            def _():
                cvt_in(k, t + 2).start()

    for k in range(3):
        cvt_in(k, 0).start()
        cvt_in(k, 1).start()
    bi = (bias_ref[h], bias_ref[HEADS + h])
    bf = (bias_ref[2 * HEADS + h], bias_ref[3 * HEADS + h])

    rr = lax.broadcasted_iota(i32, (L, L), 0)
    cc = lax.broadcasted_iota(i32, (L, L), 1)
    lo_mask = rr >= cc
    up_mask = rr <= cc
    masks = (lo_mask, up_mask)

    def row_forms(gr_ref, sl, dst0, n):
        for d in range(2):
            tri = (up_mask if d == 0 else lo_mask).astype(f32)
            i_r = gr_ref[d, sl, :] + bi[d]
            lf_r = _log_sigmoid(gr_ref[2 + d, sl, :] + bf[d])
            b_r = jnp.dot(lf_r, tri, precision=HIGHEST, preferred_element_type=f32)
            rows_s[d, dst0:dst0 + n, :] = i_r - b_r

    row_forms(grc_ref, pl.ds(b * N_CHUNK_CTX, N_CHUNK_CTX), 0, N_CHUNK_CTX)
    row_forms(grl_ref, slice(None), LAT_ROW0, N_CHUNK_LAT)

    kind = lax.broadcasted_iota(i32, (1, 4 * GROUP), 1) % 4
    bias_v = jnp.where(kind == 0, bi[0], jnp.where(kind == 1, bi[1], jnp.where(kind == 2, bf[0], bf[1])))
    tpos = lax.broadcasted_iota(i32, (L, 4 * GROUP), 0)

    def col_forms(x):
        y = x + bias_v
        y = jnp.where(kind >= 2, _log_sigmoid(y), y)
        pre = y
        suf = y
        s = 1
        while s < L:
            pre = pre + jnp.where(tpos >= s, pltpu.roll(pre, s, 0), 0.0)
            suf = suf + jnp.where(tpos < L - s, pltpu.roll(suf, L - s, 0), 0.0)
            s *= 2
        return jnp.where(kind == 2, pre, jnp.where(kind == 3, suf, y))

    cols_s[0] = col_forms(gcc_ref[...])

    def col_body(g, c):
        cols_s[g + 1] = col_forms(gcl_ref[g])
        return c

    lax.fori_loop(0, N_GROUP_LAT, col_body, 0)

    ones_col = (lax.broadcasted_iota(i32, (L, 128), 1) == 0).astype(bf16)

    s_ref[...] = jnp.zeros_like(s_ref)

    def state_step(d, c, row, col, j, k, v, m):
        i_c = col[:, 4 * j + d:4 * j + d + 1]
        b1 = col[:, 4 * j + 2 + d:4 * j + 3 + d]
        last = L - 1 if d == 0 else 0
        b_last = b1[last:last + 1, :]
        ct = i_c - b1
        mx = jnp.max(ct, axis=0, keepdims=True)
        wl = jnp.exp(ct - mx)
        mloc_last = b_last + mx
        m_new = jnp.maximum(b_last + m, mloc_last)
        decay = jnp.exp(b_last + m - m_new)
        a_l = jnp.exp(mloc_last - m_new)
        s_old = s_ref[d]
        st_ref[d, c] = s_old.astype(bf16)
        msc[d, pl.ds(row, 1), :] = jnp.broadcast_to(m, (1, 128))
        kw = (k.astype(f32) * (wl * QK_SCALE)).astype(bf16)
        vaug = jnp.concatenate([v, ones_col], axis=1)
        u = lax.dot_general(kw, vaug, (((0,), (0,)), ((), ())), preferred_element_type=f32)
        s_ref[d] = decay * s_old + a_l * u
        return m_new

    zero = jnp.zeros((1, 1), f32)
    m_f = m_b = zero
    col0 = cols_s[0]
    for step in range(GROUP):
        jf, jb = step, GROUP - 1 - step
        m_f = state_step(0, jf, jf, col0, jf, kc_ref[jf * L:(jf + 1) * L, :], vc_ref[jf * L:(jf + 1) * L, :], m_f)
        m_b = state_step(1, jb, jb, col0, jb, kc_ref[jb * L:(jb + 1) * L, :], vc_ref[jb * L:(jb + 1) * L, :], m_b)

    def state_body(it, carry):
        m_f, m_b = carry
        gf = it
        gb = N_GROUP_LAT + 1 - it
        colf = cols_s[gf]
        colb = cols_s[gb]
        for step in range(GROUP):
            jf, jb = step, GROUP - 1 - step
            clf = (gf - 1) * GROUP + jf
            clb = (gb - 1) * GROUP + jb
            rf = pl.multiple_of(clf * L, L)
            rb = pl.multiple_of(clb * L, L)
            m_f = state_step(0, N_CHUNK_CTX + clf, LAT_ROW0 + clf, colf, jf,
                             kl_ref[pl.ds(rf, L), :], vl_ref[pl.ds(rf, L), :], m_f)
            m_b = state_step(1, N_CHUNK_CTX + clb, LAT_ROW0 + clb, colb, jb,
                             kl_ref[pl.ds(rb, L), :], vl_ref[pl.ds(rb, L), :], m_b)
        cvt_step(it - 1)
        return m_f, m_b

    lax.fori_loop(1, N_GROUP_LAT + 1, state_body, (m_f, m_b))

    gain = gain_ref[...]

    def out_chunk(q, k, v, o, col, j, c, row, out_ref, r0):
        qk = lax.dot_general(q, k, (((1,), (1,)), ((), ())), preferred_element_type=f32) * QK_SCALE
        vaug = jnp.concatenate([v, ones_col], axis=1)
        sl = []
        per = []
        for d in range(2):
            b1 = col[:, 4 * j + 2 + d:4 * j + 3 + d]
            dm = jnp.where(masks[d], b1 + rows_s[d, pl.ds(row, 1), :], -jnp.inf)
            mloc = jnp.max(dm, axis=1, keepdims=True)
            wloc = jnp.exp(dm - mloc)
            m_prev = msc[d, pl.ds(row, 1), :][:, 0:1]
            m_t = jnp.maximum(b1 + m_prev, mloc)
            inter = jnp.exp(b1 + m_prev - m_t)
            a = jnp.exp(mloc - m_t)
            sl.append((qk * wloc).astype(bf16))
            per.append((m_t, inter, a))
        x = jnp.dot(jnp.concatenate(sl, axis=0), vaug, preferred_element_type=f32)
        hs = None
        for d in range(2):
            y = jnp.dot(q, st_ref[d, c], preferred_element_type=f32)
            m_t, inter, a = per[d]
            xd = x[d * L:(d + 1) * L, :]
            den = inter * y[:, DV:DV + 1] + a * xd[:, DV:DV + 1]
            rinv = 1.0 / jnp.maximum(jnp.abs(den), jnp.exp(-m_t))
            hd = (inter * rinv) * y[:, :DV] + (a * rinv) * xd[:, :DV]
            hs = hd if hs is None else hs + hd
        r = lax.rsqrt(jnp.mean(hs * hs, axis=-1, keepdims=True) + LN_EPS)
        out_ref[pl.ds(r0, L), :] = (hs * r * gain * _sigmoid(o.astype(f32))).astype(out_ref.dtype)

    if has_ctx_out:
        for j in range(GROUP):
            sl_ = slice(j * L, (j + 1) * L)
            out_chunk(qc_ref[sl_, :], kc_ref[sl_, :], vc_ref[sl_, :], oc_ref[sl_, :], col0, j, j, j,
                      mc_ref, j * L)

    def out_body(g2, carry):
        for gg in range(2):
            g = 2 * g2 + gg
            col = cols_s[g + 1]
            for j in range(GROUP):
                cl = g * GROUP + j
                r0 = pl.multiple_of(cl * L, L)
                out_chunk(ql_ref[pl.ds(r0, L), :], kl_ref[pl.ds(r0, L), :], vl_ref[pl.ds(r0, L), :],
                          ol_ref[pl.ds(r0, L), :], col, j, N_CHUNK_CTX + cl, LAT_ROW0 + cl, ml_ref, r0)
        cvt_step(N_GROUP_LAT + 2 * g2)
        cvt_step(N_GROUP_LAT + 2 * g2 + 1)
        return carry

    lax.fori_loop(0, N_GROUP_LAT // 2, out_body, 0)
    for k in range(3):
        cvt_out(k, CVT_STEPS - 2).wait()
        cvt_out(k, CVT_STEPS - 1).wait()


def _mlstm3(proj, gates, bias, gain, w1, w3, w2, has_ctx_out, layer):
    L = MCHUNK
    g = gates[:, :N_GATE]
    g4 = jnp.stack([g[:, 0:4], g[:, 4:8], g[:, 8:12], g[:, 12:16]], axis=-1)
    gh = jnp.transpose(g4, (1, 0, 2))
    gcol_l = gh[:, :N_LAT].reshape(HEADS, BATCH * N_GROUP_LAT, GROUP, L, 4)
    gcol_l = jnp.transpose(gcol_l, (0, 1, 3, 2, 4)).reshape(HEADS, BATCH * N_GROUP_LAT, L, 4 * GROUP)
    gcol_c = gh[:, N_LAT:].reshape(HEADS, BATCH, GROUP, L, 4)
    gcol_c = jnp.transpose(gcol_c, (0, 1, 3, 2, 4)).reshape(HEADS, BATCH, L, 4 * GROUP)
    gr = jnp.transpose(g4, (1, 2, 0))
    grow_l = gr[:, :, :N_LAT].reshape(HEADS, 4, N_LAT // L, L)
    grow_c = gr[:, :, N_LAT:].reshape(HEADS, 4, N_CTX // L, L)

    ctx_rb = lambda b, h: N_LAT // CTX_LEN + b
    in_specs = [
        pl.BlockSpec(memory_space=pltpu.SMEM),
        pl.BlockSpec((SEQ, DQK), lambda b, h: (b, h)),
        pl.BlockSpec((SEQ, DQK), lambda b, h: (b, HEADS + h)),
        pl.BlockSpec((SEQ, DV), lambda b, h: (b, HEADS + h)),
        pl.BlockSpec((SEQ, DV), lambda b, h: (b, 2 * HEADS + h)),
        pl.BlockSpec((CTX_LEN, DQK), lambda b, h: (ctx_rb(b, h), h)),
        pl.BlockSpec((CTX_LEN, DQK), lambda b, h: (ctx_rb(b, h), HEADS + h)),
        pl.BlockSpec((CTX_LEN, DV), lambda b, h: (ctx_rb(b, h), HEADS + h)),
        pl.BlockSpec((CTX_LEN, DV), lambda b, h: (ctx_rb(b, h), 2 * HEADS + h)),
        pl.BlockSpec((None, N_GROUP_LAT, L, 4 * GROUP), lambda b, h: (h, b, 0, 0)),
        pl.BlockSpec((None, 4, N_CHUNK_LAT, L), lambda b, h: (h, 0, b, 0)),
        pl.BlockSpec((None, None, L, 4 * GROUP), lambda b, h: (h, b, 0, 0)),
        pl.BlockSpec((None, 4, N_CTX // L, L), lambda b, h: (h, 0, 0, 0)),
        pl.BlockSpec((1, DV), lambda b, h: (0, h)),
        pl.BlockSpec(memory_space=pl.ANY),
        pl.BlockSpec(memory_space=pl.ANY),
        pl.BlockSpec(memory_space=pl.ANY),
    ]
    out_specs = [pl.BlockSpec((SEQ, DV), lambda b, h: (b, h))]
    out_shape = [jax.ShapeDtypeStruct((N_LAT, D_MLSTM), bf16)]
    if has_ctx_out:
        out_specs.append(pl.BlockSpec((CTX_LEN, DV), lambda b, h: (b, h)))
        out_shape.append(jax.ShapeDtypeStruct((N_CTX, D_MLSTM), bf16))
    out_specs += [pl.BlockSpec(memory_space=pl.ANY)] * 3
    out_shape += [jax.ShapeDtypeStruct((N_EXPERTS, D_MODEL, D_FF), bf16),
                  jax.ShapeDtypeStruct((N_EXPERTS, D_MODEL, D_FF), bf16),
                  jax.ShapeDtypeStruct((N_EXPERTS, D_FF, D_MODEL), bf16)]
    scratch = [
        pltpu.VMEM((2, DQK, D_AUG), f32),
        pltpu.VMEM((2, N_CHUNK, DQK, D_AUG), bf16),
        pltpu.VMEM((2, LAT_ROW0 + N_CHUNK_LAT, 128), f32),
        pltpu.VMEM((2, LAT_ROW0 + N_CHUNK_LAT, L), f32),
        pltpu.VMEM((N_GROUP_LAT + 1, L, 4 * GROUP), f32),
        pltpu.VMEM((2, CVT_ROWS_IN, D_FF), f32),
        pltpu.VMEM((2, CVT_ROWS_IN, D_FF), f32),
        pltpu.VMEM((2, CVT_ROWS_OUT, D_MODEL), f32),
        pltpu.VMEM((2, CVT_ROWS_IN, D_FF), bf16),
        pltpu.VMEM((2, CVT_ROWS_IN, D_FF), bf16),
        pltpu.VMEM((2, CVT_ROWS_OUT, D_MODEL), bf16),
        pltpu.SemaphoreType.DMA((3, 2)),
        pltpu.SemaphoreType.DMA((3, 2)),
    ]
    assert N_EXPERTS == BATCH * HEADS
    res = pl.pallas_call(
        functools.partial(_mlstm3_kernel, has_ctx_out=has_ctx_out, layer=layer),
        grid=(BATCH, HEADS),
        in_specs=in_specs,
        out_specs=out_specs,
        out_shape=out_shape,
        scratch_shapes=scratch,
        compiler_params=_params(("arbitrary", "arbitrary")),
        name="mlstm",
    )(bias, proj, proj, proj, proj, proj, proj, proj, proj, gcol_l, grow_l, gcol_c, grow_c, gain,
      w1, w3, w2)
    if has_ctx_out:
        return res[0], res[1], res[2:]
    return res[0], None, res[1:]


def _top2_rows(vals):
    best = vals[0]
    bi = jnp.zeros(best.shape, i32)
    for j in range(1, len(vals)):
        take = vals[j] > best
        best = jnp.where(take, vals[j], best)
        bi = jnp.where(take, j, bi)
    sec = None
    si = None
    for j in range(len(vals)):
        cand = jnp.where(bi == j, -jnp.inf, vals[j])
        if sec is None:
            sec, si = cand, jnp.zeros(best.shape, i32)
        else:
            take = cand > sec
            sec = jnp.where(take, cand, sec)
            si = jnp.where(take, j, si)
    return bi, si


def _mix_kernel(x_ref, mod_ref, ml_ref, mc_ref, u_ref, bg_ref, cg_ref, ut_ref, ct_ref, ub_ref, cb_ref,
                cw_ref, cbias_ref, wo_ref, g1_ref, b1_ref, wr_ref, br_ref,
                xo_ref, hp_ref, idx_ref, wt_ref, m_scr, y_scr, *, n_lat_blocks, has_ctx):
    tm = TM_MIX
    i = pl.program_id(0)
    gate = mod_ref[2:3, :]
    shift2 = mod_ref[3:4, :]
    scale2 = mod_ref[4:5, :]
    cw = cw_ref[...]
    cbias = cbias_ref[...]
    row = lax.broadcasted_iota(i32, (tm, 1), 0)

    def shifted(z, first, last):
        prev = jnp.where(first, 0.0, pltpu.roll(z, 1, 0))
        nxt = jnp.where(last, 0.0, pltpu.roll(z, tm - 1, 0))
        return prev, nxt

    def lat_branch():
        z = cg_ref[...].astype(f32) * u_ref[...].astype(f32)
        col = row % GRID_W
        zh = z[:, :D_CONV_H]
        prev, nxt = shifted(zh, col == 0, col == GRID_W - 1)
        yh = cw[0:1, :D_CONV_H] * prev + cw[1:2, :D_CONV_H] * zh + cw[2:3, :D_CONV_H] * nxt
        bpb = SEQ // tm
        top_ok = (i % bpb != 0).astype(f32)
        bot_ok = (i % bpb != bpb - 1).astype(f32)
        zt = ct_ref[...].astype(f32) * ut_ref[...].astype(f32) * top_ok
        zb = cb_ref[...].astype(f32) * ub_ref[...].astype(f32) * bot_ok
        zv = z[:, D_CONV_H:]
        zext = jnp.concatenate([zt, zv, zb], axis=0)
        yv = cw[0:1, D_CONV_H:] * zext[0:tm] + cw[1:2, D_CONV_H:] * zv \
            + cw[2:3, D_CONV_H:] * zext[2 * GRID_W:2 * GRID_W + tm]
        y = jnp.concatenate([yh, yv], axis=1) + cbias
        y_scr[...] = (bg_ref[...].astype(f32) * y).astype(bf16)
        m_scr[...] = ml_ref[...]

    def ctx_branch():
        z = cg_ref[...].astype(f32) * u_ref[...].astype(f32)
        pos = row % CTX_LEN
        prev, nxt = shifted(z, pos == 0, pos == CTX_LEN - 1)
        y = cw[0:1, :] * prev + cw[1:2, :] * z + cw[2:3, :] * nxt + cbias
        y_scr[...] = (bg_ref[...].astype(f32) * y).astype(bf16)
        m_scr[...] = mc_ref[...]

    if has_ctx:
        pl.when(i < n_lat_blocks)(lat_branch)
        pl.when(i >= n_lat_blocks)(ctx_branch)
    else:
        lat_branch()

    out = jnp.dot(m_scr[...], wo_ref[0:D_MLSTM, :], preferred_element_type=f32) \
        + jnp.dot(y_scr[...], wo_ref[D_MLSTM:, :], preferred_element_type=f32)
    xn = _ln(ALPHA * x_ref[...] + gate * out) * g1_ref[...] + b1_ref[...]
    xo_ref[...] = xn
    h2 = _ln(xn) * (1.0 + scale2) + shift2

    hp_ref[...] = h2

    h_hi = h2.astype(bf16)
    hs = jnp.concatenate([h_hi, (h2 - h_hi.astype(f32)).astype(bf16)], axis=0)
    pr = jnp.dot(hs, wr_ref[...], preferred_element_type=f32)
    logits = pr[:tm, :128] + (pr[:tm, 128:] + pr[tm:, :128])
    lt = logits.T
    s = _sigmoid(lt[0:N_EXPERTS, :])
    sb = s + br_ref[...]
    sb_rows = [sb[e:e + 1, :] for e in range(N_EXPERTS)]
    s_rows = [s[e:e + 1, :] for e in range(N_EXPERTS)]
    gscores = []
    for g in range(N_GROUPS):
        a_, b_, c_, d_ = sb_rows[EPG * g:EPG * g + EPG]
        hi1, lo1 = jnp.maximum(a_, b_), jnp.minimum(a_, b_)
        hi2, lo2 = jnp.maximum(c_, d_), jnp.minimum(c_, d_)
        top = jnp.maximum(hi1, hi2)
        second = jnp.maximum(jnp.minimum(hi1, hi2), jnp.maximum(lo1, lo2))
        gscores.append(top + second)
    gbest = gscores[0]
    gsel = jnp.zeros(gbest.shape, i32)
    for g in range(1, N_GROUPS):
        take = gscores[g] > gbest
        gbest = jnp.where(take, gscores[g], gbest)
        gsel = jnp.where(take, g, gsel)

    def pick_group(rows, j):
        v = rows[j]
        for g in range(1, N_GROUPS):
            v = jnp.where(gsel == g, rows[EPG * g + j], v)
        return v

    cand_b = [pick_group(sb_rows, j) for j in range(EPG)]
    cand_s = [pick_group(s_rows, j) for j in range(EPG)]
    i1, i2 = _top2_rows(cand_b)

    def pick_idx(rows, idx):
        v = rows[0]
        for j in range(1, EPG):
            v = jnp.where(idx == j, rows[j], v)
        return v

    s1 = pick_idx(cand_s, i1)
    s2 = pick_idx(cand_s, i2)
    tot = s1 + s2
    w1 = s1 / tot
    w2 = s2 / tot
    e1 = gsel * EPG + i1
    e2 = gsel * EPG + i2
    r8 = lax.broadcasted_iota(i32, (8, tm), 0)
    idx_ref[...] = jnp.where(r8 == 0, e1, jnp.where(r8 == 1, e2, 0))
    r128 = lax.broadcasted_iota(i32, (128, tm), 0)
    wmat = jnp.where(r128 == 0, w1, jnp.where(r128 == 1, w2, 0.0))
    wt_ref[...] = wmat.T


def _mix(x, mod_l, m_lat, m_ctx, proj, conv_w, conv_b, w_out, ln_g, ln_b, w_router, b_router, has_ctx, l):
    tm = TM_MIX
    m_rows = N_ALL if has_ctx else N_LAT
    nlb = N_LAT // tm
    nblocks = m_rows // tm
    hb = tm // GRID_W
    n_hblocks = N_ALL // GRID_W
    if m_ctx is None:
        m_ctx = m_lat
    ncb = m_ctx.shape[0] // tm
    in_specs = [
        pl.BlockSpec((tm, D_MODEL), lambda i: (i, 0)),
        pl.BlockSpec((None, 6, D_MODEL), lambda i: (_mod_row(i, tm), 0, 0)),
        pl.BlockSpec((tm, D_MLSTM), lambda i: (jnp.minimum(i, nlb - 1), 0)),
        pl.BlockSpec((tm, D_MLSTM), lambda i: (jnp.clip(i - nlb, 0, ncb - 1), 0)),
        pl.BlockSpec((tm, D_CONV), lambda i: (i, 3)),
        pl.BlockSpec((tm, D_CONV), lambda i: (i, 4)),
        pl.BlockSpec((tm, D_CONV), lambda i: (i, 5)),
        pl.BlockSpec((GRID_W, D_CONV_H), lambda i: (jnp.maximum(i * hb - 1, 0), 7)),
        pl.BlockSpec((GRID_W, D_CONV_H), lambda i: (jnp.maximum(i * hb - 1, 0), 11)),
        pl.BlockSpec((GRID_W, D_CONV_H), lambda i: (jnp.minimum((i + 1) * hb, n_hblocks - 1), 7)),
        pl.BlockSpec((GRID_W, D_CONV_H), lambda i: (jnp.minimum((i + 1) * hb, n_hblocks - 1), 11)),
        pl.BlockSpec((3, D_CONV), lambda i: (0, 0)),
        pl.BlockSpec((1, D_CONV), lambda i: (0, 0)),
        pl.BlockSpec((None, D_MODEL, D_MODEL), lambda i: (l, 0, 0)),
        pl.BlockSpec((1, D_MODEL), lambda i: (0, 0)),
        pl.BlockSpec((1, D_MODEL), lambda i: (0, 0)),
        pl.BlockSpec((D_MODEL, 256), lambda i: (0, 0)),
        pl.BlockSpec((N_EXPERTS, 1), lambda i: (0, 0)),
    ]
    out_specs = [
        pl.BlockSpec((tm, D_MODEL), lambda i: (i, 0)),
        pl.BlockSpec((tm, D_MODEL), lambda i: (i, 0)),
        pl.BlockSpec((8, tm), lambda i: (0, i)),
        pl.BlockSpec((tm, 128), lambda i: (i, 0)),
    ]
    out_shape = [
        jax.ShapeDtypeStruct((m_rows, D_MODEL), f32),
        jax.ShapeDtypeStruct((m_rows, D_MODEL), f32),
        jax.ShapeDtypeStruct((8, m_rows), i32),
        jax.ShapeDtypeStruct((m_rows, 128), f32),
    ]
    return pl.pallas_call(
        functools.partial(_mix_kernel, n_lat_blocks=nlb, has_ctx=has_ctx),
        grid=(nblocks,),
        in_specs=in_specs,
        out_specs=out_specs,
        out_shape=out_shape,
        scratch_shapes=[pltpu.VMEM((tm, D_MLSTM), bf16), pltpu.VMEM((tm, D_CONV), bf16)],
        compiler_params=_params(("parallel",)),
        name="mix",
    )(x, mod_l, m_lat, m_ctx, proj, proj, proj, proj, proj, proj, proj,
      conv_w, conv_b, w_out, ln_g, ln_b, w_router, b_router)


def _route_kernel(ef_ref, pos_ref, meta_ref, *, rows):
    R = rows
    ef = ef_ref[...]
    li = lax.broadcasted_iota(i32, (128, 128), 0)
    lj = lax.broadcasted_iota(i32, (128, 128), 1)
    strict_up = (li < lj).astype(bf16)
    ri = lax.broadcasted_iota(i32, (R, R), 0)
    rj = lax.broadcasted_iota(i32, (R, R), 1)
    strict_lo = (rj < ri).astype(bf16)
    lane16 = lax.broadcasted_iota(i32, (R, N_EXPERTS), 1)

    ohs = [(ef == e) for e in range(N_EXPERTS)]
    within = [jnp.dot(oh.astype(bf16), strict_up, preferred_element_type=f32) for oh in ohs]
    rt = jnp.zeros((R, N_EXPERTS), f32)
    for e in range(N_EXPERTS):
        rt = jnp.where(lane16 == e, jnp.sum(ohs[e].astype(f32), axis=1, keepdims=True), rt)
    rp = jnp.dot(strict_lo, rt.astype(bf16), preferred_element_type=f32)
    counts = rp[R - 1:R, :] + rt[R - 1:R, :]
    ntile = jnp.floor((counts + (TM_EXP - 1)) * (1.0 / TM_EXP))
    ei = lax.broadcasted_iota(i32, (N_EXPERTS, N_EXPERTS), 0)
    ej = lax.broadcasted_iota(i32, (N_EXPERTS, N_EXPERTS), 1)
    tend = jnp.dot(ntile.astype(bf16), (ei <= ej).astype(bf16), preferred_element_type=f32)
    off = rp + (tend - ntile) * float(TM_EXP)
    pos = jnp.zeros((R, 128), f32)
    for e in range(N_EXPERTS):
        pos = jnp.where(ohs[e], within[e] + off[:, e:e + 1], pos)
    pos_ref[...] = pos.astype(i32)

    n_used = tend[:, N_EXPERTS - 1:N_EXPERTS]
    tile = jnp.minimum(lax.broadcasted_iota(i32, (1, 128), 1).astype(f32), n_used - 1.0)
    te = jnp.zeros((1, 128), f32)
    for e in range(N_EXPERTS):
        te = te + (tend[:, e:e + 1] <= tile).astype(f32)
    te = jnp.minimum(te, float(N_EXPERTS - 1))
    r8 = lax.broadcasted_iota(i32, (8, 128), 0)
    meta_ref[...] = jnp.where(r8 == 0, te, jnp.where(r8 == 1, n_used, 0.0)).astype(i32)


def _route(idx8, m_rows):
    rows = 2 * m_rows // 128
    n_tiles = 2 * m_rows // TM_EXP + N_EXPERTS
    assert n_tiles <= 128
    pos, meta = pl.pallas_call(
        functools.partial(_route_kernel, rows=rows),
        out_shape=[jax.ShapeDtypeStruct((rows, 128), i32), jax.ShapeDtypeStruct((8, 128), i32)],
        name="route",
    )(idx8[:2, :].reshape(rows, 128))
    return pos.reshape(-1), meta[0, :n_tiles], meta[1, 0:1]


def _expert_kernel(te_ref, nu_ref, pos_ref, hp_ref, w1_ref, w3_ref, w2_ref, y_ref, xbuf, sem, src_s, *,
                   m_rows):
    tm = TM_EXP
    i = pl.program_id(0)
    n_used = nu_ref[0]
    n_rows = src_s.shape[0]

    def issue(tile, slot):
        base = tile * tm
        for j in range(tm):
            t = src_s[base + j]
            pltpu.make_async_copy(hp_ref.at[pl.ds(t, 1), :], xbuf.at[slot, pl.ds(j, 1), :],
                                  sem.at[slot]).start()

    def wait(slot):
        pltpu.make_async_copy(hp_ref.at[pl.ds(0, tm), :], xbuf.at[slot], sem.at[slot]).wait()

    @pl.when(i == 0)
    def _():
        def clear(p, c):
            src_s[p] = 0
            return c

        lax.fori_loop(0, n_rows, clear, 0, unroll=16)

        def scatter(t, c):
            src_s[pos_ref[t]] = t
            src_s[pos_ref[m_rows + t]] = t
            return c

        lax.fori_loop(0, m_rows, scatter, 0, unroll=8)
        issue(0, 0)

    @pl.when(i < n_used)
    def _():
        slot = i % 2
        issue(i + 1, 1 - slot)
        wait(slot)
        xb = xbuf[slot].astype(bf16)
        a1 = jnp.dot(xb, w1_ref[...], preferred_element_type=f32)
        a3 = jnp.dot(xb, w3_ref[...], preferred_element_type=f32)
        act = (a1 * _sigmoid(a1) * a3).astype(bf16)
        y_ref[...] = jnp.dot(act, w2_ref[...], preferred_element_type=f32)

        @pl.when(i + 1 == n_used)
        def _():
            wait(1 - slot)

    @pl.when(i >= n_used)
    def _():
        y_ref[...] = jnp.zeros_like(y_ref)


def _experts(te, n_used, pos, hp, w1, w3, w2):
    tm = TM_EXP
    m_rows = hp.shape[0]
    n_tiles = te.shape[0]
    wspec = lambda r, c: pl.BlockSpec((None, r, c), lambda i, te, nu, pos: (te[i], 0, 0))
    return pl.pallas_call(
        functools.partial(_expert_kernel, m_rows=m_rows),
        grid_spec=pltpu.PrefetchScalarGridSpec(
            num_scalar_prefetch=3,
            grid=(n_tiles,),
            in_specs=[
                pl.BlockSpec(memory_space=pl.ANY),
                wspec(D_MODEL, D_FF),
                wspec(D_MODEL, D_FF),
                wspec(D_FF, D_MODEL),
            ],
            out_specs=pl.BlockSpec((tm, D_MODEL), lambda i, te, nu, pos: (i, 0)),
            scratch_shapes=[pltpu.VMEM((2, tm, D_MODEL), f32), pltpu.SemaphoreType.DMA((2,)),
                            pltpu.SMEM((n_tiles * tm,), i32)],
        ),
        out_shape=jax.ShapeDtypeStruct((n_tiles * tm, D_MODEL), f32),
        compiler_params=_params(("arbitrary",)),
        name="experts",
    )(te, n_used, pos, hp, w1, w3, w2)


def _combine_kernel(pos_ref, x_ref, mod_ref, wt_ref, g_ref, b_ref, *rest, m_rows, nblocks, has_next):
    if has_next:
        modn_ref, y_ref, o_ref, h_ref, buf, sem = rest
    else:
        y_ref, o_ref, buf, sem = rest
    tm = TM_CMB
    i = pl.program_id(0)

    def issue(blk, slot):
        base = blk * tm
        for j in range(tm):
            for k in range(2):
                p = pos_ref[k * m_rows + base + j]
                pltpu.make_async_copy(y_ref.at[pl.ds(p, 1), :], buf.at[slot, k, pl.ds(j, 1), :],
                                      sem.at[slot]).start()

    @pl.when(i == 0)
    def _():
        issue(0, 0)

    slot = i % 2
    for k in range(2):
        pltpu.make_async_copy(y_ref.at[pl.ds(0, tm), :], buf.at[slot, k], sem.at[slot]).wait()

    wt = wt_ref[...]
    moe = wt[:, 0:1] * buf[slot, 0] + wt[:, 1:2] * buf[slot, 1]
    gate = mod_ref[5:6, :]
    v = ALPHA * x_ref[...] + gate * moe

    issue(jnp.minimum(i + 1, nblocks - 1), 1 - slot)
    xn = _ln(v) * g_ref[...] + b_ref[...]
    o_ref[...] = xn
    if has_next:
        h_ref[...] = _modulated(xn, modn_ref)

    @pl.when(i + 1 == nblocks)
    def _():
        for k in range(2):
            pltpu.make_async_copy(y_ref.at[pl.ds(0, tm), :], buf.at[1 - slot, k], sem.at[1 - slot]).wait()


def _combine(pos, x, mod_l, wts, ln_g, ln_b, y, mod_next=None):
    tm = TM_CMB
    m_rows = x.shape[0]
    nblocks = m_rows // tm
    has_next = mod_next is not None
    mod_spec = pl.BlockSpec((None, 6, D_MODEL), lambda i, p: (_mod_row(i, tm), 0, 0))
    row_spec = pl.BlockSpec((tm, D_MODEL), lambda i, p: (i, 0))
    vec_spec = pl.BlockSpec((1, D_MODEL), lambda i, p: (0, 0))
    in_specs = [row_spec, mod_spec, pl.BlockSpec((tm, 128), lambda i, p: (i, 0)), vec_spec, vec_spec]
    args = [pos, x, mod_l, wts, ln_g, ln_b]
    out_specs = [row_spec]
    out_shape = [jax.ShapeDtypeStruct((m_rows, D_MODEL), f32)]
    if has_next:
        in_specs.append(mod_spec)
        args.append(mod_next)
        out_specs.append(row_spec)
        out_shape.append(jax.ShapeDtypeStruct((m_rows, D_MODEL), bf16))
    in_specs.append(pl.BlockSpec(memory_space=pl.ANY))
    args.append(y)
    res = pl.pallas_call(
        functools.partial(_combine_kernel, m_rows=m_rows, nblocks=nblocks, has_next=has_next),
        grid_spec=pltpu.PrefetchScalarGridSpec(
            num_scalar_prefetch=1,
            grid=(nblocks,),
            in_specs=in_specs,
            out_specs=out_specs,
            scratch_shapes=[pltpu.VMEM((2, 2, tm, D_MODEL), f32), pltpu.SemaphoreType.DMA((2,))],
        ),
        out_shape=out_shape,
        compiler_params=_params(("arbitrary",)),
        name="combine",
    )(*args)
    return (res[0], res[1]) if has_next else (res[0], None)


def kernel(x, c, ctx, c_ctx, w_ada, b_ada, w_in, b_igate, b_fgate, mh_norm_g, conv_w, conv_b, w_out,
           ln1_g, ln1_b, w_router, b_router, w1, w3, w2, ln2_g, ln2_b):
    cond_raw = jnp.zeros((8, D_MODEL), f32).at[:BATCH].set(c).at[BATCH].set(c_ctx)
    mod = _ada(cond_raw, w_ada, b_ada).reshape(DEPTH, 8, 6, D_MODEL)

    xa, h = _prep(x.reshape(N_LAT, D_MODEL), ctx.reshape(N_CTX, D_MODEL), mod[0])
    w_router_p = jnp.zeros((D_MODEL, 128), f32).at[:, :N_EXPERTS].set(w_router)
    w_router_hi = w_router_p.astype(bf16)
    w_router_lo = (w_router_p - w_router_hi.astype(f32)).astype(bf16)
    w_router_p = jnp.concatenate([w_router_hi, w_router_lo], axis=1)
    b_router_c = b_router.reshape(N_EXPERTS, 1)
    w_in_t = jnp.swapaxes(w_in, 1, 2)
    w_outb = w_out.astype(bf16)

    for l in range(DEPTH):
        last = l == DEPTH - 1
        proj, gates = _inproj(h, w_in_t, l)

        bias = jnp.concatenate([b_igate[l], b_fgate[l]]).astype(f32)
        m_lat, m_ctx, (w1b, w3b, w2b) = _mlstm3(proj, gates, bias, mh_norm_g[l].reshape(1, D_MLSTM),
                                                w1, w3, w2, not last, l)

        xn, hp, idx8, wts = _mix(xa, mod[l], m_lat, m_ctx, proj, conv_w[l], conv_b[l].reshape(1, D_CONV),
                                 w_outb, ln1_g[l].reshape(1, D_MODEL),
                                 ln1_b[l].reshape(1, D_MODEL), w_router_p, b_router_c, not last, l)

        pos, te, n_used = _route(idx8, xn.shape[0])
        y = _experts(te, n_used, pos, hp, w1b, w3b, w2b)
        xa, h = _combine(pos, xn, mod[l], wts, ln2_g[l].reshape(1, D_MODEL), ln2_b[l].reshape(1, D_MODEL), y,
                         None if last else mod[l + 1])

    return xa.reshape(BATCH, SEQ, D_MODEL)
```

```python
import functools

import jax
import jax.numpy as jnp
from jax import lax
from jax.experimental import pallas as pl
from jax.experimental.pallas import tpu as pltpu

f32 = jnp.float32
bf16 = jnp.bfloat16
i32 = jnp.int32

D_MODEL = 2048
BATCH = 4
SEQ = 2048
DEPTH = 4
GRID_W = 64
CTX_LEN = 256
D_MLSTM = 1024
HEADS = 4
DV = 256
DQK = 128
D_CONV = 1024
D_CONV_H = 512
N_EXPERTS = 16
N_GROUPS = 4
EPG = 4
D_FF = 1024
ALPHA = (2 * DEPTH) ** 0.25
LN_EPS = 1e-6
QK_SCALE = DQK ** -0.5

N_LAT = BATCH * SEQ
N_CTX = BATCH * CTX_LEN
N_ALL = N_LAT + N_CTX
D_PROJ = 6144
N_GATE = 16

TM_IN = 3072
TN_IN = 512
TM_MIX = 256
TM_EXP = 256
TM_CMB = 256
VMEM_LIMIT = 56 * 1024 * 1024

HIGHEST = lax.Precision.HIGHEST


def _sigmoid(x):
    return 1.0 / (1.0 + jnp.exp(-x))


def _log_sigmoid(x):
    return jnp.minimum(x, 0.0) - jnp.log1p(jnp.exp(-jnp.abs(x)))


def _ln(x):
    mu = jnp.mean(x, axis=-1, keepdims=True)
    xc = x - mu
    var = jnp.mean(xc * xc, axis=-1, keepdims=True)
    return xc * lax.rsqrt(var + LN_EPS)


def _mod_row(i, tm):
    return jnp.minimum((i * tm) // SEQ, BATCH)


def _params(sem, vmem=VMEM_LIMIT):
    return pltpu.CompilerParams(dimension_semantics=sem, vmem_limit_bytes=vmem)


def _ada_kernel(c_ref, w_ref, b_ref, o_ref):
    c = c_ref[...]
    cond = c * _sigmoid(c)
    o_ref[...] = jnp.dot(cond, w_ref[...], preferred_element_type=f32) + b_ref[...]


def _ada(cond_raw, w_ada, b_ada):
    tn = 1024
    n = 6 * D_MODEL
    return pl.pallas_call(
        _ada_kernel,
        grid=(DEPTH, n // tn),
        in_specs=[
            pl.BlockSpec((8, D_MODEL), lambda l, j: (0, 0)),
            pl.BlockSpec((None, D_MODEL, tn), lambda l, j: (l, 0, j)),
            pl.BlockSpec((None, 1, tn), lambda l, j: (l, 0, j)),
        ],
        out_specs=pl.BlockSpec((None, 8, tn), lambda l, j: (l, 0, j)),
        out_shape=jax.ShapeDtypeStruct((DEPTH, 8, n), f32),
        compiler_params=_params(("parallel", "parallel")),
        name="ada",
    )(cond_raw, w_ada, b_ada.reshape(DEPTH, 1, n))


N_STATE = 2 * HEADS * DQK + D_MLSTM
N_STATE_BLOCKS = N_STATE // TN_IN
NT_DIMS = (((1,), (1,)), ((), ()))


def _modulated(x, mod_ref):
    return (_ln(x) * (1.0 + mod_ref[1:2, :]) + mod_ref[0:1, :]).astype(bf16)


def _inproj_kernel(h_ref, w_ref, wg_ref, o_ref, g_ref):
    @pl.when(pl.program_id(1) == 0)
    def _():
        g_ref[...] = lax.dot_general(h_ref[...], wg_ref[...].astype(bf16), NT_DIMS,
                                     preferred_element_type=f32)

    o_ref[...] = lax.dot_general(h_ref[...], w_ref[0].astype(bf16), NT_DIMS,
                                 preferred_element_type=f32).astype(o_ref.dtype)


def _inproj(h, w_in_t, l):
    m = h.shape[0]
    row0 = lambda j: pl.multiple_of(jnp.where(j < N_STATE_BLOCKS, j * TN_IN, j * TN_IN + N_GATE), 8)
    return pl.pallas_call(
        _inproj_kernel,
        grid=(m // TM_IN, D_PROJ // TN_IN),
        in_specs=[
            pl.BlockSpec((TM_IN, D_MODEL), lambda i, j: (i, 0)),
            pl.BlockSpec((pl.Element(1), pl.Element(TN_IN), pl.Element(D_MODEL)),
                         lambda i, j: (l, row0(j), 0)),
            pl.BlockSpec((None, N_GATE, D_MODEL), lambda i, j: (l, N_STATE // N_GATE, 0)),
        ],
        out_specs=[
            pl.BlockSpec((TM_IN, TN_IN), lambda i, j: (i, j)),
            pl.BlockSpec((TM_IN, N_GATE), lambda i, j: (i, 0)),
        ],
        out_shape=[
            jax.ShapeDtypeStruct((m, D_PROJ), bf16),
            jax.ShapeDtypeStruct((m, N_GATE), f32),
        ],
        compiler_params=_params(("parallel", "arbitrary")),
        name="inproj",
    )(h, w_in_t, w_in_t)


def _prep_kernel(x_ref, c_ref, mod_ref, xa_ref, h_ref, *, n_lat_blocks):
    i = pl.program_id(0)

    @pl.when(i < n_lat_blocks)
    def _():
        xa_ref[...] = x_ref[...]

    @pl.when(i >= n_lat_blocks)
    def _():
        xa_ref[...] = c_ref[...]

    h_ref[...] = _modulated(xa_ref[...], mod_ref)


def _prep(x2, c2, mod_l):
    tm = TM_CMB
    nlb = N_LAT // tm
    return pl.pallas_call(
        functools.partial(_prep_kernel, n_lat_blocks=nlb),
        grid=(N_ALL // tm,),
        in_specs=[
            pl.BlockSpec((tm, D_MODEL), lambda i: (jnp.minimum(i, nlb - 1), 0)),
            pl.BlockSpec((tm, D_MODEL), lambda i: (jnp.maximum(i - nlb, 0), 0)),
            pl.BlockSpec((None, 6, D_MODEL), lambda i: (_mod_row(i, tm), 0, 0)),
        ],
        out_specs=[pl.BlockSpec((tm, D_MODEL), lambda i: (i, 0)),
                   pl.BlockSpec((tm, D_MODEL), lambda i: (i, 0))],
        out_shape=[jax.ShapeDtypeStruct((N_ALL, D_MODEL), f32),
                   jax.ShapeDtypeStruct((N_ALL, D_MODEL), bf16)],
        compiler_params=_params(("parallel",)),
        name="prep",
    )(x2, c2, mod_l)


MCHUNK = 128
N_CHUNK_CTX = CTX_LEN // MCHUNK
N_CHUNK_LAT = SEQ // MCHUNK
N_CHUNK = N_CHUNK_CTX + N_CHUNK_LAT
GROUP = N_CHUNK_CTX
N_GROUP_LAT = N_CHUNK_LAT // GROUP
LAT_ROW0 = 8
D_AUG = DV + 128


CVT_STEPS = 2 * N_GROUP_LAT
CVT_ROWS_IN = D_MODEL // CVT_STEPS
CVT_ROWS_OUT = D_FF // CVT_STEPS


def _mlstm3_kernel(bias_ref, ql_ref, kl_ref, vl_ref, ol_ref, qc_ref, kc_ref, vc_ref, oc_ref,
                   gcl_ref, grl_ref, gcc_ref, grc_ref, gain_ref, w1_hbm, w3_hbm, w2_hbm, *rest,
                   has_ctx_out, layer):
    if has_ctx_out:
        ml_ref, mc_ref = rest[:2]
        rest = rest[2:]
    else:
        ml_ref, mc_ref = rest[0], None
        rest = rest[1:]
    w1b_hbm, w3b_hbm, w2b_hbm, s_ref, st_ref, msc, rows_s, cols_s = rest[:8]
    cin = rest[8:11]
    cout = rest[11:14]
    csem_in, csem_out = rest[14:16]
    L = MCHUNK
    b = pl.program_id(0)
    h = pl.program_id(1)
    expert = b * HEADS + h
    cvt_src = (w1_hbm, w3_hbm, w2_hbm)
    cvt_dst = (w1b_hbm, w3b_hbm, w2b_hbm)
    cvt_rows = (CVT_ROWS_IN, CVT_ROWS_IN, CVT_ROWS_OUT)

    def cvt_in(k, t):
        r0 = pl.multiple_of(t * cvt_rows[k], cvt_rows[k])
        return pltpu.make_async_copy(cvt_src[k].at[layer, expert, pl.ds(r0, cvt_rows[k]), :],
                                     cin[k].at[t % 2], csem_in.at[k, t % 2])

    def cvt_out(k, t):
        r0 = pl.multiple_of(t * cvt_rows[k], cvt_rows[k])
        return pltpu.make_async_copy(cout[k].at[t % 2], cvt_dst[k].at[expert, pl.ds(r0, cvt_rows[k]), :],
                                     csem_out.at[k, t % 2])

    def cvt_step(t):
        for k in range(3):
            cvt_in(k, t).wait()

            @pl.when(t >= 2)
            def _():
                cvt_out(k, t - 2).wait()

            cout[k][t % 2] = cin[k][t % 2].astype(bf16)
            cvt_out(k, t).start()

            @pl.when(t + 2 < CVT_STEPS)
            def _():
                cvt_in(k, t + 2).start()

    for k in range(3):
        cvt_in(k, 0).start()
        cvt_in(k, 1).start()
    bi = (bias_ref[h], bias_ref[HEADS + h])
    bf = (bias_ref[2 * HEADS + h], bias_ref[3 * HEADS + h])

    rr = lax.broadcasted_iota(i32, (L, L), 0)
    cc = lax.broadcasted_iota(i32, (L, L), 1)
    lo_mask = rr >= cc
    up_mask = rr <= cc
    masks = (lo_mask, up_mask)

    def row_forms(gr_ref, sl, dst0, n):
        for d in range(2):
            tri = (up_mask if d == 0 else lo_mask).astype(f32)
            i_r = gr_ref[d, sl, :] + bi[d]
            lf_r = _log_sigmoid(gr_ref[2 + d, sl, :] + bf[d])
            b_r = jnp.dot(lf_r, tri, precision=HIGHEST, preferred_element_type=f32)
            rows_s[d, dst0:dst0 + n, :] = i_r - b_r

    row_forms(grc_ref, pl.ds(b * N_CHUNK_CTX, N_CHUNK_CTX), 0, N_CHUNK_CTX)
    row_forms(grl_ref, slice(None), LAT_ROW0, N_CHUNK_LAT)

    kind = lax.broadcasted_iota(i32, (1, 4 * GROUP), 1) % 4
    bias_v = jnp.where(kind == 0, bi[0], jnp.where(kind == 1, bi[1], jnp.where(kind == 2, bf[0], bf[1])))
    tpos = lax.broadcasted_iota(i32, (L, 4 * GROUP), 0)

    def col_forms(x):
        y = x + bias_v
        y = jnp.where(kind >= 2, _log_sigmoid(y), y)
        pre = y
        suf = y
        s = 1
        while s < L:
            pre = pre + jnp.where(tpos >= s, pltpu.roll(pre, s, 0), 0.0)
            suf = suf + jnp.where(tpos < L - s, pltpu.roll(suf, L - s, 0), 0.0)
            s *= 2
        return jnp.where(kind == 2, pre, jnp.where(kind == 3, suf, y))

    cols_s[0] = col_forms(gcc_ref[...])

    def col_body(g, c):
        cols_s[g + 1] = col_forms(gcl_ref[g])
        return c

    lax.fori_loop(0, N_GROUP_LAT, col_body, 0)

    ones_col = (lax.broadcasted_iota(i32, (L, 128), 1) == 0).astype(bf16)

    s_ref[...] = jnp.zeros_like(s_ref)

    def state_step(d, c, row, col, j, k, v, m):
        i_c = col[:, 4 * j + d:4 * j + d + 1]
        b1 = col[:, 4 * j + 2 + d:4 * j + 3 + d]
        last = L - 1 if d == 0 else 0
        b_last = b1[last:last + 1, :]
        ct = i_c - b1
        mx = jnp.max(ct, axis=0, keepdims=True)
        wl = jnp.exp(ct - mx)
        mloc_last = b_last + mx
        m_new = jnp.maximum(b_last + m, mloc_last)
        decay = jnp.exp(b_last + m - m_new)
        a_l = jnp.exp(mloc_last - m_new)
        msc[d, pl.ds(row, 1), :] = jnp.broadcast_to(m, (1, 128))
        kwt = (k.astype(f32) * (wl * QK_SCALE)).astype(bf16).T
        for cb in range(D_AUG // 128):
            cs = slice(cb * 128, (cb + 1) * 128)
            s_old = s_ref[d, :, cs]
            st_ref[d, c, :, cs] = s_old.astype(bf16)
            rhs = v[:, cs] if cb < DV // 128 else ones_col
            s_ref[d, :, cs] = decay * s_old + a_l * jnp.dot(kwt, rhs, preferred_element_type=f32)
        return m_new

    zero = jnp.zeros((1, 1), f32)
    m_f = m_b = zero
    col0 = cols_s[0]
    for step in range(GROUP):
        jf, jb = step, GROUP - 1 - step
        m_f = state_step(0, jf, jf, col0, jf, kc_ref[jf * L:(jf + 1) * L, :], vc_ref[jf * L:(jf + 1) * L, :], m_f)
        m_b = state_step(1, jb, jb, col0, jb, kc_ref[jb * L:(jb + 1) * L, :], vc_ref[jb * L:(jb + 1) * L, :], m_b)

    def state_body(it, carry):
        m_f, m_b = carry
        gf = it
        gb = N_GROUP_LAT + 1 - it
        colf = cols_s[gf]
        colb = cols_s[gb]
        for step in range(GROUP):
            jf, jb = step, GROUP - 1 - step
            clf = (gf - 1) * GROUP + jf
            clb = (gb - 1) * GROUP + jb
            rf = pl.multiple_of(clf * L, L)
            rb = pl.multiple_of(clb * L, L)
            m_f = state_step(0, N_CHUNK_CTX + clf, LAT_ROW0 + clf, colf, jf,
                             kl_ref[pl.ds(rf, L), :], vl_ref[pl.ds(rf, L), :], m_f)
            m_b = state_step(1, N_CHUNK_CTX + clb, LAT_ROW0 + clb, colb, jb,
                             kl_ref[pl.ds(rb, L), :], vl_ref[pl.ds(rb, L), :], m_b)
        cvt_step(it - 1)
        return m_f, m_b

    lax.fori_loop(1, N_GROUP_LAT + 1, state_body, (m_f, m_b))

    gain = gain_ref[...]

    def out_chunk(q, k, v, o, col, j, c, row, out_ref, r0):
        qk = lax.dot_general(q, k, (((1,), (1,)), ((), ())), preferred_element_type=f32) * QK_SCALE
        vaug = jnp.concatenate([v, ones_col], axis=1)
        sl = []
        per = []
        for d in range(2):
            b1 = col[:, 4 * j + 2 + d:4 * j + 3 + d]
            dm = jnp.where(masks[d], b1 + rows_s[d, pl.ds(row, 1), :], -jnp.inf)
            mloc = jnp.max(dm, axis=1, keepdims=True)
            wloc = jnp.exp(dm - mloc)
            m_prev = msc[d, pl.ds(row, 1), :][:, 0:1]
            m_t = jnp.maximum(b1 + m_prev, mloc)
            inter = jnp.exp(b1 + m_prev - m_t)
            a = jnp.exp(mloc - m_t)
            sl.append((qk * wloc).astype(bf16))
            per.append((m_t, inter, a))
        x = jnp.dot(jnp.concatenate(sl, axis=0), vaug, preferred_element_type=f32)
        hs = None
        for d in range(2):
            y = jnp.dot(q, st_ref[d, c], preferred_element_type=f32)
            m_t, inter, a = per[d]
            xd = x[d * L:(d + 1) * L, :]
            den = inter * y[:, DV:DV + 1] + a * xd[:, DV:DV + 1]
            rinv = 1.0 / jnp.maximum(jnp.abs(den), jnp.exp(-m_t))
            hd = (inter * rinv) * y[:, :DV] + (a * rinv) * xd[:, :DV]
            hs = hd if hs is None else hs + hd
        r = lax.rsqrt(jnp.mean(hs * hs, axis=-1, keepdims=True) + LN_EPS)
        out_ref[pl.ds(r0, L), :] = (hs * r * gain * _sigmoid(o.astype(f32))).astype(out_ref.dtype)

    if has_ctx_out:
        for j in range(GROUP):
            sl_ = slice(j * L, (j + 1) * L)
            out_chunk(qc_ref[sl_, :], kc_ref[sl_, :], vc_ref[sl_, :], oc_ref[sl_, :], col0, j, j, j,
                      mc_ref, j * L)

    def out_body(g2, carry):
        for gg in range(2):
            g = 2 * g2 + gg
            col = cols_s[g + 1]
            for j in range(GROUP):
                cl = g * GROUP + j
                r0 = pl.multiple_of(cl * L, L)
                out_chunk(ql_ref[pl.ds(r0, L), :], kl_ref[pl.ds(r0, L), :], vl_ref[pl.ds(r0, L), :],
                          ol_ref[pl.ds(r0, L), :], col, j, N_CHUNK_CTX + cl, LAT_ROW0 + cl, ml_ref, r0)
        cvt_step(N_GROUP_LAT + 2 * g2)
        cvt_step(N_GROUP_LAT + 2 * g2 + 1)
        return carry

    lax.fori_loop(0, N_GROUP_LAT // 2, out_body, 0)
    for k in range(3):
        cvt_out(k, CVT_STEPS - 2).wait()
        cvt_out(k, CVT_STEPS - 1).wait()


def _mlstm3(proj, gates, bias, gain, w1, w3, w2, has_ctx_out, layer):
    L = MCHUNK
    g = gates[:, :N_GATE]
    g4 = jnp.stack([g[:, 0:4], g[:, 4:8], g[:, 8:12], g[:, 12:16]], axis=-1)
    gh = jnp.transpose(g4, (1, 0, 2))
    gcol_l = gh[:, :N_LAT].reshape(HEADS, BATCH * N_GROUP_LAT, GROUP, L, 4)
    gcol_l = jnp.transpose(gcol_l, (0, 1, 3, 2, 4)).reshape(HEADS, BATCH * N_GROUP_LAT, L, 4 * GROUP)
    gcol_c = gh[:, N_LAT:].reshape(HEADS, BATCH, GROUP, L, 4)
    gcol_c = jnp.transpose(gcol_c, (0, 1, 3, 2, 4)).reshape(HEADS, BATCH, L, 4 * GROUP)
    gr = jnp.transpose(g4, (1, 2, 0))
    grow_l = gr[:, :, :N_LAT].reshape(HEADS, 4, N_LAT // L, L)
    grow_c = gr[:, :, N_LAT:].reshape(HEADS, 4, N_CTX // L, L)

    ctx_rb = lambda b, h: N_LAT // CTX_LEN + b
    in_specs = [
        pl.BlockSpec(memory_space=pltpu.SMEM),
        pl.BlockSpec((SEQ, DQK), lambda b, h: (b, h)),
        pl.BlockSpec((SEQ, DQK), lambda b, h: (b, HEADS + h)),
        pl.BlockSpec((SEQ, DV), lambda b, h: (b, HEADS + h)),
        pl.BlockSpec((SEQ, DV), lambda b, h: (b, 2 * HEADS + h)),
        pl.BlockSpec((CTX_LEN, DQK), lambda b, h: (ctx_rb(b, h), h)),
        pl.BlockSpec((CTX_LEN, DQK), lambda b, h: (ctx_rb(b, h), HEADS + h)),
        pl.BlockSpec((CTX_LEN, DV), lambda b, h: (ctx_rb(b, h), HEADS + h)),
        pl.BlockSpec((CTX_LEN, DV), lambda b, h: (ctx_rb(b, h), 2 * HEADS + h)),
        pl.BlockSpec((None, N_GROUP_LAT, L, 4 * GROUP), lambda b, h: (h, b, 0, 0)),
        pl.BlockSpec((None, 4, N_CHUNK_LAT, L), lambda b, h: (h, 0, b, 0)),
        pl.BlockSpec((None, None, L, 4 * GROUP), lambda b, h: (h, b, 0, 0)),
        pl.BlockSpec((None, 4, N_CTX // L, L), lambda b, h: (h, 0, 0, 0)),
        pl.BlockSpec((1, DV), lambda b, h: (0, h)),
        pl.BlockSpec(memory_space=pl.ANY),
        pl.BlockSpec(memory_space=pl.ANY),
        pl.BlockSpec(memory_space=pl.ANY),
    ]
    out_specs = [pl.BlockSpec((SEQ, DV), lambda b, h: (b, h))]
    out_shape = [jax.ShapeDtypeStruct((N_LAT, D_MLSTM), bf16)]
    if has_ctx_out:
        out_specs.append(pl.BlockSpec((CTX_LEN, DV), lambda b, h: (b, h)))
        out_shape.append(jax.ShapeDtypeStruct((N_CTX, D_MLSTM), bf16))
    out_specs += [pl.BlockSpec(memory_space=pl.ANY)] * 3
    out_shape += [jax.ShapeDtypeStruct((N_EXPERTS, D_MODEL, D_FF), bf16),
                  jax.ShapeDtypeStruct((N_EXPERTS, D_MODEL, D_FF), bf16),
                  jax.ShapeDtypeStruct((N_EXPERTS, D_FF, D_MODEL), bf16)]
    scratch = [
        pltpu.VMEM((2, DQK, D_AUG), f32),
        pltpu.VMEM((2, N_CHUNK, DQK, D_AUG), bf16),
        pltpu.VMEM((2, LAT_ROW0 + N_CHUNK_LAT, 128), f32),
        pltpu.VMEM((2, LAT_ROW0 + N_CHUNK_LAT, L), f32),
        pltpu.VMEM((N_GROUP_LAT + 1, L, 4 * GROUP), f32),
        pltpu.VMEM((2, CVT_ROWS_IN, D_FF), f32),
        pltpu.VMEM((2, CVT_ROWS_IN, D_FF), f32),
        pltpu.VMEM((2, CVT_ROWS_OUT, D_MODEL), f32),
        pltpu.VMEM((2, CVT_ROWS_IN, D_FF), bf16),
        pltpu.VMEM((2, CVT_ROWS_IN, D_FF), bf16),
        pltpu.VMEM((2, CVT_ROWS_OUT, D_MODEL), bf16),
        pltpu.SemaphoreType.DMA((3, 2)),
        pltpu.SemaphoreType.DMA((3, 2)),
    ]
    assert N_EXPERTS == BATCH * HEADS
    res = pl.pallas_call(
        functools.partial(_mlstm3_kernel, has_ctx_out=has_ctx_out, layer=layer),
        grid=(BATCH, HEADS),
        in_specs=in_specs,
        out_specs=out_specs,
        out_shape=out_shape,
        scratch_shapes=scratch,
        compiler_params=_params(("arbitrary", "arbitrary")),
        name="mlstm",
    )(bias, proj, proj, proj, proj, proj, proj, proj, proj, gcol_l, grow_l, gcol_c, grow_c, gain,
      w1, w3, w2)
    if has_ctx_out:
        return res[0], res[1], res[2:]
    return res[0], None, res[1:]


def _top2_rows(vals):
    best = vals[0]
    bi = jnp.zeros(best.shape, i32)
    for j in range(1, len(vals)):
        take = vals[j] > best
        best = jnp.where(take, vals[j], best)
        bi = jnp.where(take, j, bi)
    sec = None
    si = None
    for j in range(len(vals)):
        cand = jnp.where(bi == j, -jnp.inf, vals[j])
        if sec is None:
            sec, si = cand, jnp.zeros(best.shape, i32)
        else:
            take = cand > sec
            sec = jnp.where(take, cand, sec)
            si = jnp.where(take, j, si)
    return bi, si


def _mix_kernel(x_ref, mod_ref, ml_ref, mc_ref, u_ref, bg_ref, cg_ref, ut_ref, ct_ref, ub_ref, cb_ref,
                cw_ref, cbias_ref, wo_ref, g1_ref, b1_ref, wr_ref, br_ref,
                xo_ref, hp_ref, idx_ref, wt_ref, m_scr, y_scr, *, n_lat_blocks, has_ctx):
    tm = TM_MIX
    i = pl.program_id(0)
    gate = mod_ref[2:3, :]
    shift2 = mod_ref[3:4, :]
    scale2 = mod_ref[4:5, :]
    cw = cw_ref[...]
    cbias = cbias_ref[...]
    row = lax.broadcasted_iota(i32, (tm, 1), 0)

    def shifted(z, first, last):
        prev = jnp.where(first, 0.0, pltpu.roll(z, 1, 0))
        nxt = jnp.where(last, 0.0, pltpu.roll(z, tm - 1, 0))
        return prev, nxt

    def lat_branch():
        z = cg_ref[...].astype(f32) * u_ref[...].astype(f32)
        col = row % GRID_W
        zh = z[:, :D_CONV_H]
        prev, nxt = shifted(zh, col == 0, col == GRID_W - 1)
        yh = cw[0:1, :D_CONV_H] * prev + cw[1:2, :D_CONV_H] * zh + cw[2:3, :D_CONV_H] * nxt
        bpb = SEQ // tm
        top_ok = (i % bpb != 0).astype(f32)
        bot_ok = (i % bpb != bpb - 1).astype(f32)
        zt = ct_ref[...].astype(f32) * ut_ref[...].astype(f32) * top_ok
        zb = cb_ref[...].astype(f32) * ub_ref[...].astype(f32) * bot_ok
        zv = z[:, D_CONV_H:]
        zext = jnp.concatenate([zt, zv, zb], axis=0)
        yv = cw[0:1, D_CONV_H:] * zext[0:tm] + cw[1:2, D_CONV_H:] * zv \
            + cw[2:3, D_CONV_H:] * zext[2 * GRID_W:2 * GRID_W + tm]
        y = jnp.concatenate([yh, yv], axis=1) + cbias
        y_scr[...] = (bg_ref[...].astype(f32) * y).astype(bf16)
        m_scr[...] = ml_ref[...]

    def ctx_branch():
        z = cg_ref[...].astype(f32) * u_ref[...].astype(f32)
        pos = row % CTX_LEN
        prev, nxt = shifted(z, pos == 0, pos == CTX_LEN - 1)
        y = cw[0:1, :] * prev + cw[1:2, :] * z + cw[2:3, :] * nxt + cbias
        y_scr[...] = (bg_ref[...].astype(f32) * y).astype(bf16)
        m_scr[...] = mc_ref[...]

    if has_ctx:
        pl.when(i < n_lat_blocks)(lat_branch)
        pl.when(i >= n_lat_blocks)(ctx_branch)
    else:
        lat_branch()

    out = jnp.dot(m_scr[...], wo_ref[0:D_MLSTM, :], preferred_element_type=f32) \
        + jnp.dot(y_scr[...], wo_ref[D_MLSTM:, :], preferred_element_type=f32)
    xn = _ln(ALPHA * x_ref[...] + gate * out) * g1_ref[...] + b1_ref[...]
    xo_ref[...] = xn
    h2 = _ln(xn) * (1.0 + scale2) + shift2

    hp_ref[...] = h2

    h_hi = h2.astype(bf16)
    hs = jnp.concatenate([h_hi, (h2 - h_hi.astype(f32)).astype(bf16)], axis=0)
    pr = jnp.dot(hs, wr_ref[...], preferred_element_type=f32)
    logits = pr[:tm, :128] + (pr[:tm, 128:] + pr[tm:, :128])
    lt = logits.T
    s = _sigmoid(lt[0:N_EXPERTS, :])
    sb = s + br_ref[...]
    sb_rows = [sb[e:e + 1, :] for e in range(N_EXPERTS)]
    s_rows = [s[e:e + 1, :] for e in range(N_EXPERTS)]
    gscores = []
    for g in range(N_GROUPS):
        a_, b_, c_, d_ = sb_rows[EPG * g:EPG * g + EPG]
        hi1, lo1 = jnp.maximum(a_, b_), jnp.minimum(a_, b_)
        hi2, lo2 = jnp.maximum(c_, d_), jnp.minimum(c_, d_)
        top = jnp.maximum(hi1, hi2)
        second = jnp.maximum(jnp.minimum(hi1, hi2), jnp.maximum(lo1, lo2))
        gscores.append(top + second)
    gbest = gscores[0]
    gsel = jnp.zeros(gbest.shape, i32)
    for g in range(1, N_GROUPS):
        take = gscores[g] > gbest
        gbest = jnp.where(take, gscores[g], gbest)
        gsel = jnp.where(take, g, gsel)

    def pick_group(rows, j):
        v = rows[j]
        for g in range(1, N_GROUPS):
            v = jnp.where(gsel == g, rows[EPG * g + j], v)
        return v

    cand_b = [pick_group(sb_rows, j) for j in range(EPG)]
    cand_s = [pick_group(s_rows, j) for j in range(EPG)]
    i1, i2 = _top2_rows(cand_b)

    def pick_idx(rows, idx):
        v = rows[0]
        for j in range(1, EPG):
            v = jnp.where(idx == j, rows[j], v)
        return v

    s1 = pick_idx(cand_s, i1)
    s2 = pick_idx(cand_s, i2)
    tot = s1 + s2
    w1 = s1 / tot
    w2 = s2 / tot
    e1 = gsel * EPG + i1
    e2 = gsel * EPG + i2
    r8 = lax.broadcasted_iota(i32, (8, tm), 0)
    idx_ref[...] = jnp.where(r8 == 0, e1, jnp.where(r8 == 1, e2, 0))
    r128 = lax.broadcasted_iota(i32, (128, tm), 0)
    wmat = jnp.where(r128 == 0, w1, jnp.where(r128 == 1, w2, 0.0))
    wt_ref[...] = wmat.T


def _mix(x, mod_l, m_lat, m_ctx, proj, conv_w, conv_b, w_out, ln_g, ln_b, w_router, b_router, has_ctx, l):
    tm = TM_MIX
    m_rows = N_ALL if has_ctx else N_LAT
    nlb = N_LAT // tm
    nblocks = m_rows // tm
    hb = tm // GRID_W
    n_hblocks = N_ALL // GRID_W
    if m_ctx is None:
        m_ctx = m_lat
    ncb = m_ctx.shape[0] // tm
    in_specs = [
        pl.BlockSpec((tm, D_MODEL), lambda i: (i, 0)),
        pl.BlockSpec((None, 6, D_MODEL), lambda i: (_mod_row(i, tm), 0, 0)),
        pl.BlockSpec((tm, D_MLSTM), lambda i: (jnp.minimum(i, nlb - 1), 0)),
        pl.BlockSpec((tm, D_MLSTM), lambda i: (jnp.clip(i - nlb, 0, ncb - 1), 0)),
        pl.BlockSpec((tm, D_CONV), lambda i: (i, 3)),
        pl.BlockSpec((tm, D_CONV), lambda i: (i, 4)),
        pl.BlockSpec((tm, D_CONV), lambda i: (i, 5)),
        pl.BlockSpec((GRID_W, D_CONV_H), lambda i: (jnp.maximum(i * hb - 1, 0), 7)),
        pl.BlockSpec((GRID_W, D_CONV_H), lambda i: (jnp.maximum(i * hb - 1, 0), 11)),
        pl.BlockSpec((GRID_W, D_CONV_H), lambda i: (jnp.minimum((i + 1) * hb, n_hblocks - 1), 7)),
        pl.BlockSpec((GRID_W, D_CONV_H), lambda i: (jnp.minimum((i + 1) * hb, n_hblocks - 1), 11)),
        pl.BlockSpec((3, D_CONV), lambda i: (0, 0)),
        pl.BlockSpec((1, D_CONV), lambda i: (0, 0)),
        pl.BlockSpec((None, D_MODEL, D_MODEL), lambda i: (l, 0, 0)),
        pl.BlockSpec((1, D_MODEL), lambda i: (0, 0)),
        pl.BlockSpec((1, D_MODEL), lambda i: (0, 0)),
        pl.BlockSpec((D_MODEL, 256), lambda i: (0, 0)),
        pl.BlockSpec((N_EXPERTS, 1), lambda i: (0, 0)),
    ]
    out_specs = [
        pl.BlockSpec((tm, D_MODEL), lambda i: (i, 0)),
        pl.BlockSpec((tm, D_MODEL), lambda i: (i, 0)),
        pl.BlockSpec((8, tm), lambda i: (0, i)),
        pl.BlockSpec((tm, 128), lambda i: (i, 0)),
    ]
    out_shape = [
        jax.ShapeDtypeStruct((m_rows, D_MODEL), f32),
        jax.ShapeDtypeStruct((m_rows, D_MODEL), f32),
        jax.ShapeDtypeStruct((8, m_rows), i32),
        jax.ShapeDtypeStruct((m_rows, 128), f32),
    ]
    return pl.pallas_call(
        functools.partial(_mix_kernel, n_lat_blocks=nlb, has_ctx=has_ctx),
        grid=(nblocks,),
        in_specs=in_specs,
        out_specs=out_specs,
        out_shape=out_shape,
        scratch_shapes=[pltpu.VMEM((tm, D_MLSTM), bf16), pltpu.VMEM((tm, D_CONV), bf16)],
        compiler_params=_params(("parallel",)),
        name="mix",
    )(x, mod_l, m_lat, m_ctx, proj, proj, proj, proj, proj, proj, proj,
      conv_w, conv_b, w_out, ln_g, ln_b, w_router, b_router)


def _route_kernel(ef_ref, pos_ref, meta_ref, *, rows):
    R = rows
    ef = ef_ref[...]
    li = lax.broadcasted_iota(i32, (128, 128), 0)
    lj = lax.broadcasted_iota(i32, (128, 128), 1)
    strict_up = (li < lj).astype(bf16)
    ri = lax.broadcasted_iota(i32, (R, R), 0)
    rj = lax.broadcasted_iota(i32, (R, R), 1)
    strict_lo = (rj < ri).astype(bf16)
    lane16 = lax.broadcasted_iota(i32, (R, N_EXPERTS), 1)

    ohs = [(ef == e) for e in range(N_EXPERTS)]
    within = [jnp.dot(oh.astype(bf16), strict_up, preferred_element_type=f32) for oh in ohs]
    rt = jnp.zeros((R, N_EXPERTS), f32)
    for e in range(N_EXPERTS):
        rt = jnp.where(lane16 == e, jnp.sum(ohs[e].astype(f32), axis=1, keepdims=True), rt)
    rp = jnp.dot(strict_lo, rt.astype(bf16), preferred_element_type=f32)
    counts = rp[R - 1:R, :] + rt[R - 1:R, :]
    ntile = jnp.floor((counts + (TM_EXP - 1)) * (1.0 / TM_EXP))
    ei = lax.broadcasted_iota(i32, (N_EXPERTS, N_EXPERTS), 0)
    ej = lax.broadcasted_iota(i32, (N_EXPERTS, N_EXPERTS), 1)
    tend = jnp.dot(ntile.astype(bf16), (ei <= ej).astype(bf16), preferred_element_type=f32)
    off = rp + (tend - ntile) * float(TM_EXP)
    pos = jnp.zeros((R, 128), f32)
    for e in range(N_EXPERTS):
        pos = jnp.where(ohs[e], within[e] + off[:, e:e + 1], pos)
    pos_ref[...] = pos.astype(i32)

    n_used = tend[:, N_EXPERTS - 1:N_EXPERTS]
    tile = jnp.minimum(lax.broadcasted_iota(i32, (1, 128), 1).astype(f32), n_used - 1.0)
    te = jnp.zeros((1, 128), f32)
    for e in range(N_EXPERTS):
        te = te + (tend[:, e:e + 1] <= tile).astype(f32)
    te = jnp.minimum(te, float(N_EXPERTS - 1))
    r8 = lax.broadcasted_iota(i32, (8, 128), 0)
    meta_ref[...] = jnp.where(r8 == 0, te, jnp.where(r8 == 1, n_used, 0.0)).astype(i32)


def _route(idx8, m_rows):
    rows = 2 * m_rows // 128
    n_tiles = 2 * m_rows // TM_EXP + N_EXPERTS
    assert n_tiles <= 128
    pos, meta = pl.pallas_call(
        functools.partial(_route_kernel, rows=rows),
        out_shape=[jax.ShapeDtypeStruct((rows, 128), i32), jax.ShapeDtypeStruct((8, 128), i32)],
        name="route",
    )(idx8[:2, :].reshape(rows, 128))
    return pos.reshape(-1), meta[0, :n_tiles], meta[1, 0:1]


def _expert_kernel(te_ref, nu_ref, pos_ref, hp_ref, zeros_ref, w1_ref, w3_ref, w2_ref, y_ref, xbuf, sem,
                   src_s, zsem, *, m_rows):
    tm = TM_EXP
    i = pl.program_id(0)
    n_used = nu_ref[0]
    n_rows = src_s.shape[0]

    def issue(tile, slot):
        base = tile * tm
        for j in range(tm):
            t = src_s[base + j]
            pltpu.make_async_copy(hp_ref.at[pl.ds(t, 1), :], xbuf.at[slot, pl.ds(j, 1), :],
                                  sem.at[slot]).start()

    def wait(slot):
        pltpu.make_async_copy(hp_ref.at[pl.ds(0, tm), :], xbuf.at[slot], sem.at[slot]).wait()

    @pl.when(i == 0)
    def _():
        zfill = pltpu.make_async_copy(zeros_ref, src_s, zsem)
        zfill.start()
        zfill.wait()

        def scatter(t, c):
            src_s[pos_ref[t]] = t
            src_s[pos_ref[m_rows + t]] = t
            return c

        lax.fori_loop(0, m_rows, scatter, 0, unroll=8)
        issue(0, 0)

    @pl.when(i < n_used)
    def _():
        slot = i % 2
        issue(i + 1, 1 - slot)
        wait(slot)
        xb = xbuf[slot].astype(bf16)
        a1 = jnp.dot(xb, w1_ref[...], preferred_element_type=f32)
        a3 = jnp.dot(xb, w3_ref[...], preferred_element_type=f32)
        act = (a1 * _sigmoid(a1) * a3).astype(bf16)
        y_ref[...] = jnp.dot(act, w2_ref[...], preferred_element_type=f32)

        @pl.when(i + 1 == n_used)
        def _():
            wait(1 - slot)

    @pl.when(i >= n_used)
    def _():
        y_ref[...] = jnp.zeros_like(y_ref)


def _experts(te, n_used, pos, hp, w1, w3, w2):
    tm = TM_EXP
    m_rows = hp.shape[0]
    n_tiles = te.shape[0]
    wspec = lambda r, c: pl.BlockSpec((None, r, c), lambda i, te, nu, pos: (te[i], 0, 0))
    return pl.pallas_call(
        functools.partial(_expert_kernel, m_rows=m_rows),
        grid_spec=pltpu.PrefetchScalarGridSpec(
            num_scalar_prefetch=3,
            grid=(n_tiles,),
            in_specs=[
                pl.BlockSpec(memory_space=pl.ANY),
                pl.BlockSpec(memory_space=pl.ANY),
                wspec(D_MODEL, D_FF),
                wspec(D_MODEL, D_FF),
                wspec(D_FF, D_MODEL),
            ],
            out_specs=pl.BlockSpec((tm, D_MODEL), lambda i, te, nu, pos: (i, 0)),
            scratch_shapes=[pltpu.VMEM((2, tm, D_MODEL), f32), pltpu.SemaphoreType.DMA((2,)),
                            pltpu.SMEM((n_tiles * tm,), i32), pltpu.SemaphoreType.DMA(())],
        ),
        out_shape=jax.ShapeDtypeStruct((n_tiles * tm, D_MODEL), f32),
        compiler_params=_params(("arbitrary",)),
        name="experts",
    )(te, n_used, pos, hp, jnp.zeros((n_tiles * tm,), i32), w1, w3, w2)


def _combine_kernel(pos_ref, x_ref, mod_ref, wt_ref, g_ref, b_ref, *rest, m_rows, nblocks, has_next):
    if has_next:
        modn_ref, y_ref, o_ref, h_ref, buf, sem = rest
    else:
        y_ref, o_ref, buf, sem = rest
    tm = TM_CMB
    i = pl.program_id(0)

    def issue(blk, slot):
        base = blk * tm
        for j in range(tm):
            for k in range(2):
                p = pos_ref[k * m_rows + base + j]
                pltpu.make_async_copy(y_ref.at[pl.ds(p, 1), :], buf.at[slot, k, pl.ds(j, 1), :],
                                      sem.at[slot]).start()

    @pl.when(i == 0)
    def _():
        issue(0, 0)

    slot = i % 2
    for k in range(2):
        pltpu.make_async_copy(y_ref.at[pl.ds(0, tm), :], buf.at[slot, k], sem.at[slot]).wait()

    wt = wt_ref[...]
    moe = wt[:, 0:1] * buf[slot, 0] + wt[:, 1:2] * buf[slot, 1]
    gate = mod_ref[5:6, :]
    v = ALPHA * x_ref[...] + gate * moe

    issue(jnp.minimum(i + 1, nblocks - 1), 1 - slot)
    xn = _ln(v) * g_ref[...] + b_ref[...]
    o_ref[...] = xn
    if has_next:
        h_ref[...] = _modulated(xn, modn_ref)

    @pl.when(i + 1 == nblocks)
    def _():
        for k in range(2):
            pltpu.make_async_copy(y_ref.at[pl.ds(0, tm), :], buf.at[1 - slot, k], sem.at[1 - slot]).wait()


def _combine(pos, x, mod_l, wts, ln_g, ln_b, y, mod_next=None):
    tm = TM_CMB
    m_rows = x.shape[0]
    nblocks = m_rows // tm
    has_next = mod_next is not None
    mod_spec = pl.BlockSpec((None, 6, D_MODEL), lambda i, p: (_mod_row(i, tm), 0, 0))
    row_spec = pl.BlockSpec((tm, D_MODEL), lambda i, p: (i, 0))
    vec_spec = pl.BlockSpec((1, D_MODEL), lambda i, p: (0, 0))
    in_specs = [row_spec, mod_spec, pl.BlockSpec((tm, 128), lambda i, p: (i, 0)), vec_spec, vec_spec]
    args = [pos, x, mod_l, wts, ln_g, ln_b]
    out_specs = [row_spec]
    out_shape = [jax.ShapeDtypeStruct((m_rows, D_MODEL), f32)]
    if has_next:
        in_specs.append(mod_spec)
        args.append(mod_next)
        out_specs.append(row_spec)
        out_shape.append(jax.ShapeDtypeStruct((m_rows, D_MODEL), bf16))
    in_specs.append(pl.BlockSpec(memory_space=pl.ANY))
    args.append(y)
    res = pl.pallas_call(
        functools.partial(_combine_kernel, m_rows=m_rows, nblocks=nblocks, has_next=has_next),
        grid_spec=pltpu.PrefetchScalarGridSpec(
            num_scalar_prefetch=1,
            grid=(nblocks,),
            in_specs=in_specs,
            out_specs=out_specs,
            scratch_shapes=[pltpu.VMEM((2, 2, tm, D_MODEL), f32), pltpu.SemaphoreType.DMA((2,))],
        ),
        out_shape=out_shape,
        compiler_params=_params(("arbitrary",)),
        name="combine",
    )(*args)
    return (res[0], res[1]) if has_next else (res[0], None)


def kernel(x, c, ctx, c_ctx, w_ada, b_ada, w_in, b_igate, b_fgate, mh_norm_g, conv_w, conv_b, w_out,
           ln1_g, ln1_b, w_router, b_router, w1, w3, w2, ln2_g, ln2_b):
    cond_raw = jnp.zeros((8, D_MODEL), f32).at[:BATCH].set(c).at[BATCH].set(c_ctx)
    mod = _ada(cond_raw, w_ada, b_ada).reshape(DEPTH, 8, 6, D_MODEL)

    xa, h = _prep(x.reshape(N_LAT, D_MODEL), ctx.reshape(N_CTX, D_MODEL), mod[0])
    w_router_p = jnp.zeros((D_MODEL, 128), f32).at[:, :N_EXPERTS].set(w_router)
    w_router_hi = w_router_p.astype(bf16)
    w_router_lo = (w_router_p - w_router_hi.astype(f32)).astype(bf16)
    w_router_p = jnp.concatenate([w_router_hi, w_router_lo], axis=1)
    b_router_c = b_router.reshape(N_EXPERTS, 1)
    w_in_t = jnp.swapaxes(w_in, 1, 2)
    w_outb = w_out.astype(bf16)

    for l in range(DEPTH):
        last = l == DEPTH - 1
        proj, gates = _inproj(h, w_in_t, l)

        bias = jnp.concatenate([b_igate[l], b_fgate[l]]).astype(f32)
        m_lat, m_ctx, (w1b, w3b, w2b) = _mlstm3(proj, gates, bias, mh_norm_g[l].reshape(1, D_MLSTM),
                                                w1, w3, w2, not last, l)

        xn, hp, idx8, wts = _mix(xa, mod[l], m_lat, m_ctx, proj, conv_w[l], conv_b[l].reshape(1, D_CONV),
                                 w_outb, ln1_g[l].reshape(1, D_MODEL),
                                 ln1_b[l].reshape(1, D_MODEL), w_router_p, b_router_c, not last, l)

        pos, te, n_used = _route(idx8, xn.shape[0])
        y = _experts(te, n_used, pos, hp, w1b, w3b, w2b)
        xa, h = _combine(pos, xn, mod[l], wts, ln2_g[l].reshape(1, D_MODEL), ln2_b[l].reshape(1, D_MODEL), y,
                         None if last else mod[l + 1])

    return xa.reshape(BATCH, SEQ, D_MODEL)
```

```python
import functools

import jax
import jax.numpy as jnp
from jax import lax
from jax.experimental import pallas as pl
from jax.experimental.pallas import tpu as pltpu

f32 = jnp.float32
bf16 = jnp.bfloat16
i32 = jnp.int32

D_MODEL = 2048
BATCH = 4
SEQ = 2048
DEPTH = 4
GRID_W = 64
CTX_LEN = 256
D_MLSTM = 1024
HEADS = 4
DV = 256
DQK = 128
D_CONV = 1024
D_CONV_H = 512
N_EXPERTS = 16
N_GROUPS = 4
EPG = 4
D_FF = 1024
ALPHA = (2 * DEPTH) ** 0.25
LN_EPS = 1e-6
QK_SCALE = DQK ** -0.5

N_LAT = BATCH * SEQ
N_CTX = BATCH * CTX_LEN
N_ALL = N_LAT + N_CTX
D_PROJ = 6144
N_GATE = 16

TM_IN = 3072
TN_IN = 512
TM_MIX = 256
TM_EXP = 256
TM_CMB = 512
VMEM_LIMIT = 56 * 1024 * 1024

HIGHEST = lax.Precision.HIGHEST


def _sigmoid(x):
    return 1.0 / (1.0 + jnp.exp(-x))


def _log_sigmoid(x):
    return jnp.minimum(x, 0.0) - jnp.log1p(jnp.exp(-jnp.abs(x)))


def _ln(x):
    mu = jnp.mean(x, axis=-1, keepdims=True)
    xc = x - mu
    var = jnp.mean(xc * xc, axis=-1, keepdims=True)
    return xc * lax.rsqrt(var + LN_EPS)


def _mod_row(i, tm):
    return jnp.minimum((i * tm) // SEQ, BATCH)


def _params(sem, vmem=VMEM_LIMIT):
    return pltpu.CompilerParams(dimension_semantics=sem, vmem_limit_bytes=vmem)


def _ada_kernel(c_ref, w_ref, b_ref, o_ref):
    c = c_ref[...]
    cond = c * _sigmoid(c)
    o_ref[...] = jnp.dot(cond, w_ref[...], preferred_element_type=f32) + b_ref[...]


def _ada(cond_raw, w_ada, b_ada):
    tn = 1024
    n = 6 * D_MODEL
    return pl.pallas_call(
        _ada_kernel,
        grid=(DEPTH, n // tn),
        in_specs=[
            pl.BlockSpec((8, D_MODEL), lambda l, j: (0, 0)),
            pl.BlockSpec((None, D_MODEL, tn), lambda l, j: (l, 0, j)),
            pl.BlockSpec((None, 1, tn), lambda l, j: (l, 0, j)),
        ],
        out_specs=pl.BlockSpec((None, 8, tn), lambda l, j: (l, 0, j)),
        out_shape=jax.ShapeDtypeStruct((DEPTH, 8, n), f32),
        compiler_params=_params(("parallel", "parallel")),
        name="ada",
    )(cond_raw, w_ada, b_ada.reshape(DEPTH, 1, n))


N_STATE = 2 * HEADS * DQK + D_MLSTM
N_STATE_BLOCKS = N_STATE // TN_IN
NT_DIMS = (((1,), (1,)), ((), ()))


def _modulated(x, mod_ref):
    return (_ln(x) * (1.0 + mod_ref[1:2, :]) + mod_ref[0:1, :]).astype(bf16)


def _inproj_kernel(h_ref, w_ref, wg_ref, o_ref, g_ref):
    @pl.when(pl.program_id(1) == 0)
    def _():
        g_ref[...] = lax.dot_general(h_ref[...], wg_ref[...].astype(bf16), NT_DIMS,
                                     preferred_element_type=f32)

    o_ref[...] = lax.dot_general(h_ref[...], w_ref[0].astype(bf16), NT_DIMS,
                                 preferred_element_type=f32).astype(o_ref.dtype)


def _inproj(h, w_in_t, l):
    m = h.shape[0]
    row0 = lambda j: pl.multiple_of(jnp.where(j < N_STATE_BLOCKS, j * TN_IN, j * TN_IN + N_GATE), 8)
    return pl.pallas_call(
        _inproj_kernel,
        grid=(m // TM_IN, D_PROJ // TN_IN),
        in_specs=[
            pl.BlockSpec((TM_IN, D_MODEL), lambda i, j: (i, 0)),
            pl.BlockSpec((pl.Element(1), pl.Element(TN_IN), pl.Element(D_MODEL)),
                         lambda i, j: (l, row0(j), 0)),
            pl.BlockSpec((None, N_GATE, D_MODEL), lambda i, j: (l, N_STATE // N_GATE, 0)),
        ],
        out_specs=[
            pl.BlockSpec((TM_IN, TN_IN), lambda i, j: (i, j)),
            pl.BlockSpec((TM_IN, N_GATE), lambda i, j: (i, 0)),
        ],
        out_shape=[
            jax.ShapeDtypeStruct((m, D_PROJ), bf16),
            jax.ShapeDtypeStruct((m, N_GATE), f32),
        ],
        compiler_params=_params(("parallel", "arbitrary")),
        name="inproj",
    )(h, w_in_t, w_in_t)


def _prep_kernel(x_ref, c_ref, mod_ref, xa_ref, h_ref, *, n_lat_blocks):
    i = pl.program_id(0)

    @pl.when(i < n_lat_blocks)
    def _():
        xa_ref[...] = x_ref[...]

    @pl.when(i >= n_lat_blocks)
    def _():
        xa_ref[...] = c_ref[...]

    h_ref[...] = _modulated(xa_ref[...], mod_ref)


def _prep(x2, c2, mod_l):
    tm = TM_CMB
    nlb = N_LAT // tm
    return pl.pallas_call(
        functools.partial(_prep_kernel, n_lat_blocks=nlb),
        grid=(N_ALL // tm,),
        in_specs=[
            pl.BlockSpec((tm, D_MODEL), lambda i: (jnp.minimum(i, nlb - 1), 0)),
            pl.BlockSpec((tm, D_MODEL), lambda i: (jnp.maximum(i - nlb, 0), 0)),
            pl.BlockSpec((None, 6, D_MODEL), lambda i: (_mod_row(i, tm), 0, 0)),
        ],
        out_specs=[pl.BlockSpec((tm, D_MODEL), lambda i: (i, 0)),
                   pl.BlockSpec((tm, D_MODEL), lambda i: (i, 0))],
        out_shape=[jax.ShapeDtypeStruct((N_ALL, D_MODEL), f32),
                   jax.ShapeDtypeStruct((N_ALL, D_MODEL), bf16)],
        compiler_params=_params(("parallel",)),
        name="prep",
    )(x2, c2, mod_l)


MCHUNK = 128
N_CHUNK_CTX = CTX_LEN // MCHUNK
N_CHUNK_LAT = SEQ // MCHUNK
N_CHUNK = N_CHUNK_CTX + N_CHUNK_LAT
GROUP = N_CHUNK_CTX
N_GROUP_LAT = N_CHUNK_LAT // GROUP
LAT_ROW0 = 8
D_AUG = DV + 128


CVT_STEPS = 2 * N_GROUP_LAT
CVT_ROWS_IN = D_MODEL // CVT_STEPS
CVT_ROWS_OUT = D_FF // CVT_STEPS


def _mlstm3_kernel(bias_ref, ql_ref, kl_ref, vl_ref, ol_ref, qc_ref, kc_ref, vc_ref, oc_ref,
                   gcl_ref, grl_ref, gcc_ref, grc_ref, gain_ref, w1_hbm, w3_hbm, w2_hbm, *rest,
                   has_ctx_out, layer):
    if has_ctx_out:
        ml_ref, mc_ref = rest[:2]
        rest = rest[2:]
    else:
        ml_ref, mc_ref = rest[0], None
        rest = rest[1:]
    w1b_hbm, w3b_hbm, w2b_hbm, s_ref, st_ref, msc, rows_s, cols_s = rest[:8]
    cin = rest[8:11]
    cout = rest[11:14]
    csem_in, csem_out = rest[14:16]
    L = MCHUNK
    b = pl.program_id(0)
    h = pl.program_id(1)
    expert = b * HEADS + h
    cvt_src = (w1_hbm, w3_hbm, w2_hbm)
    cvt_dst = (w1b_hbm, w3b_hbm, w2b_hbm)
    cvt_rows = (CVT_ROWS_IN, CVT_ROWS_IN, CVT_ROWS_OUT)

    def cvt_in(k, t):
        r0 = pl.multiple_of(t * cvt_rows[k], cvt_rows[k])
        return pltpu.make_async_copy(cvt_src[k].at[layer, expert, pl.ds(r0, cvt_rows[k]), :],
                                     cin[k].at[t % 2], csem_in.at[k, t % 2])

    def cvt_out(k, t):
        r0 = pl.multiple_of(t * cvt_rows[k], cvt_rows[k])
        return pltpu.make_async_copy(cout[k].at[t % 2], cvt_dst[k].at[expert, pl.ds(r0, cvt_rows[k]), :],
                                     csem_out.at[k, t % 2])

    def cvt_step(t):
        for k in range(3):
            cvt_in(k, t).wait()

            @pl.when(t >= 2)
            def _():
                cvt_out(k, t - 2).wait()

            cout[k][t % 2] = cin[k][t % 2].astype(bf16)
            cvt_out(k, t).start()

            @pl.when(t + 2 < CVT_STEPS)
            def _():
                cvt_in(k, t + 2).start()

    for k in range(3):
        cvt_in(k, 0).start()
        cvt_in(k, 1).start()
    bi = (bias_ref[h], bias_ref[HEADS + h])
    bf = (bias_ref[2 * HEADS + h], bias_ref[3 * HEADS + h])

    rr = lax.broadcasted_iota(i32, (L, L), 0)
    cc = lax.broadcasted_iota(i32, (L, L), 1)
    lo_mask = rr >= cc
    up_mask = rr <= cc
    masks = (lo_mask, up_mask)

    def row_forms(gr_ref, sl, dst0, n):
        for d in range(2):
            tri = (up_mask if d == 0 else lo_mask).astype(f32)
            i_r = gr_ref[d, sl, :] + bi[d]
            lf_r = _log_sigmoid(gr_ref[2 + d, sl, :] + bf[d])
            b_r = jnp.dot(lf_r, tri, precision=HIGHEST, preferred_element_type=f32)
            rows_s[d, dst0:dst0 + n, :] = i_r - b_r

    row_forms(grc_ref, pl.ds(b * N_CHUNK_CTX, N_CHUNK_CTX), 0, N_CHUNK_CTX)
    row_forms(grl_ref, slice(None), LAT_ROW0, N_CHUNK_LAT)

    kind = lax.broadcasted_iota(i32, (1, 4 * GROUP), 1) % 4
    bias_v = jnp.where(kind == 0, bi[0], jnp.where(kind == 1, bi[1], jnp.where(kind == 2, bf[0], bf[1])))
    tpos = lax.broadcasted_iota(i32, (L, 4 * GROUP), 0)

    def col_forms(x):
        y = x + bias_v
        y = jnp.where(kind >= 2, _log_sigmoid(y), y)
        pre = y
        suf = y
        s = 1
        while s < L:
            pre = pre + jnp.where(tpos >= s, pltpu.roll(pre, s, 0), 0.0)
            suf = suf + jnp.where(tpos < L - s, pltpu.roll(suf, L - s, 0), 0.0)
            s *= 2
        return jnp.where(kind == 2, pre, jnp.where(kind == 3, suf, y))

    cols_s[0] = col_forms(gcc_ref[...])

    def col_body(g, c):
        cols_s[g + 1] = col_forms(gcl_ref[g])
        return c

    lax.fori_loop(0, N_GROUP_LAT, col_body, 0)

    ones_col = (lax.broadcasted_iota(i32, (L, 128), 1) == 0).astype(bf16)

    s_ref[...] = jnp.zeros_like(s_ref)

    def state_step(d, c, row, col, j, k, v, m):
        i_c = col[:, 4 * j + d:4 * j + d + 1]
        b1 = col[:, 4 * j + 2 + d:4 * j + 3 + d]
        last = L - 1 if d == 0 else 0
        b_last = b1[last:last + 1, :]
        ct = i_c - b1
        mx = jnp.max(ct, axis=0, keepdims=True)
        wl = jnp.exp(ct - mx)
        mloc_last = b_last + mx
        m_new = jnp.maximum(b_last + m, mloc_last)
        decay = jnp.exp(b_last + m - m_new)
        a_l = jnp.exp(mloc_last - m_new)
        msc[d, pl.ds(row, 1), :] = jnp.broadcast_to(m, (1, 128))
        kwt = (k.astype(f32) * (wl * QK_SCALE)).astype(bf16).T
        for cb in range(D_AUG // 128):
            cs = slice(cb * 128, (cb + 1) * 128)
            s_old = s_ref[d, :, cs]
            st_ref[d, c, :, cs] = s_old.astype(bf16)
            rhs = v[:, cs] if cb < DV // 128 else ones_col
            s_ref[d, :, cs] = decay * s_old + a_l * jnp.dot(kwt, rhs, preferred_element_type=f32)
        return m_new

    zero = jnp.zeros((1, 1), f32)
    m_f = m_b = zero
    col0 = cols_s[0]
    for step in range(GROUP):
        jf, jb = step, GROUP - 1 - step
        m_f = state_step(0, jf, jf, col0, jf, kc_ref[jf * L:(jf + 1) * L, :], vc_ref[jf * L:(jf + 1) * L, :], m_f)
        m_b = state_step(1, jb, jb, col0, jb, kc_ref[jb * L:(jb + 1) * L, :], vc_ref[jb * L:(jb + 1) * L, :], m_b)

    def state_body(it, carry):
        m_f, m_b = carry
        gf = it
        gb = N_GROUP_LAT + 1 - it
        colf = cols_s[gf]
        colb = cols_s[gb]
        for step in range(GROUP):
            jf, jb = step, GROUP - 1 - step
            clf = (gf - 1) * GROUP + jf
            clb = (gb - 1) * GROUP + jb
            rf = pl.multiple_of(clf * L, L)
            rb = pl.multiple_of(clb * L, L)
            m_f = state_step(0, N_CHUNK_CTX + clf, LAT_ROW0 + clf, colf, jf,
                             kl_ref[pl.ds(rf, L), :], vl_ref[pl.ds(rf, L), :], m_f)
            m_b = state_step(1, N_CHUNK_CTX + clb, LAT_ROW0 + clb, colb, jb,
                             kl_ref[pl.ds(rb, L), :], vl_ref[pl.ds(rb, L), :], m_b)
        cvt_step(it - 1)
        return m_f, m_b

    lax.fori_loop(1, N_GROUP_LAT + 1, state_body, (m_f, m_b))

    gain = gain_ref[...]

    def out_chunk(q, k, v, o, col, j, c, row, out_ref, r0):
        qk = lax.dot_general(q, k, (((1,), (1,)), ((), ())), preferred_element_type=f32) * QK_SCALE
        vaug = jnp.concatenate([v, ones_col], axis=1)
        sl = []
        per = []
        for d in range(2):
            b1 = col[:, 4 * j + 2 + d:4 * j + 3 + d]
            dm = jnp.where(masks[d], b1 + rows_s[d, pl.ds(row, 1), :], -jnp.inf)
            mloc = jnp.max(dm, axis=1, keepdims=True)
            wloc = jnp.exp(dm - mloc)
            m_prev = msc[d, pl.ds(row, 1), :][:, 0:1]
            m_t = jnp.maximum(b1 + m_prev, mloc)
            inter = jnp.exp(b1 + m_prev - m_t)
            a = jnp.exp(mloc - m_t)
            sl.append((qk * wloc).astype(bf16))
            per.append((m_t, inter, a))
        x = jnp.dot(jnp.concatenate(sl, axis=0), vaug, preferred_element_type=f32)
        hs = None
        for d in range(2):
            y = jnp.dot(q, st_ref[d, c], preferred_element_type=f32)
            m_t, inter, a = per[d]
            xd = x[d * L:(d + 1) * L, :]
            den = inter * y[:, DV:DV + 1] + a * xd[:, DV:DV + 1]
            rinv = 1.0 / jnp.maximum(jnp.abs(den), jnp.exp(-m_t))
            hd = (inter * rinv) * y[:, :DV] + (a * rinv) * xd[:, :DV]
            hs = hd if hs is None else hs + hd
        r = lax.rsqrt(jnp.mean(hs * hs, axis=-1, keepdims=True) + LN_EPS)
        out_ref[pl.ds(r0, L), :] = (hs * r * gain * _sigmoid(o.astype(f32))).astype(out_ref.dtype)

    if has_ctx_out:
        for j in range(GROUP):
            sl_ = slice(j * L, (j + 1) * L)
            out_chunk(qc_ref[sl_, :], kc_ref[sl_, :], vc_ref[sl_, :], oc_ref[sl_, :], col0, j, j, j,
                      mc_ref, j * L)

    def out_body(g2, carry):
        for gg in range(2):
            g = 2 * g2 + gg
            col = cols_s[g + 1]
            for j in range(GROUP):
                cl = g * GROUP + j
                r0 = pl.multiple_of(cl * L, L)
                out_chunk(ql_ref[pl.ds(r0, L), :], kl_ref[pl.ds(r0, L), :], vl_ref[pl.ds(r0, L), :],
                          ol_ref[pl.ds(r0, L), :], col, j, N_CHUNK_CTX + cl, LAT_ROW0 + cl, ml_ref, r0)
        cvt_step(N_GROUP_LAT + 2 * g2)
        cvt_step(N_GROUP_LAT + 2 * g2 + 1)
        return carry

    lax.fori_loop(0, N_GROUP_LAT // 2, out_body, 0)
    for k in range(3):
        cvt_out(k, CVT_STEPS - 2).wait()
        cvt_out(k, CVT_STEPS - 1).wait()


def _mlstm3(proj, gates, bias, gain, w1, w3, w2, has_ctx_out, layer):
    L = MCHUNK
    g = gates[:, :N_GATE]
    g4 = jnp.stack([g[:, 0:4], g[:, 4:8], g[:, 8:12], g[:, 12:16]], axis=-1)
    gh = jnp.transpose(g4, (1, 0, 2))
    gcol_l = gh[:, :N_LAT].reshape(HEADS, BATCH * N_GROUP_LAT, GROUP, L, 4)
    gcol_l = jnp.transpose(gcol_l, (0, 1, 3, 2, 4)).reshape(HEADS, BATCH * N_GROUP_LAT, L, 4 * GROUP)
    gcol_c = gh[:, N_LAT:].reshape(HEADS, BATCH, GROUP, L, 4)
    gcol_c = jnp.transpose(gcol_c, (0, 1, 3, 2, 4)).reshape(HEADS, BATCH, L, 4 * GROUP)
    gr = jnp.transpose(g4, (1, 2, 0))
    grow_l = gr[:, :, :N_LAT].reshape(HEADS, 4, N_LAT // L, L)
    grow_c = gr[:, :, N_LAT:].reshape(HEADS, 4, N_CTX // L, L)

    ctx_rb = lambda b, h: N_LAT // CTX_LEN + b
    in_specs = [
        pl.BlockSpec(memory_space=pltpu.SMEM),
        pl.BlockSpec((SEQ, DQK), lambda b, h: (b, h)),
        pl.BlockSpec((SEQ, DQK), lambda b, h: (b, HEADS + h)),
        pl.BlockSpec((SEQ, DV), lambda b, h: (b, HEADS + h)),
        pl.BlockSpec((SEQ, DV), lambda b, h: (b, 2 * HEADS + h)),
        pl.BlockSpec((CTX_LEN, DQK), lambda b, h: (ctx_rb(b, h), h)),
        pl.BlockSpec((CTX_LEN, DQK), lambda b, h: (ctx_rb(b, h), HEADS + h)),
        pl.BlockSpec((CTX_LEN, DV), lambda b, h: (ctx_rb(b, h), HEADS + h)),
        pl.BlockSpec((CTX_LEN, DV), lambda b, h: (ctx_rb(b, h), 2 * HEADS + h)),
        pl.BlockSpec((None, N_GROUP_LAT, L, 4 * GROUP), lambda b, h: (h, b, 0, 0)),
        pl.BlockSpec((None, 4, N_CHUNK_LAT, L), lambda b, h: (h, 0, b, 0)),
        pl.BlockSpec((None, None, L, 4 * GROUP), lambda b, h: (h, b, 0, 0)),
        pl.BlockSpec((None, 4, N_CTX // L, L), lambda b, h: (h, 0, 0, 0)),
        pl.BlockSpec((1, DV), lambda b, h: (0, h)),
        pl.BlockSpec(memory_space=pl.ANY),
        pl.BlockSpec(memory_space=pl.ANY),
        pl.BlockSpec(memory_space=pl.ANY),
    ]
    out_specs = [pl.BlockSpec((SEQ, DV), lambda b, h: (b, h))]
    out_shape = [jax.ShapeDtypeStruct((N_LAT, D_MLSTM), bf16)]
    if has_ctx_out:
        out_specs.append(pl.BlockSpec((CTX_LEN, DV), lambda b, h: (b, h)))
        out_shape.append(jax.ShapeDtypeStruct((N_CTX, D_MLSTM), bf16))
    out_specs += [pl.BlockSpec(memory_space=pl.ANY)] * 3
    out_shape += [jax.ShapeDtypeStruct((N_EXPERTS, D_MODEL, D_FF), bf16),
                  jax.ShapeDtypeStruct((N_EXPERTS, D_MODEL, D_FF), bf16),
                  jax.ShapeDtypeStruct((N_EXPERTS, D_FF, D_MODEL), bf16)]
    scratch = [
        pltpu.VMEM((2, DQK, D_AUG), f32),
        pltpu.VMEM((2, N_CHUNK, DQK, D_AUG), bf16),
        pltpu.VMEM((2, LAT_ROW0 + N_CHUNK_LAT, 128), f32),
        pltpu.VMEM((2, LAT_ROW0 + N_CHUNK_LAT, L), f32),
        pltpu.VMEM((N_GROUP_LAT + 1, L, 4 * GROUP), f32),
        pltpu.VMEM((2, CVT_ROWS_IN, D_FF), f32),
        pltpu.VMEM((2, CVT_ROWS_IN, D_FF), f32),
        pltpu.VMEM((2, CVT_ROWS_OUT, D_MODEL), f32),
        pltpu.VMEM((2, CVT_ROWS_IN, D_FF), bf16),
        pltpu.VMEM((2, CVT_ROWS_IN, D_FF), bf16),
        pltpu.VMEM((2, CVT_ROWS_OUT, D_MODEL), bf16),
        pltpu.SemaphoreType.DMA((3, 2)),
        pltpu.SemaphoreType.DMA((3, 2)),
    ]
    assert N_EXPERTS == BATCH * HEADS
    res = pl.pallas_call(
        functools.partial(_mlstm3_kernel, has_ctx_out=has_ctx_out, layer=layer),
        grid=(BATCH, HEADS),
        in_specs=in_specs,
        out_specs=out_specs,
        out_shape=out_shape,
        scratch_shapes=scratch,
        compiler_params=_params(("arbitrary", "arbitrary")),
        name="mlstm",
    )(bias, proj, proj, proj, proj, proj, proj, proj, proj, gcol_l, grow_l, gcol_c, grow_c, gain,
      w1, w3, w2)
    if has_ctx_out:
        return res[0], res[1], res[2:]
    return res[0], None, res[1:]


def _top2_rows(vals):
    best = vals[0]
    bi = jnp.zeros(best.shape, i32)
    for j in range(1, len(vals)):
        take = vals[j] > best
        best = jnp.where(take, vals[j], best)
        bi = jnp.where(take, j, bi)
    sec = None
    si = None
    for j in range(len(vals)):
        cand = jnp.where(bi == j, -jnp.inf, vals[j])
        if sec is None:
            sec, si = cand, jnp.zeros(best.shape, i32)
        else:
            take = cand > sec
            sec = jnp.where(take, cand, sec)
            si = jnp.where(take, j, si)
    return bi, si


def _mix_kernel(x_ref, mod_ref, ml_ref, mc_ref, u_ref, bg_ref, cg_ref, ut_ref, ct_ref, ub_ref, cb_ref,
                cw_ref, cbias_ref, wo_ref, g1_ref, b1_ref, wr_ref, br_ref,
                xo_ref, hp_ref, idx_ref, wt_ref, m_scr, y_scr, *, n_lat_blocks, has_ctx):
    tm = TM_MIX
    i = pl.program_id(0)
    gate = mod_ref[2:3, :]
    shift2 = mod_ref[3:4, :]
    scale2 = mod_ref[4:5, :]
    cw = cw_ref[...]
    cbias = cbias_ref[...]
    row = lax.broadcasted_iota(i32, (tm, 1), 0)

    def shifted(z, first, last):
        prev = jnp.where(first, 0.0, pltpu.roll(z, 1, 0))
        nxt = jnp.where(last, 0.0, pltpu.roll(z, tm - 1, 0))
        return prev, nxt

    def lat_branch():
        z = cg_ref[...].astype(f32) * u_ref[...].astype(f32)
        col = row % GRID_W
        zh = z[:, :D_CONV_H]
        prev, nxt = shifted(zh, col == 0, col == GRID_W - 1)
        yh = cw[0:1, :D_CONV_H] * prev + cw[1:2, :D_CONV_H] * zh + cw[2:3, :D_CONV_H] * nxt
        bpb = SEQ // tm
        top_ok = (i % bpb != 0).astype(f32)
        bot_ok = (i % bpb != bpb - 1).astype(f32)
        zt = ct_ref[...].astype(f32) * ut_ref[...].astype(f32) * top_ok
        zb = cb_ref[...].astype(f32) * ub_ref[...].astype(f32) * bot_ok
        zv = z[:, D_CONV_H:]
        zext = jnp.concatenate([zt, zv, zb], axis=0)
        yv = cw[0:1, D_CONV_H:] * zext[0:tm] + cw[1:2, D_CONV_H:] * zv \
            + cw[2:3, D_CONV_H:] * zext[2 * GRID_W:2 * GRID_W + tm]
        y = jnp.concatenate([yh, yv], axis=1) + cbias
        y_scr[...] = (bg_ref[...].astype(f32) * y).astype(bf16)
        m_scr[...] = ml_ref[...]

    def ctx_branch():
        z = cg_ref[...].astype(f32) * u_ref[...].astype(f32)
        pos = row % CTX_LEN
        prev, nxt = shifted(z, pos == 0, pos == CTX_LEN - 1)
        y = cw[0:1, :] * prev + cw[1:2, :] * z + cw[2:3, :] * nxt + cbias
        y_scr[...] = (bg_ref[...].astype(f32) * y).astype(bf16)
        m_scr[...] = mc_ref[...]

    if has_ctx:
        pl.when(i < n_lat_blocks)(lat_branch)
        pl.when(i >= n_lat_blocks)(ctx_branch)
    else:
        lat_branch()

    out = jnp.dot(m_scr[...], wo_ref[0:D_MLSTM, :], preferred_element_type=f32) \
        + jnp.dot(y_scr[...], wo_ref[D_MLSTM:, :], preferred_element_type=f32)
    xn = _ln(ALPHA * x_ref[...] + gate * out) * g1_ref[...] + b1_ref[...]
    xo_ref[...] = xn
    h2 = _ln(xn) * (1.0 + scale2) + shift2

    hp_ref[...] = h2

    h_hi = h2.astype(bf16)
    hs = jnp.concatenate([h_hi, (h2 - h_hi.astype(f32)).astype(bf16)], axis=0)
    pr = jnp.dot(hs, wr_ref[...], preferred_element_type=f32)
    logits = pr[:tm, :128] + (pr[:tm, 128:] + pr[tm:, :128])
    lt = logits.T
    s = _sigmoid(lt[0:N_EXPERTS, :])
    sb = s + br_ref[...]
    sb_rows = [sb[e:e + 1, :] for e in range(N_EXPERTS)]
    s_rows = [s[e:e + 1, :] for e in range(N_EXPERTS)]
    gscores = []
    for g in range(N_GROUPS):
        a_, b_, c_, d_ = sb_rows[EPG * g:EPG * g + EPG]
        hi1, lo1 = jnp.maximum(a_, b_), jnp.minimum(a_, b_)
        hi2, lo2 = jnp.maximum(c_, d_), jnp.minimum(c_, d_)
        top = jnp.maximum(hi1, hi2)
        second = jnp.maximum(jnp.minimum(hi1, hi2), jnp.maximum(lo1, lo2))
        gscores.append(top + second)
    gbest = gscores[0]
    gsel = jnp.zeros(gbest.shape, i32)
    for g in range(1, N_GROUPS):
        take = gscores[g] > gbest
        gbest = jnp.where(take, gscores[g], gbest)
        gsel = jnp.where(take, g, gsel)

    def pick_group(rows, j):
        v = rows[j]
        for g in range(1, N_GROUPS):
            v = jnp.where(gsel == g, rows[EPG * g + j], v)
        return v

    cand_b = [pick_group(sb_rows, j) for j in range(EPG)]
    cand_s = [pick_group(s_rows, j) for j in range(EPG)]
    i1, i2 = _top2_rows(cand_b)

    def pick_idx(rows, idx):
        v = rows[0]
        for j in range(1, EPG):
            v = jnp.where(idx == j, rows[j], v)
        return v

    s1 = pick_idx(cand_s, i1)
    s2 = pick_idx(cand_s, i2)
    tot = s1 + s2
    w1 = s1 / tot
    w2 = s2 / tot
    e1 = gsel * EPG + i1
    e2 = gsel * EPG + i2
    r8 = lax.broadcasted_iota(i32, (8, tm), 0)
    idx_ref[...] = jnp.where(r8 == 0, e1, jnp.where(r8 == 1, e2, 0))
    r128 = lax.broadcasted_iota(i32, (128, tm), 0)
    wmat = jnp.where(r128 == 0, w1, jnp.where(r128 == 1, w2, 0.0))
    wt_ref[...] = wmat.T


def _mix(x, mod_l, m_lat, m_ctx, proj, conv_w, conv_b, w_out, ln_g, ln_b, w_router, b_router, has_ctx, l):
    tm = TM_MIX
    m_rows = N_ALL if has_ctx else N_LAT
    nlb = N_LAT // tm
    nblocks = m_rows // tm
    hb = tm // GRID_W
    n_hblocks = N_ALL // GRID_W
    if m_ctx is None:
        m_ctx = m_lat
    ncb = m_ctx.shape[0] // tm
    in_specs = [
        pl.BlockSpec((tm, D_MODEL), lambda i: (i, 0)),
        pl.BlockSpec((None, 6, D_MODEL), lambda i: (_mod_row(i, tm), 0, 0)),
        pl.BlockSpec((tm, D_MLSTM), lambda i: (jnp.minimum(i, nlb - 1), 0)),
        pl.BlockSpec((tm, D_MLSTM), lambda i: (jnp.clip(i - nlb, 0, ncb - 1), 0)),
        pl.BlockSpec((tm, D_CONV), lambda i: (i, 3)),
        pl.BlockSpec((tm, D_CONV), lambda i: (i, 4)),
        pl.BlockSpec((tm, D_CONV), lambda i: (i, 5)),
        pl.BlockSpec((GRID_W, D_CONV_H), lambda i: (jnp.maximum(i * hb - 1, 0), 7)),
        pl.BlockSpec((GRID_W, D_CONV_H), lambda i: (jnp.maximum(i * hb - 1, 0), 11)),
        pl.BlockSpec((GRID_W, D_CONV_H), lambda i: (jnp.minimum((i + 1) * hb, n_hblocks - 1), 7)),
        pl.BlockSpec((GRID_W, D_CONV_H), lambda i: (jnp.minimum((i + 1) * hb, n_hblocks - 1), 11)),
        pl.BlockSpec((3, D_CONV), lambda i: (0, 0)),
        pl.BlockSpec((1, D_CONV), lambda i: (0, 0)),
        pl.BlockSpec((None, D_MODEL, D_MODEL), lambda i: (l, 0, 0)),
        pl.BlockSpec((1, D_MODEL), lambda i: (0, 0)),
        pl.BlockSpec((1, D_MODEL), lambda i: (0, 0)),
        pl.BlockSpec((D_MODEL, 256), lambda i: (0, 0)),
        pl.BlockSpec((N_EXPERTS, 1), lambda i: (0, 0)),
    ]
    out_specs = [
        pl.BlockSpec((tm, D_MODEL), lambda i: (i, 0)),
        pl.BlockSpec((tm, D_MODEL), lambda i: (i, 0)),
        pl.BlockSpec((8, tm), lambda i: (0, i)),
        pl.BlockSpec((tm, 128), lambda i: (i, 0)),
    ]
    out_shape = [
        jax.ShapeDtypeStruct((m_rows, D_MODEL), f32),
        jax.ShapeDtypeStruct((m_rows, D_MODEL), f32),
        jax.ShapeDtypeStruct((8, m_rows), i32),
        jax.ShapeDtypeStruct((m_rows, 128), f32),
    ]
    return pl.pallas_call(
        functools.partial(_mix_kernel, n_lat_blocks=nlb, has_ctx=has_ctx),
        grid=(nblocks,),
        in_specs=in_specs,
        out_specs=out_specs,
        out_shape=out_shape,
        scratch_shapes=[pltpu.VMEM((tm, D_MLSTM), bf16), pltpu.VMEM((tm, D_CONV), bf16)],
        compiler_params=_params(("parallel",)),
        name="mix",
    )(x, mod_l, m_lat, m_ctx, proj, proj, proj, proj, proj, proj, proj,
      conv_w, conv_b, w_out, ln_g, ln_b, w_router, b_router)


def _route_kernel(ef_ref, pos_ref, meta_ref, *, rows):
    R = rows
    ef = ef_ref[...]
    li = lax.broadcasted_iota(i32, (128, 128), 0)
    lj = lax.broadcasted_iota(i32, (128, 128), 1)
    strict_up = (li < lj).astype(bf16)
    ri = lax.broadcasted_iota(i32, (R, R), 0)
    rj = lax.broadcasted_iota(i32, (R, R), 1)
    strict_lo = (rj < ri).astype(bf16)
    lane16 = lax.broadcasted_iota(i32, (R, N_EXPERTS), 1)

    ohs = [(ef == e) for e in range(N_EXPERTS)]
    within = [jnp.dot(oh.astype(bf16), strict_up, preferred_element_type=f32) for oh in ohs]
    rt = jnp.zeros((R, N_EXPERTS), f32)
    for e in range(N_EXPERTS):
        rt = jnp.where(lane16 == e, jnp.sum(ohs[e].astype(f32), axis=1, keepdims=True), rt)
    rp = jnp.dot(strict_lo, rt.astype(bf16), preferred_element_type=f32)
    counts = rp[R - 1:R, :] + rt[R - 1:R, :]
    ntile = jnp.floor((counts + (TM_EXP - 1)) * (1.0 / TM_EXP))
    ei = lax.broadcasted_iota(i32, (N_EXPERTS, N_EXPERTS), 0)
    ej = lax.broadcasted_iota(i32, (N_EXPERTS, N_EXPERTS), 1)
    tend = jnp.dot(ntile.astype(bf16), (ei <= ej).astype(bf16), preferred_element_type=f32)
    off = rp + (tend - ntile) * float(TM_EXP)
    pos = jnp.zeros((R, 128), f32)
    for e in range(N_EXPERTS):
        pos = jnp.where(ohs[e], within[e] + off[:, e:e + 1], pos)
    pos_ref[...] = pos.astype(i32)

    n_used = tend[:, N_EXPERTS - 1:N_EXPERTS]
    tile = jnp.minimum(lax.broadcasted_iota(i32, (1, 128), 1).astype(f32), n_used - 1.0)
    te = jnp.zeros((1, 128), f32)
    for e in range(N_EXPERTS):
        te = te + (tend[:, e:e + 1] <= tile).astype(f32)
    te = jnp.minimum(te, float(N_EXPERTS - 1))
    r8 = lax.broadcasted_iota(i32, (8, 128), 0)
    meta_ref[...] = jnp.where(r8 == 0, te, jnp.where(r8 == 1, n_used, 0.0)).astype(i32)


def _route(idx8, m_rows):
    rows = 2 * m_rows // 128
    n_tiles = 2 * m_rows // TM_EXP + N_EXPERTS
    assert n_tiles <= 128
    pos, meta = pl.pallas_call(
        functools.partial(_route_kernel, rows=rows),
        out_shape=[jax.ShapeDtypeStruct((rows, 128), i32), jax.ShapeDtypeStruct((8, 128), i32)],
        name="route",
    )(idx8[:2, :].reshape(rows, 128))
    return pos.reshape(-1), meta[0, :n_tiles], meta[1, 0:1]


def _expert_kernel(te_ref, nu_ref, pos_ref, hp_ref, zeros_ref, w1_ref, w3_ref, w2_ref, y_ref, xbuf, sem,
                   src_s, zsem, *, m_rows):
    tm = TM_EXP
    i = pl.program_id(0)
    n_used = nu_ref[0]
    n_rows = src_s.shape[0]

    def issue(tile, slot):
        base = tile * tm
        for j in range(tm):
            t = src_s[base + j]
            pltpu.make_async_copy(hp_ref.at[pl.ds(t, 1), :], xbuf.at[slot, pl.ds(j, 1), :],
                                  sem.at[slot]).start()

    def wait(slot):
        pltpu.make_async_copy(hp_ref.at[pl.ds(0, tm), :], xbuf.at[slot], sem.at[slot]).wait()

    @pl.when(i == 0)
    def _():
        zfill = pltpu.make_async_copy(zeros_ref, src_s, zsem)
        zfill.start()
        zfill.wait()

        def scatter(t, c):
            src_s[pos_ref[t]] = t
            src_s[pos_ref[m_rows + t]] = t
            return c

        lax.fori_loop(0, m_rows, scatter, 0, unroll=16)
        issue(0, 0)

    @pl.when(i < n_used)
    def _():
        slot = i % 2
        issue(i + 1, 1 - slot)
        wait(slot)
        xb = xbuf[slot].astype(bf16)
        a1 = jnp.dot(xb, w1_ref[...], preferred_element_type=f32)
        a3 = jnp.dot(xb, w3_ref[...], preferred_element_type=f32)
        act = (a1 * _sigmoid(a1) * a3).astype(bf16)
        y_ref[...] = jnp.dot(act, w2_ref[...], preferred_element_type=f32)

        @pl.when(i + 1 == n_used)
        def _():
            wait(1 - slot)

    @pl.when(i >= n_used)
    def _():
        y_ref[...] = jnp.zeros_like(y_ref)


def _experts(te, n_used, pos, hp, w1, w3, w2):
    tm = TM_EXP
    m_rows = hp.shape[0]
    n_tiles = te.shape[0]
    wspec = lambda r, c: pl.BlockSpec((None, r, c), lambda i, te, nu, pos: (te[i], 0, 0))
    return pl.pallas_call(
        functools.partial(_expert_kernel, m_rows=m_rows),
        grid_spec=pltpu.PrefetchScalarGridSpec(
            num_scalar_prefetch=3,
            grid=(n_tiles,),
            in_specs=[
                pl.BlockSpec(memory_space=pl.ANY),
                pl.BlockSpec(memory_space=pl.ANY),
                wspec(D_MODEL, D_FF),
                wspec(D_MODEL, D_FF),
                wspec(D_FF, D_MODEL),
            ],
            out_specs=pl.BlockSpec((tm, D_MODEL), lambda i, te, nu, pos: (i, 0)),
            scratch_shapes=[pltpu.VMEM((2, tm, D_MODEL), f32), pltpu.SemaphoreType.DMA((2,)),
                            pltpu.SMEM((n_tiles * tm,), i32), pltpu.SemaphoreType.DMA(())],
        ),
        out_shape=jax.ShapeDtypeStruct((n_tiles * tm, D_MODEL), f32),
        compiler_params=_params(("arbitrary",)),
        name="experts",
    )(te, n_used, pos, hp, jnp.zeros((n_tiles * tm,), i32), w1, w3, w2)


def _combine_kernel(pos_ref, x_ref, mod_ref, wt_ref, g_ref, b_ref, *rest, m_rows, nblocks, has_next):
    if has_next:
        modn_ref, y_ref, o_ref, h_ref, buf, sem = rest
    else:
        y_ref, o_ref, buf, sem = rest
    tm = TM_CMB
    i = pl.program_id(0)

    def issue(blk, slot):
        base = blk * tm
        for j in range(tm):
            for k in range(2):
                p = pos_ref[k * m_rows + base + j]
                pltpu.make_async_copy(y_ref.at[pl.ds(p, 1), :], buf.at[slot, k, pl.ds(j, 1), :],
                                      sem.at[slot]).start()

    @pl.when(i == 0)
    def _():
        issue(0, 0)

    slot = i % 2
    for k in range(2):
        pltpu.make_async_copy(y_ref.at[pl.ds(0, tm), :], buf.at[slot, k], sem.at[slot]).wait()

    wt = wt_ref[...]
    moe = wt[:, 0:1] * buf[slot, 0] + wt[:, 1:2] * buf[slot, 1]
    gate = mod_ref[5:6, :]
    v = ALPHA * x_ref[...] + gate * moe

    issue(jnp.minimum(i + 1, nblocks - 1), 1 - slot)
    xn = _ln(v) * g_ref[...] + b_ref[...]
    o_ref[...] = xn
    if has_next:
        h_ref[...] = _modulated(xn, modn_ref)

    @pl.when(i + 1 == nblocks)
    def _():
        for k in range(2):
            pltpu.make_async_copy(y_ref.at[pl.ds(0, tm), :], buf.at[1 - slot, k], sem.at[1 - slot]).wait()


def _combine(pos, x, mod_l, wts, ln_g, ln_b, y, mod_next=None):
    tm = TM_CMB
    m_rows = x.shape[0]
    nblocks = m_rows // tm
    has_next = mod_next is not None
    mod_spec = pl.BlockSpec((None, 6, D_MODEL), lambda i, p: (_mod_row(i, tm), 0, 0))
    row_spec = pl.BlockSpec((tm, D_MODEL), lambda i, p: (i, 0))
    vec_spec = pl.BlockSpec((1, D_MODEL), lambda i, p: (0, 0))
    in_specs = [row_spec, mod_spec, pl.BlockSpec((tm, 128), lambda i, p: (i, 0)), vec_spec, vec_spec]
    args = [pos, x, mod_l, wts, ln_g, ln_b]
    out_specs = [row_spec]
    out_shape = [jax.ShapeDtypeStruct((m_rows, D_MODEL), f32)]
    if has_next:
        in_specs.append(mod_spec)
        args.append(mod_next)
        out_specs.append(row_spec)
        out_shape.append(jax.ShapeDtypeStruct((m_rows, D_MODEL), bf16))
    in_specs.append(pl.BlockSpec(memory_space=pl.ANY))
    args.append(y)
    res = pl.pallas_call(
        functools.partial(_combine_kernel, m_rows=m_rows, nblocks=nblocks, has_next=has_next),
        grid_spec=pltpu.PrefetchScalarGridSpec(
            num_scalar_prefetch=1,
            grid=(nblocks,),
            in_specs=in_specs,
            out_specs=out_specs,
            scratch_shapes=[pltpu.VMEM((2, 2, tm, D_MODEL), f32), pltpu.SemaphoreType.DMA((2,))],
        ),
        out_shape=out_shape,
        compiler_params=_params(("arbitrary",)),
        name="combine",
    )(*args)
    return (res[0], res[1]) if has_next else (res[0], None)


def kernel(x, c, ctx, c_ctx, w_ada, b_ada, w_in, b_igate, b_fgate, mh_norm_g, conv_w, conv_b, w_out,
           ln1_g, ln1_b, w_router, b_router, w1, w3, w2, ln2_g, ln2_b):
    cond_raw = jnp.zeros((8, D_MODEL), f32).at[:BATCH].set(c).at[BATCH].set(c_ctx)
    mod = _ada(cond_raw, w_ada, b_ada).reshape(DEPTH, 8, 6, D_MODEL)

    xa, h = _prep(x.reshape(N_LAT, D_MODEL), ctx.reshape(N_CTX, D_MODEL), mod[0])
    w_router_p = jnp.zeros((D_MODEL, 128), f32).at[:, :N_EXPERTS].set(w_router)
    w_router_hi = w_router_p.astype(bf16)
    w_router_lo = (w_router_p - w_router_hi.astype(f32)).astype(bf16)
    w_router_p = jnp.concatenate([w_router_hi, w_router_lo], axis=1)
    b_router_c = b_router.reshape(N_EXPERTS, 1)
    w_in_t = jnp.swapaxes(w_in, 1, 2)
    w_outb = w_out.astype(bf16)

    for l in range(DEPTH):
        last = l == DEPTH - 1
        proj, gates = _inproj(h, w_in_t, l)

        bias = jnp.concatenate([b_igate[l], b_fgate[l]]).astype(f32)
        m_lat, m_ctx, (w1b, w3b, w2b) = _mlstm3(proj, gates, bias, mh_norm_g[l].reshape(1, D_MLSTM),
                                                w1, w3, w2, not last, l)

        xn, hp, idx8, wts = _mix(xa, mod[l], m_lat, m_ctx, proj, conv_w[l], conv_b[l].reshape(1, D_CONV),
                                 w_outb, ln1_g[l].reshape(1, D_MODEL),
                                 ln1_b[l].reshape(1, D_MODEL), w_router_p, b_router_c, not last, l)

        pos, te, n_used = _route(idx8, xn.shape[0])
        y = _experts(te, n_used, pos, hp, w1b, w3b, w2b)
        xa, h = _combine(pos, xn, mod[l], wts, ln2_g[l].reshape(1, D_MODEL), ln2_b[l].reshape(1, D_MODEL), y,
                         None if last else mod[l + 1])

    return xa.reshape(BATCH, SEQ, D_MODEL)
```

```python
import functools

import jax
import jax.numpy as jnp
from jax import lax
from jax.experimental import pallas as pl
from jax.experimental.pallas import tpu as pltpu

f32 = jnp.float32
bf16 = jnp.bfloat16
i32 = jnp.int32

D_MODEL = 2048
BATCH = 4
SEQ = 2048
DEPTH = 4
GRID_W = 64
CTX_LEN = 256
D_MLSTM = 1024
HEADS = 4
DV = 256
DQK = 128
D_CONV = 1024
D_CONV_H = 512
N_EXPERTS = 16
N_GROUPS = 4
EPG = 4
D_FF = 1024
ALPHA = (2 * DEPTH) ** 0.25
LN_EPS = 1e-6
QK_SCALE = DQK ** -0.5

N_LAT = BATCH * SEQ
N_CTX = BATCH * CTX_LEN
N_ALL = N_LAT + N_CTX
D_PROJ = 6144
N_GATE = 16

TM_IN = 3072
TN_IN = 512
TM_MIX = 256
TM_EXP = 256
TM_CMB = 512
VMEM_LIMIT = 56 * 1024 * 1024

HIGHEST = lax.Precision.HIGHEST


def _sigmoid(x):
    return 1.0 / (1.0 + jnp.exp(-x))


def _log_sigmoid(x):
    return jnp.minimum(x, 0.0) - jnp.log1p(jnp.exp(-jnp.abs(x)))


def _ln(x):
    mu = jnp.mean(x, axis=-1, keepdims=True)
    xc = x - mu
    var = jnp.mean(xc * xc, axis=-1, keepdims=True)
    return xc * lax.rsqrt(var + LN_EPS)


def _mod_row(i, tm):
    return jnp.minimum((i * tm) // SEQ, BATCH)


def _params(sem, vmem=VMEM_LIMIT):
    return pltpu.CompilerParams(dimension_semantics=sem, vmem_limit_bytes=vmem)


def _ada_kernel(c_ref, w_ref, b_ref, o_ref):
    c = c_ref[...]
    cond = c * _sigmoid(c)
    o_ref[...] = jnp.dot(cond, w_ref[...], preferred_element_type=f32) + b_ref[...]


def _ada(cond_raw, w_ada, b_ada):
    tn = 1024
    n = 6 * D_MODEL
    return pl.pallas_call(
        _ada_kernel,
        grid=(DEPTH, n // tn),
        in_specs=[
            pl.BlockSpec((8, D_MODEL), lambda l, j: (0, 0)),
            pl.BlockSpec((None, D_MODEL, tn), lambda l, j: (l, 0, j)),
            pl.BlockSpec((None, 1, tn), lambda l, j: (l, 0, j)),
        ],
        out_specs=pl.BlockSpec((None, 8, tn), lambda l, j: (l, 0, j)),
        out_shape=jax.ShapeDtypeStruct((DEPTH, 8, n), f32),
        compiler_params=_params(("parallel", "parallel")),
        name="ada",
    )(cond_raw, w_ada, b_ada.reshape(DEPTH, 1, n))


N_STATE = 2 * HEADS * DQK + D_MLSTM
N_STATE_BLOCKS = N_STATE // TN_IN
NT_DIMS = (((1,), (1,)), ((), ()))


def _modulated(x, mod_ref):
    return (_ln(x) * (1.0 + mod_ref[1:2, :]) + mod_ref[0:1, :]).astype(bf16)


def _inproj_kernel(h_ref, w_ref, wg_ref, o_ref, g_ref):
    @pl.when(pl.program_id(1) == 0)
    def _():
        g_ref[...] = lax.dot_general(h_ref[...], wg_ref[...].astype(bf16), NT_DIMS,
                                     preferred_element_type=f32)

    o_ref[...] = lax.dot_general(h_ref[...], w_ref[0].astype(bf16), NT_DIMS,
                                 preferred_element_type=f32).astype(o_ref.dtype)


def _inproj(h, w_in_t, l):
    m = h.shape[0]
    row0 = lambda j: pl.multiple_of(jnp.where(j < N_STATE_BLOCKS, j * TN_IN, j * TN_IN + N_GATE), 8)
    return pl.pallas_call(
        _inproj_kernel,
        grid=(m // TM_IN, D_PROJ // TN_IN),
        in_specs=[
            pl.BlockSpec((TM_IN, D_MODEL), lambda i, j: (i, 0)),
            pl.BlockSpec((pl.Element(1), pl.Element(TN_IN), pl.Element(D_MODEL)),
                         lambda i, j: (l, row0(j), 0)),
            pl.BlockSpec((None, N_GATE, D_MODEL), lambda i, j: (l, N_STATE // N_GATE, 0)),
        ],
        out_specs=[
            pl.BlockSpec((TM_IN, TN_IN), lambda i, j: (i, j)),
            pl.BlockSpec((TM_IN, N_GATE), lambda i, j: (i, 0)),
        ],
        out_shape=[
            jax.ShapeDtypeStruct((m, D_PROJ), bf16),
            jax.ShapeDtypeStruct((m, N_GATE), f32),
        ],
        compiler_params=_params(("parallel", "arbitrary")),
        name="inproj",
    )(h, w_in_t, w_in_t)


def _prep_kernel(x_ref, c_ref, mod_ref, xa_ref, h_ref, *, n_lat_blocks):
    i = pl.program_id(0)

    @pl.when(i < n_lat_blocks)
    def _():
        xa_ref[...] = x_ref[...]

    @pl.when(i >= n_lat_blocks)
    def _():
        xa_ref[...] = c_ref[...]

    h_ref[...] = _modulated(xa_ref[...], mod_ref)


def _prep(x2, c2, mod_l):
    tm = TM_CMB
    nlb = N_LAT // tm
    return pl.pallas_call(
        functools.partial(_prep_kernel, n_lat_blocks=nlb),
        grid=(N_ALL // tm,),
        in_specs=[
            pl.BlockSpec((tm, D_MODEL), lambda i: (jnp.minimum(i, nlb - 1), 0)),
            pl.BlockSpec((tm, D_MODEL), lambda i: (jnp.maximum(i - nlb, 0), 0)),
            pl.BlockSpec((None, 6, D_MODEL), lambda i: (_mod_row(i, tm), 0, 0)),
        ],
        out_specs=[pl.BlockSpec((tm, D_MODEL), lambda i: (i, 0)),
                   pl.BlockSpec((tm, D_MODEL), lambda i: (i, 0))],
        out_shape=[jax.ShapeDtypeStruct((N_ALL, D_MODEL), f32),
                   jax.ShapeDtypeStruct((N_ALL, D_MODEL), bf16)],
        compiler_params=_params(("parallel",)),
        name="prep",
    )(x2, c2, mod_l)


MCHUNK = 128
N_CHUNK_CTX = CTX_LEN // MCHUNK
N_CHUNK_LAT = SEQ // MCHUNK
N_CHUNK = N_CHUNK_CTX + N_CHUNK_LAT
GROUP = N_CHUNK_CTX
N_GROUP_LAT = N_CHUNK_LAT // GROUP
LAT_ROW0 = 8
D_AUG = DV + 128


CVT_STEPS = 2 * N_GROUP_LAT
CVT_ROWS_IN = D_MODEL // CVT_STEPS
CVT_ROWS_OUT = D_FF // CVT_STEPS


def _mlstm3_kernel(bias_ref, ql_ref, kl_ref, vl_ref, ol_ref, qc_ref, kc_ref, vc_ref, oc_ref,
                   gcl_ref, grl_ref, gcc_ref, grc_ref, gain_ref, w1_hbm, w3_hbm, w2_hbm, *rest,
                   has_ctx_out, layer):
    if has_ctx_out:
        ml_ref, mc_ref = rest[:2]
        rest = rest[2:]
    else:
        ml_ref, mc_ref = rest[0], None
        rest = rest[1:]
    w1b_hbm, w3b_hbm, w2b_hbm, s_ref, st_ref, msc, rows_s, cols_s = rest[:8]
    cin = rest[8:11]
    cout = rest[11:14]
    csem_in, csem_out = rest[14:16]
    L = MCHUNK
    b = pl.program_id(0)
    h = pl.program_id(1)
    expert = b * HEADS + h
    cvt_src = (w1_hbm, w3_hbm, w2_hbm)
    cvt_dst = (w1b_hbm, w3b_hbm, w2b_hbm)
    cvt_rows = (CVT_ROWS_IN, CVT_ROWS_IN, CVT_ROWS_OUT)

    def cvt_in(k, t):
        r0 = pl.multiple_of(t * cvt_rows[k], cvt_rows[k])
        return pltpu.make_async_copy(cvt_src[k].at[layer, expert, pl.ds(r0, cvt_rows[k]), :],
                                     cin[k].at[t % 2], csem_in.at[k, t % 2])

    def cvt_out(k, t):
        r0 = pl.multiple_of(t * cvt_rows[k], cvt_rows[k])
        return pltpu.make_async_copy(cout[k].at[t % 2], cvt_dst[k].at[expert, pl.ds(r0, cvt_rows[k]), :],
                                     csem_out.at[k, t % 2])

    def cvt_step(t):
        for k in range(3):
            cvt_in(k, t).wait()

            @pl.when(t >= 2)
            def _():
                cvt_out(k, t - 2).wait()

            cout[k][t % 2] = cin[k][t % 2].astype(bf16)
            cvt_out(k, t).start()

            @pl.when(t + 2 < CVT_STEPS)
            def _():
                cvt_in(k, t + 2).start()

    for k in range(3):
        cvt_in(k, 0).start()
        cvt_in(k, 1).start()
    bi = (bias_ref[h], bias_ref[HEADS + h])
    bf = (bias_ref[2 * HEADS + h], bias_ref[3 * HEADS + h])

    rr = lax.broadcasted_iota(i32, (L, L), 0)
    cc = lax.broadcasted_iota(i32, (L, L), 1)
    lo_mask = rr >= cc
    up_mask = rr <= cc
    masks = (lo_mask, up_mask)

    def row_forms(gr_ref, sl, dst0, n):
        for d in range(2):
            tri = (up_mask if d == 0 else lo_mask).astype(f32)
            i_r = gr_ref[d, sl, :] + bi[d]
            lf_r = _log_sigmoid(gr_ref[2 + d, sl, :] + bf[d])
            b_r = jnp.dot(lf_r, tri, precision=HIGHEST, preferred_element_type=f32)
            rows_s[d, dst0:dst0 + n, :] = i_r - b_r

    row_forms(grc_ref, pl.ds(b * N_CHUNK_CTX, N_CHUNK_CTX), 0, N_CHUNK_CTX)
    row_forms(grl_ref, slice(None), LAT_ROW0, N_CHUNK_LAT)

    kind = lax.broadcasted_iota(i32, (1, 4 * GROUP), 1) % 4
    bias_v = jnp.where(kind == 0, bi[0], jnp.where(kind == 1, bi[1], jnp.where(kind == 2, bf[0], bf[1])))
    tpos = lax.broadcasted_iota(i32, (L, 4 * GROUP), 0)

    def col_forms(x):
        y = x + bias_v
        y = jnp.where(kind >= 2, _log_sigmoid(y), y)
        pre = y
        suf = y
        s = 1
        while s < L:
            pre = pre + jnp.where(tpos >= s, pltpu.roll(pre, s, 0), 0.0)
            suf = suf + jnp.where(tpos < L - s, pltpu.roll(suf, L - s, 0), 0.0)
            s *= 2
        return jnp.where(kind == 2, pre, jnp.where(kind == 3, suf, y))

    cols_s[0] = col_forms(gcc_ref[...])

    def col_body(g, c):
        cols_s[g + 1] = col_forms(gcl_ref[g])
        return c

    lax.fori_loop(0, N_GROUP_LAT, col_body, 0)

    ones_col = (lax.broadcasted_iota(i32, (L, 128), 1) == 0).astype(bf16)

    s_ref[...] = jnp.zeros_like(s_ref)

    def state_step(d, c, row, col, j, k, v, m):
        i_c = col[:, 4 * j + d:4 * j + d + 1]
        b1 = col[:, 4 * j + 2 + d:4 * j + 3 + d]
        last = L - 1 if d == 0 else 0
        b_last = b1[last:last + 1, :]
        ct = i_c - b1
        mx = jnp.max(ct, axis=0, keepdims=True)
        wl = jnp.exp(ct - mx)
        mloc_last = b_last + mx
        m_new = jnp.maximum(b_last + m, mloc_last)
        decay = jnp.exp(b_last + m - m_new)
        a_l = jnp.exp(mloc_last - m_new)
        msc[d, pl.ds(row, 1), :] = jnp.broadcast_to(m, (1, 128))
        kwt = (k.astype(f32) * (wl * QK_SCALE)).astype(bf16).T
        for cb in range(D_AUG // 128):
            cs = slice(cb * 128, (cb + 1) * 128)
            s_old = s_ref[d, :, cs]
            st_ref[d, c, :, cs] = s_old.astype(bf16)
            rhs = v[:, cs] if cb < DV // 128 else ones_col
            s_ref[d, :, cs] = decay * s_old + a_l * jnp.dot(kwt, rhs, preferred_element_type=f32)
        return m_new

    zero = jnp.zeros((1, 1), f32)
    m_f = m_b = zero
    col0 = cols_s[0]
    for step in range(GROUP):
        jf, jb = step, GROUP - 1 - step
        m_f = state_step(0, jf, jf, col0, jf, kc_ref[jf * L:(jf + 1) * L, :], vc_ref[jf * L:(jf + 1) * L, :], m_f)
        m_b = state_step(1, jb, jb, col0, jb, kc_ref[jb * L:(jb + 1) * L, :], vc_ref[jb * L:(jb + 1) * L, :], m_b)

    def state_body(it, carry):
        m_f, m_b = carry
        gf = it
        gb = N_GROUP_LAT + 1 - it
        colf = cols_s[gf]
        colb = cols_s[gb]
        for step in range(GROUP):
            jf, jb = step, GROUP - 1 - step
            clf = (gf - 1) * GROUP + jf
            clb = (gb - 1) * GROUP + jb
            rf = pl.multiple_of(clf * L, L)
            rb = pl.multiple_of(clb * L, L)
            m_f = state_step(0, N_CHUNK_CTX + clf, LAT_ROW0 + clf, colf, jf,
                             kl_ref[pl.ds(rf, L), :], vl_ref[pl.ds(rf, L), :], m_f)
            m_b = state_step(1, N_CHUNK_CTX + clb, LAT_ROW0 + clb, colb, jb,
                             kl_ref[pl.ds(rb, L), :], vl_ref[pl.ds(rb, L), :], m_b)
        cvt_step(it - 1)
        return m_f, m_b

    lax.fori_loop(1, N_GROUP_LAT + 1, state_body, (m_f, m_b))

    gain = gain_ref[...]

    def out_chunk(q, k, v, o, col, j, c, row, out_ref, r0):
        qk = lax.dot_general(q, k, (((1,), (1,)), ((), ())), preferred_element_type=f32) * QK_SCALE
        vaug = jnp.concatenate([v, ones_col], axis=1)
        sl = []
        per = []
        for d in range(2):
            b1 = col[:, 4 * j + 2 + d:4 * j + 3 + d]
            dm = jnp.where(masks[d], b1 + rows_s[d, pl.ds(row, 1), :], -jnp.inf)
            mloc = jnp.max(dm, axis=1, keepdims=True)
            wloc = jnp.exp(dm - mloc)
            m_prev = msc[d, pl.ds(row, 1), :][:, 0:1]
            m_t = jnp.maximum(b1 + m_prev, mloc)
            inter = jnp.exp(b1 + m_prev - m_t)
            a = jnp.exp(mloc - m_t)
            sl.append((qk * wloc).astype(bf16))
            per.append((m_t, inter, a))
        x = jnp.dot(jnp.concatenate(sl, axis=0), vaug, preferred_element_type=f32)
        hs = None
        for d in range(2):
            y = jnp.dot(q, st_ref[d, c], preferred_element_type=f32)
            m_t, inter, a = per[d]
            xd = x[d * L:(d + 1) * L, :]
            den = inter * y[:, DV:DV + 1] + a * xd[:, DV:DV + 1]
            rinv = 1.0 / jnp.maximum(jnp.abs(den), jnp.exp(-m_t))
            hd = (inter * rinv) * y[:, :DV] + (a * rinv) * xd[:, :DV]
            hs = hd if hs is None else hs + hd
        r = lax.rsqrt(jnp.mean(hs * hs, axis=-1, keepdims=True) + LN_EPS)
        out_ref[pl.ds(r0, L), :] = (hs * r * gain * _sigmoid(o.astype(f32))).astype(out_ref.dtype)

    if has_ctx_out:
        for j in range(GROUP):
            sl_ = slice(j * L, (j + 1) * L)
            out_chunk(qc_ref[sl_, :], kc_ref[sl_, :], vc_ref[sl_, :], oc_ref[sl_, :], col0, j, j, j,
                      mc_ref, j * L)

    def out_body(g2, carry):
        for gg in range(2):
            g = 2 * g2 + gg
            col = cols_s[g + 1]
            for j in range(GROUP):
                cl = g * GROUP + j
                r0 = pl.multiple_of(cl * L, L)
                out_chunk(ql_ref[pl.ds(r0, L), :], kl_ref[pl.ds(r0, L), :], vl_ref[pl.ds(r0, L), :],
                          ol_ref[pl.ds(r0, L), :], col, j, N_CHUNK_CTX + cl, LAT_ROW0 + cl, ml_ref, r0)
        cvt_step(N_GROUP_LAT + 2 * g2)
        cvt_step(N_GROUP_LAT + 2 * g2 + 1)
        return carry

    lax.fori_loop(0, N_GROUP_LAT // 2, out_body, 0)
    for k in range(3):
        cvt_out(k, CVT_STEPS - 2).wait()
        cvt_out(k, CVT_STEPS - 1).wait()


def _mlstm3(proj, gates, bias, gain, w1, w3, w2, has_ctx_out, layer):
    L = MCHUNK
    g = gates[:, :N_GATE]
    g4 = jnp.stack([g[:, 0:4], g[:, 4:8], g[:, 8:12], g[:, 12:16]], axis=-1)
    gh = jnp.transpose(g4, (1, 0, 2))
    gcol_l = gh[:, :N_LAT].reshape(HEADS, BATCH * N_GROUP_LAT, GROUP, L, 4)
    gcol_l = jnp.transpose(gcol_l, (0, 1, 3, 2, 4)).reshape(HEADS, BATCH * N_GROUP_LAT, L, 4 * GROUP)
    gcol_c = gh[:, N_LAT:].reshape(HEADS, BATCH, GROUP, L, 4)
    gcol_c = jnp.transpose(gcol_c, (0, 1, 3, 2, 4)).reshape(HEADS, BATCH, L, 4 * GROUP)
    gr = jnp.transpose(g4, (1, 2, 0))
    grow_l = gr[:, :, :N_LAT].reshape(HEADS, 4, N_LAT // L, L)
    grow_c = gr[:, :, N_LAT:].reshape(HEADS, 4, N_CTX // L, L)

    ctx_rb = lambda b, h: N_LAT // CTX_LEN + b
    in_specs = [
        pl.BlockSpec(memory_space=pltpu.SMEM),
        pl.BlockSpec((SEQ, DQK), lambda b, h: (b, h)),
        pl.BlockSpec((SEQ, DQK), lambda b, h: (b, HEADS + h)),
        pl.BlockSpec((SEQ, DV), lambda b, h: (b, HEADS + h)),
        pl.BlockSpec((SEQ, DV), lambda b, h: (b, 2 * HEADS + h)),
        pl.BlockSpec((CTX_LEN, DQK), lambda b, h: (ctx_rb(b, h), h)),
        pl.BlockSpec((CTX_LEN, DQK), lambda b, h: (ctx_rb(b, h), HEADS + h)),
        pl.BlockSpec((CTX_LEN, DV), lambda b, h: (ctx_rb(b, h), HEADS + h)),
        pl.BlockSpec((CTX_LEN, DV), lambda b, h: (ctx_rb(b, h), 2 * HEADS + h)),
        pl.BlockSpec((None, N_GROUP_LAT, L, 4 * GROUP), lambda b, h: (h, b, 0, 0)),
        pl.BlockSpec((None, 4, N_CHUNK_LAT, L), lambda b, h: (h, 0, b, 0)),
        pl.BlockSpec((None, None, L, 4 * GROUP), lambda b, h: (h, b, 0, 0)),
        pl.BlockSpec((None, 4, N_CTX // L, L), lambda b, h: (h, 0, 0, 0)),
        pl.BlockSpec((1, DV), lambda b, h: (0, h)),
        pl.BlockSpec(memory_space=pl.ANY),
        pl.BlockSpec(memory_space=pl.ANY),
        pl.BlockSpec(memory_space=pl.ANY),
    ]
    out_specs = [pl.BlockSpec((SEQ, DV), lambda b, h: (b, h))]
    out_shape = [jax.ShapeDtypeStruct((N_LAT, D_MLSTM), bf16)]
    if has_ctx_out:
        out_specs.append(pl.BlockSpec((CTX_LEN, DV), lambda b, h: (b, h)))
        out_shape.append(jax.ShapeDtypeStruct((N_CTX, D_MLSTM), bf16))
    out_specs += [pl.BlockSpec(memory_space=pl.ANY)] * 3
    out_shape += [jax.ShapeDtypeStruct((N_EXPERTS, D_MODEL, D_FF), bf16),
                  jax.ShapeDtypeStruct((N_EXPERTS, D_MODEL, D_FF), bf16),
                  jax.ShapeDtypeStruct((N_EXPERTS, D_FF, D_MODEL), bf16)]
    scratch = [
        pltpu.VMEM((2, DQK, D_AUG), f32),
        pltpu.VMEM((2, N_CHUNK, DQK, D_AUG), bf16),
        pltpu.VMEM((2, LAT_ROW0 + N_CHUNK_LAT, 128), f32),
        pltpu.VMEM((2, LAT_ROW0 + N_CHUNK_LAT, L), f32),
        pltpu.VMEM((N_GROUP_LAT + 1, L, 4 * GROUP), f32),
        pltpu.VMEM((2, CVT_ROWS_IN, D_FF), f32),
        pltpu.VMEM((2, CVT_ROWS_IN, D_FF), f32),
        pltpu.VMEM((2, CVT_ROWS_OUT, D_MODEL), f32),
        pltpu.VMEM((2, CVT_ROWS_IN, D_FF), bf16),
        pltpu.VMEM((2, CVT_ROWS_IN, D_FF), bf16),
        pltpu.VMEM((2, CVT_ROWS_OUT, D_MODEL), bf16),
        pltpu.SemaphoreType.DMA((3, 2)),
        pltpu.SemaphoreType.DMA((3, 2)),
    ]
    assert N_EXPERTS == BATCH * HEADS
    res = pl.pallas_call(
        functools.partial(_mlstm3_kernel, has_ctx_out=has_ctx_out, layer=layer),
        grid=(BATCH, HEADS),
        in_specs=in_specs,
        out_specs=out_specs,
        out_shape=out_shape,
        scratch_shapes=scratch,
        compiler_params=_params(("arbitrary", "arbitrary")),
        name="mlstm",
    )(bias, proj, proj, proj, proj, proj, proj, proj, proj, gcol_l, grow_l, gcol_c, grow_c, gain,
      w1, w3, w2)
    if has_ctx_out:
        return res[0], res[1], res[2:]
    return res[0], None, res[1:]


def _top2_rows(vals):
    best = vals[0]
    bi = jnp.zeros(best.shape, i32)
    for j in range(1, len(vals)):
        take = vals[j] > best
        best = jnp.where(take, vals[j], best)
        bi = jnp.where(take, j, bi)
    sec = None
    si = None
    for j in range(len(vals)):
        cand = jnp.where(bi == j, -jnp.inf, vals[j])
        if sec is None:
            sec, si = cand, jnp.zeros(best.shape, i32)
        else:
            take = cand > sec
            sec = jnp.where(take, cand, sec)
            si = jnp.where(take, j, si)
    return bi, si


def _mix_kernel(x_ref, mod_ref, ml_ref, mc_ref, u_ref, bg_ref, cg_ref, ut_ref, ct_ref, ub_ref, cb_ref,
                cw_ref, cbias_ref, wo_ref, g1_ref, b1_ref, wr_ref, br_ref,
                xo_ref, hp_ref, idx_ref, wt_ref, m_scr, y_scr, *, n_lat_blocks, has_ctx):
    tm = TM_MIX
    i = pl.program_id(0)
    gate = mod_ref[2:3, :]
    shift2 = mod_ref[3:4, :]
    scale2 = mod_ref[4:5, :]
    cw = cw_ref[...]
    cbias = cbias_ref[...]
    row = lax.broadcasted_iota(i32, (tm, 1), 0)

    def shifted(z, first, last):
        prev = jnp.where(first, 0.0, pltpu.roll(z, 1, 0))
        nxt = jnp.where(last, 0.0, pltpu.roll(z, tm - 1, 0))
        return prev, nxt

    def lat_branch():
        z = cg_ref[...].astype(f32) * u_ref[...].astype(f32)
        col = row % GRID_W
        zh = z[:, :D_CONV_H]
        prev, nxt = shifted(zh, col == 0, col == GRID_W - 1)
        yh = cw[0:1, :D_CONV_H] * prev + cw[1:2, :D_CONV_H] * zh + cw[2:3, :D_CONV_H] * nxt
        bpb = SEQ // tm
        top_ok = (i % bpb != 0).astype(f32)
        bot_ok = (i % bpb != bpb - 1).astype(f32)
        zt = ct_ref[...].astype(f32) * ut_ref[...].astype(f32) * top_ok
        zb = cb_ref[...].astype(f32) * ub_ref[...].astype(f32) * bot_ok
        zv = z[:, D_CONV_H:]
        zext = jnp.concatenate([zt, zv, zb], axis=0)
        yv = cw[0:1, D_CONV_H:] * zext[0:tm] + cw[1:2, D_CONV_H:] * zv \
            + cw[2:3, D_CONV_H:] * zext[2 * GRID_W:2 * GRID_W + tm]
        y = jnp.concatenate([yh, yv], axis=1) + cbias
        y_scr[...] = (bg_ref[...].astype(f32) * y).astype(bf16)
        m_scr[...] = ml_ref[...]

    def ctx_branch():
        z = cg_ref[...].astype(f32) * u_ref[...].astype(f32)
        pos = row % CTX_LEN
        prev, nxt = shifted(z, pos == 0, pos == CTX_LEN - 1)
        y = cw[0:1, :] * prev + cw[1:2, :] * z + cw[2:3, :] * nxt + cbias
        y_scr[...] = (bg_ref[...].astype(f32) * y).astype(bf16)
        m_scr[...] = mc_ref[...]

    if has_ctx:
        pl.when(i < n_lat_blocks)(lat_branch)
        pl.when(i >= n_lat_blocks)(ctx_branch)
    else:
        lat_branch()

    out = jnp.dot(m_scr[...], wo_ref[0:D_MLSTM, :], preferred_element_type=f32) \
        + jnp.dot(y_scr[...], wo_ref[D_MLSTM:, :], preferred_element_type=f32)
    xn = _ln(ALPHA * x_ref[...] + gate * out) * g1_ref[...] + b1_ref[...]
    xo_ref[...] = xn
    h2 = _ln(xn) * (1.0 + scale2) + shift2

    hp_ref[...] = h2

    h_hi = h2.astype(bf16)
    hs = jnp.concatenate([h_hi, (h2 - h_hi.astype(f32)).astype(bf16)], axis=0)
    pr = jnp.dot(hs, wr_ref[...], preferred_element_type=f32)
    logits = pr[:tm, :128] + (pr[:tm, 128:] + pr[tm:, :128])
    lt = logits.T
    s = _sigmoid(lt[0:N_EXPERTS, :])
    sb = s + br_ref[...]
    sb_rows = [sb[e:e + 1, :] for e in range(N_EXPERTS)]
    s_rows = [s[e:e + 1, :] for e in range(N_EXPERTS)]
    gscores = []
    for g in range(N_GROUPS):
        a_, b_, c_, d_ = sb_rows[EPG * g:EPG * g + EPG]
        hi1, lo1 = jnp.maximum(a_, b_), jnp.minimum(a_, b_)
        hi2, lo2 = jnp.maximum(c_, d_), jnp.minimum(c_, d_)
        top = jnp.maximum(hi1, hi2)
        second = jnp.maximum(jnp.minimum(hi1, hi2), jnp.maximum(lo1, lo2))
        gscores.append(top + second)
    gbest = gscores[0]
    gsel = jnp.zeros(gbest.shape, i32)
    for g in range(1, N_GROUPS):
        take = gscores[g] > gbest
        gbest = jnp.where(take, gscores[g], gbest)
        gsel = jnp.where(take, g, gsel)

    def pick_group(rows, j):
        v = rows[j]
        for g in range(1, N_GROUPS):
            v = jnp.where(gsel == g, rows[EPG * g + j], v)
        return v

    cand_b = [pick_group(sb_rows, j) for j in range(EPG)]
    cand_s = [pick_group(s_rows, j) for j in range(EPG)]
    i1, i2 = _top2_rows(cand_b)

    def pick_idx(rows, idx):
        v = rows[0]
        for j in range(1, EPG):
            v = jnp.where(idx == j, rows[j], v)
        return v

    s1 = pick_idx(cand_s, i1)
    s2 = pick_idx(cand_s, i2)
    tot = s1 + s2
    w1 = s1 / tot
    w2 = s2 / tot
    e1 = gsel * EPG + i1
    e2 = gsel * EPG + i2
    r8 = lax.broadcasted_iota(i32, (8, tm), 0)
    idx_ref[...] = jnp.where(r8 == 0, e1, jnp.where(r8 == 1, e2, 0))
    r128 = lax.broadcasted_iota(i32, (128, tm), 0)
    wmat = jnp.where(r128 == 0, w1, jnp.where(r128 == 1, w2, 0.0))
    wt_ref[...] = wmat.T


def _mix(x, mod_l, m_lat, m_ctx, proj, conv_w, conv_b, w_out, ln_g, ln_b, w_router, b_router, has_ctx, l):
    tm = TM_MIX
    m_rows = N_ALL if has_ctx else N_LAT
    nlb = N_LAT // tm
    nblocks = m_rows // tm
    hb = tm // GRID_W
    n_hblocks = N_ALL // GRID_W
    if m_ctx is None:
        m_ctx = m_lat
    ncb = m_ctx.shape[0] // tm
    in_specs = [
        pl.BlockSpec((tm, D_MODEL), lambda i: (i, 0)),
        pl.BlockSpec((None, 6, D_MODEL), lambda i: (_mod_row(i, tm), 0, 0)),
        pl.BlockSpec((tm, D_MLSTM), lambda i: (jnp.minimum(i, nlb - 1), 0)),
        pl.BlockSpec((tm, D_MLSTM), lambda i: (jnp.clip(i - nlb, 0, ncb - 1), 0)),
        pl.BlockSpec((tm, D_CONV), lambda i: (i, 3)),
        pl.BlockSpec((tm, D_CONV), lambda i: (i, 4)),
        pl.BlockSpec((tm, D_CONV), lambda i: (i, 5)),
        pl.BlockSpec((GRID_W, D_CONV_H), lambda i: (jnp.maximum(i * hb - 1, 0), 7)),
        pl.BlockSpec((GRID_W, D_CONV_H), lambda i: (jnp.maximum(i * hb - 1, 0), 11)),
        pl.BlockSpec((GRID_W, D_CONV_H), lambda i: (jnp.minimum((i + 1) * hb, n_hblocks - 1), 7)),
        pl.BlockSpec((GRID_W, D_CONV_H), lambda i: (jnp.minimum((i + 1) * hb, n_hblocks - 1), 11)),
        pl.BlockSpec((3, D_CONV), lambda i: (0, 0)),
        pl.BlockSpec((1, D_CONV), lambda i: (0, 0)),
        pl.BlockSpec((None, D_MODEL, D_MODEL), lambda i: (l, 0, 0)),
        pl.BlockSpec((1, D_MODEL), lambda i: (0, 0)),
        pl.BlockSpec((1, D_MODEL), lambda i: (0, 0)),
        pl.BlockSpec((D_MODEL, 256), lambda i: (0, 0)),
        pl.BlockSpec((N_EXPERTS, 1), lambda i: (0, 0)),
    ]
    out_specs = [
        pl.BlockSpec((tm, D_MODEL), lambda i: (i, 0)),
        pl.BlockSpec((tm, D_MODEL), lambda i: (i, 0)),
        pl.BlockSpec((8, tm), lambda i: (0, i)),
        pl.BlockSpec((tm, 128), lambda i: (i, 0)),
    ]
    out_shape = [
        jax.ShapeDtypeStruct((m_rows, D_MODEL), f32),
        jax.ShapeDtypeStruct((m_rows, D_MODEL), f32),
        jax.ShapeDtypeStruct((8, m_rows), i32),
        jax.ShapeDtypeStruct((m_rows, 128), f32),
    ]
    return pl.pallas_call(
        functools.partial(_mix_kernel, n_lat_blocks=nlb, has_ctx=has_ctx),
        grid=(nblocks,),
        in_specs=in_specs,
        out_specs=out_specs,
        out_shape=out_shape,
        scratch_shapes=[pltpu.VMEM((tm, D_MLSTM), bf16), pltpu.VMEM((tm, D_CONV), bf16)],
        compiler_params=_params(("parallel",)),
        name="mix",
    )(x, mod_l, m_lat, m_ctx, proj, proj, proj, proj, proj, proj, proj,
      conv_w, conv_b, w_out, ln_g, ln_b, w_router, b_router)


def _route_kernel(ef_ref, pos_ref, meta_ref, *, rows):
    R = rows
    ef = ef_ref[...]
    li = lax.broadcasted_iota(i32, (128, 128), 0)
    lj = lax.broadcasted_iota(i32, (128, 128), 1)
    strict_up = (li < lj).astype(bf16)
    ri = lax.broadcasted_iota(i32, (R, R), 0)
    rj = lax.broadcasted_iota(i32, (R, R), 1)
    strict_lo = (rj < ri).astype(bf16)
    lane16 = lax.broadcasted_iota(i32, (R, N_EXPERTS), 1)

    ohs = [(ef == e) for e in range(N_EXPERTS)]
    within = [jnp.dot(oh.astype(bf16), strict_up, preferred_element_type=f32) for oh in ohs]
    rt = jnp.zeros((R, N_EXPERTS), f32)
    for e in range(N_EXPERTS):
        rt = jnp.where(lane16 == e, jnp.sum(ohs[e].astype(f32), axis=1, keepdims=True), rt)
    rp = jnp.dot(strict_lo, rt.astype(bf16), preferred_element_type=f32)
    counts = rp[R - 1:R, :] + rt[R - 1:R, :]
    ntile = jnp.floor((counts + (TM_EXP - 1)) * (1.0 / TM_EXP))
    ei = lax.broadcasted_iota(i32, (N_EXPERTS, N_EXPERTS), 0)
    ej = lax.broadcasted_iota(i32, (N_EXPERTS, N_EXPERTS), 1)
    tend = jnp.dot(ntile.astype(bf16), (ei <= ej).astype(bf16), preferred_element_type=f32)
    off = rp + (tend - ntile) * float(TM_EXP)
    pos = jnp.zeros((R, 128), f32)
    for e in range(N_EXPERTS):
        pos = jnp.where(ohs[e], within[e] + off[:, e:e + 1], pos)
    pos_ref[...] = pos.astype(i32)

    n_used = tend[:, N_EXPERTS - 1:N_EXPERTS]
    tile = jnp.minimum(lax.broadcasted_iota(i32, (1, 128), 1).astype(f32), n_used - 1.0)
    te = jnp.zeros((1, 128), f32)
    for e in range(N_EXPERTS):
        te = te + (tend[:, e:e + 1] <= tile).astype(f32)
    te = jnp.minimum(te, float(N_EXPERTS - 1))
    r8 = lax.broadcasted_iota(i32, (8, 128), 0)
    meta_ref[...] = jnp.where(r8 == 0, te, jnp.where(r8 == 1, n_used, 0.0)).astype(i32)


def _route(idx8, m_rows):
    rows = 2 * m_rows // 128
    n_tiles = 2 * m_rows // TM_EXP + N_EXPERTS
    assert n_tiles <= 128
    pos, meta = pl.pallas_call(
        functools.partial(_route_kernel, rows=rows),
        out_shape=[jax.ShapeDtypeStruct((rows, 128), i32), jax.ShapeDtypeStruct((8, 128), i32)],
        name="route",
    )(idx8[:2, :].reshape(rows, 128))
    return pos.reshape(-1), meta[0, :n_tiles], meta[1, 0:1]


def _expert_kernel(te_ref, nu_ref, pos_ref, hp_ref, zeros_ref, w1_ref, w3_ref, w2_ref, y_ref, xbuf, sem,
                   src_s, zsem, *, m_rows):
    tm = TM_EXP
    i = pl.program_id(0)
    n_used = nu_ref[0]
    n_rows = src_s.shape[0]

    def issue(tile, slot):
        base = tile * tm
        for j in range(tm):
            t = src_s[base + j]
            pltpu.make_async_copy(hp_ref.at[pl.ds(t, 1), :], xbuf.at[slot, pl.ds(j, 1), :],
                                  sem.at[slot]).start()

    def wait(slot):
        pltpu.make_async_copy(hp_ref.at[pl.ds(0, tm), :], xbuf.at[slot], sem.at[slot]).wait()

    @pl.when(i == 0)
    def _():
        zfill = pltpu.make_async_copy(zeros_ref, src_s, zsem)
        zfill.start()
        zfill.wait()

        def scatter(t, c):
            src_s[pos_ref[t]] = t
            src_s[pos_ref[m_rows + t]] = t
            return c

        lax.fori_loop(0, m_rows, scatter, 0, unroll=16)
        issue(0, 0)

    @pl.when(i < n_used)
    def _():
        slot = i % 2
        issue(i + 1, 1 - slot)
        wait(slot)
        xb = xbuf[slot].astype(bf16)
        a1 = jnp.dot(xb, w1_ref[...], preferred_element_type=f32)
        a3 = jnp.dot(xb, w3_ref[...], preferred_element_type=f32)
        act = (a1 * _sigmoid(a1) * a3).astype(bf16)
        y_ref[...] = jnp.dot(act, w2_ref[...], preferred_element_type=f32)

        @pl.when(i + 1 == n_used)
        def _():
            wait(1 - slot)

    @pl.when(i >= n_used)
    def _():
        y_ref[...] = jnp.zeros_like(y_ref)


def _experts(te, n_used, pos, hp, w1, w3, w2):
    tm = TM_EXP
    m_rows = hp.shape[0]
    n_tiles = te.shape[0]
    wspec = lambda r, c: pl.BlockSpec((None, r, c), lambda i, te, nu, pos: (te[i], 0, 0))
    return pl.pallas_call(
        functools.partial(_expert_kernel, m_rows=m_rows),
        grid_spec=pltpu.PrefetchScalarGridSpec(
            num_scalar_prefetch=3,
            grid=(n_tiles,),
            in_specs=[
                pl.BlockSpec(memory_space=pl.ANY),
                pl.BlockSpec(memory_space=pl.ANY),
                wspec(D_MODEL, D_FF),
                wspec(D_MODEL, D_FF),
                wspec(D_FF, D_MODEL),
            ],
            out_specs=pl.BlockSpec((tm, D_MODEL), lambda i, te, nu, pos: (i, 0)),
            scratch_shapes=[pltpu.VMEM((2, tm, D_MODEL), f32), pltpu.SemaphoreType.DMA((2,)),
                            pltpu.SMEM((n_tiles * tm,), i32), pltpu.SemaphoreType.DMA(())],
        ),
        out_shape=jax.ShapeDtypeStruct((n_tiles * tm, D_MODEL), f32),
        compiler_params=_params(("arbitrary",)),
        name="experts",
    )(te, n_used, pos, hp, jnp.zeros((n_tiles * tm,), i32), w1, w3, w2)


def _combine_kernel(pos_ref, x_ref, mod_ref, wt_ref, g_ref, b_ref, *rest, m_rows, nblocks, has_next):
    if has_next:
        modn_ref, y_ref, o_ref, h_ref, buf, sem = rest
    else:
        y_ref, o_ref, buf, sem = rest
    tm = TM_CMB
    i = pl.program_id(0)

    def issue(blk, slot):
        base = blk * tm
        for j in range(tm):
            for k in range(2):
                p = pos_ref[k * m_rows + base + j]
                pltpu.make_async_copy(y_ref.at[pl.ds(p, 1), :], buf.at[slot, k, pl.ds(j, 1), :],
                                      sem.at[slot]).start(priority=k)

    @pl.when(i == 0)
    def _():
        issue(0, 0)

    slot = i % 2
    for k in range(2):
        pltpu.make_async_copy(y_ref.at[pl.ds(0, tm), :], buf.at[slot, k], sem.at[slot]).wait()

    wt = wt_ref[...]
    moe = wt[:, 0:1] * buf[slot, 0] + wt[:, 1:2] * buf[slot, 1]
    gate = mod_ref[5:6, :]
    v = ALPHA * x_ref[...] + gate * moe

    issue(jnp.minimum(i + 1, nblocks - 1), 1 - slot)
    xn = _ln(v) * g_ref[...] + b_ref[...]
    o_ref[...] = xn
    if has_next:
        h_ref[...] = _modulated(xn, modn_ref)

    @pl.when(i + 1 == nblocks)
    def _():
        for k in range(2):
            pltpu.make_async_copy(y_ref.at[pl.ds(0, tm), :], buf.at[1 - slot, k], sem.at[1 - slot]).wait()


def _combine(pos, x, mod_l, wts, ln_g, ln_b, y, mod_next=None):
    tm = TM_CMB
    m_rows = x.shape[0]
    nblocks = m_rows // tm
    has_next = mod_next is not None
    mod_spec = pl.BlockSpec((None, 6, D_MODEL), lambda i, p: (_mod_row(i, tm), 0, 0))
    row_spec = pl.BlockSpec((tm, D_MODEL), lambda i, p: (i, 0))
    vec_spec = pl.BlockSpec((1, D_MODEL), lambda i, p: (0, 0))
    in_specs = [row_spec, mod_spec, pl.BlockSpec((tm, 128), lambda i, p: (i, 0)), vec_spec, vec_spec]
    args = [pos, x, mod_l, wts, ln_g, ln_b]
    out_specs = [row_spec]
    out_shape = [jax.ShapeDtypeStruct((m_rows, D_MODEL), f32)]
    if has_next:
        in_specs.append(mod_spec)
        args.append(mod_next)
        out_specs.append(row_spec)
        out_shape.append(jax.ShapeDtypeStruct((m_rows, D_MODEL), bf16))
    in_specs.append(pl.BlockSpec(memory_space=pl.ANY))
    args.append(y)
    res = pl.pallas_call(
        functools.partial(_combine_kernel, m_rows=m_rows, nblocks=nblocks, has_next=has_next),
        grid_spec=pltpu.PrefetchScalarGridSpec(
            num_scalar_prefetch=1,
            grid=(nblocks,),
            in_specs=in_specs,
            out_specs=out_specs,
            scratch_shapes=[pltpu.VMEM((2, 2, tm, D_MODEL), f32), pltpu.SemaphoreType.DMA((2,))],
        ),
        out_shape=out_shape,
        compiler_params=_params(("arbitrary",)),
        name="combine",
    )(*args)
    return (res[0], res[1]) if has_next else (res[0], None)


def kernel(x, c, ctx, c_ctx, w_ada, b_ada, w_in, b_igate, b_fgate, mh_norm_g, conv_w, conv_b, w_out,
           ln1_g, ln1_b, w_router, b_router, w1, w3, w2, ln2_g, ln2_b):
    cond_raw = jnp.zeros((8, D_MODEL), f32).at[:BATCH].set(c).at[BATCH].set(c_ctx)
    mod = _ada(cond_raw, w_ada, b_ada).reshape(DEPTH, 8, 6, D_MODEL)

    xa, h = _prep(x.reshape(N_LAT, D_MODEL), ctx.reshape(N_CTX, D_MODEL), mod[0])
    w_router_p = jnp.zeros((D_MODEL, 128), f32).at[:, :N_EXPERTS].set(w_router)
    w_router_hi = w_router_p.astype(bf16)
    w_router_lo = (w_router_p - w_router_hi.astype(f32)).astype(bf16)
    w_router_p = jnp.concatenate([w_router_hi, w_router_lo], axis=1)
    b_router_c = b_router.reshape(N_EXPERTS, 1)
    w_in_t = jnp.swapaxes(w_in, 1, 2)
    w_outb = w_out.astype(bf16)

    for l in range(DEPTH):
        last = l == DEPTH - 1
        proj, gates = _inproj(h, w_in_t, l)

        bias = jnp.concatenate([b_igate[l], b_fgate[l]]).astype(f32)
        m_lat, m_ctx, (w1b, w3b, w2b) = _mlstm3(proj, gates, bias, mh_norm_g[l].reshape(1, D_MLSTM),
                                                w1, w3, w2, not last, l)

        xn, hp, idx8, wts = _mix(xa, mod[l], m_lat, m_ctx, proj, conv_w[l], conv_b[l].reshape(1, D_CONV),
                                 w_outb, ln1_g[l].reshape(1, D_MODEL),
                                 ln1_b[l].reshape(1, D_MODEL), w_router_p, b_router_c, not last, l)

        pos, te, n_used = _route(idx8, xn.shape[0])
        y = _experts(te, n_used, pos, hp, w1b, w3b, w2b)
        xa, h = _combine(pos, xn, mod[l], wts, ln2_g[l].reshape(1, D_MODEL), ln2_b[l].reshape(1, D_MODEL), y,
                         None if last else mod[l + 1])

    return xa.reshape(BATCH, SEQ, D_MODEL)
```
